```python
import math
import jax, jax.numpy as jnp
from jax import lax
import numpy as np

D_MODEL = 2048
BATCH = 2
SEQ = 4096
DEPTH = 1
DEC_BATCH = 8
DEC_SEQ = 4
PAST_LEN = 16384
PAGE_SIZE = 128

MIX_W = D_MODEL
DIFF_W = MIX_W // 4
DIFF_H = 4
DIFF_DV = DIFF_W // DIFF_H
DIFF_DH = DIFF_DV // 2
RWKV_W = MIX_W // 2
RWKV_N = 64
RWKV_H = RWKV_W // RWKV_N
CROSS_W = MIX_W // 4
CROSS_H = 4
CROSS_DH = CROSS_W // CROSS_H
N_MEM = 256
DECAY_LORA = max(32, round(1.8 * math.sqrt(RWKV_W) / 32) * 32)
AAA_LORA = max(32, round(1.8 * math.sqrt(RWKV_W) / 32) * 32)
Q_BLOCK = 128
RWKV_GN_EPS = 64e-5

OFF_DQ = 0
OFF_DK = OFF_DQ + DIFF_W
OFF_DV = OFF_DK + DIFF_W
OFF_RWKV = OFF_DV + DIFF_W
RWKV_SHIFT_COLS = 3 * RWKV_W + DECAY_LORA + AAA_LORA
OFF_CQ = OFF_RWKV + RWKV_SHIFT_COLS
OFF_GATE = OFF_CQ + CROSS_W
IN_COLS = OFF_GATE + MIX_W

kernel_name = 'hymba_diffattn_rwkv7_memxattn_step'


def _rmsnorm(x, g, eps=1e-6):
    xf = x.astype(jnp.float32)
    y = xf * lax.rsqrt(jnp.mean(xf * xf, axis=-1, keepdims=True) + eps)
    return (y * g.astype(jnp.float32)).astype(x.dtype)


def _alibi_slopes(n):
    return 2.0 ** (-8.0 * jnp.arange(1, n + 1, dtype=jnp.float32) / n)


def _halves(k):
    return k.reshape(k.shape[:-1] + (2, DIFF_DH))


def _split(p):
    B, T, _ = p.shape
    q = p[..., OFF_DQ:OFF_DK].reshape(B, T, DIFF_H, 2, DIFF_DH)
    k = p[..., OFF_DK:OFF_DV].reshape(B, T, DIFF_H, DIFF_DV)
    v = p[..., OFF_DV:OFF_RWKV].reshape(B, T, DIFF_H, DIFF_DV)
    z = p[..., OFF_RWKV:OFF_CQ]
    cq = p[..., OFF_CQ:OFF_GATE].reshape(B, T, CROSS_H, CROSS_DH)
    gate = p[..., OFF_GATE:]
    return q, k, v, z, cq, gate


def _diff_lambda(lq, lam_init):
    lq = lq.astype(jnp.float32)
    return jnp.exp(jnp.sum(lq[0] * lq[1])) - jnp.exp(jnp.sum(lq[2] * lq[3])) + lam_init


def _diff_attend(q, k, v, q_pos, k_pos, lam):
    s = jnp.einsum('bqhmd,bshmd->bhmqs', q, k).astype(jnp.float32) * (DIFF_DH ** -0.5)
    dist = (q_pos[:, None] - k_pos[None, :]).astype(jnp.float32)
    bias = -_alibi_slopes(DIFF_H)[:, None, None] * dist
    s = jnp.where(dist >= 0.0, s + bias[None, :, None], -1e30)
    pr = jax.nn.softmax(s, axis=-1)
    pr = pr[:, :, 0] - lam * pr[:, :, 1]
    return jnp.einsum('bhqs,bshd->bqhd', pr.astype(v.dtype), v)


def _diff_prompt(q, k, v, lam):
    B, T = q.shape[:2]
    nb = T // Q_BLOCK
    pos = jnp.arange(T, dtype=jnp.int32)
    qb = jnp.moveaxis(q.reshape(B, nb, Q_BLOCK, DIFF_H, 2, DIFF_DH), 1, 0)
    pb = pos.reshape(nb, Q_BLOCK)
    kh = _halves(k)
    ob = lax.map(lambda a: _diff_attend(a[0], kh, v, a[1], pos, lam), (qb, pb))
    return jnp.moveaxis(ob, 0, 1).reshape(B, T, DIFF_H, DIFF_DV)


def _diff_post(o, subln, lam_init):
    o = _rmsnorm(o, subln, 1e-5) * (1.0 - lam_init)
    return o.reshape(o.shape[:2] + (DIFF_W,))


def _rwkv_branch(z, S0, w0, w2, a0, a2, k_k, k_a, r_k, ln_g, ln_b):
    B, T, _ = z.shape
    f32 = jnp.float32
    hs = (B, T, RWKV_H, RWKV_N)
    r = z[..., :RWKV_W]
    k = z[..., RWKV_W:2 * RWKV_W]
    v = z[..., 2 * RWKV_W:3 * RWKV_W]
    wd = z[..., 3 * RWKV_W:3 * RWKV_W + DECAY_LORA]
    ad = z[..., 3 * RWKV_W + DECAY_LORA:]
    w = -jax.nn.softplus(-(w0 + jnp.tanh(wd) @ w2).astype(f32)) - 0.5
    decay = jnp.exp(-jnp.exp(w)).reshape(hs)
    a = jax.nn.sigmoid((a0 + ad @ a2).astype(f32))
    kk = (k * k_k).astype(f32).reshape(hs)
    kk = kk / jnp.maximum(jnp.sqrt(jnp.sum(kk * kk, axis=-1, keepdims=True)), 1e-12)
    k = (k.astype(f32) * (1.0 + (a - 1.0) * k_a.astype(f32))).reshape(hs)
    r = r.astype(f32).reshape(hs)
    v = v.astype(f32).reshape(hs)
    a = a.reshape(hs)

    def step(S, inp):
        r_t, w_t, k_t, v_t, kk_t, a_t = inp
        sa = jnp.einsum('bhij,bhj->bhi', S, -kk_t)
        S = (S * w_t[:, :, None, :] + sa[..., None] * (kk_t * a_t)[:, :, None, :]
             + v_t[..., None] * k_t[:, :, None, :])
        return S, jnp.einsum('bhij,bhj->bhi', S, r_t)

    xs = tuple(jnp.swapaxes(t, 0, 1) for t in (r, decay, k, v, kk, a))
    S, y = lax.scan(step, S0.astype(f32), xs)
    y = jnp.swapaxes(y, 0, 1)
    mu = jnp.mean(y, axis=-1, keepdims=True)
    var = jnp.mean(jnp.square(y - mu), axis=-1, keepdims=True)
    y = (y - mu) * lax.rsqrt(var + RWKV_GN_EPS)
    y = y * ln_g.astype(f32).reshape(RWKV_H, RWKV_N) + ln_b.astype(f32).reshape(RWKV_H, RWKV_N)
    y = y + jnp.sum(r * k * r_k.astype(f32), axis=-1, keepdims=True) * v
    return y.reshape(B, T, RWKV_W).astype(z.dtype), S.astype(S0.dtype)


def _mem_kv(mem, norm_mem, w_mem_kv):
    B, M, _ = mem.shape
    kv = _rmsnorm(mem, norm_mem) @ w_mem_kv
    return (kv[..., :CROSS_W].reshape(B, M, CROSS_H, CROSS_DH),
            kv[..., CROSS_W:].reshape(B, M, CROSS_H, CROSS_DH))


def _cross_attend(q, mk, mv):
    B, T = q.shape[:2]
    s = jnp.einsum('bqhd,bshd->bhqs', q, mk).astype(jnp.float32) * (CROSS_DH ** -0.5)
    pr = jax.nn.softmax(s, axis=-1)
    return jnp.einsum('bhqs,bshd->bqhd', pr.astype(mv.dtype), mv).reshape(B, T, CROSS_W)


def _merge(x, d_o, r_o, c_o, gate, w_out):
    mix = jnp.concatenate([d_o, r_o, c_o], axis=-1) * jax.nn.silu(gate)
    return x + mix @ w_out


def setup_inputs(seed: int = 0) -> dict:
    key = jax.random.key(seed)
    keys = jax.random.split(key, 40)
    cnt = [0]

    def nxt():
        cnt[0] += 1
        return keys[cnt[0] - 1]

    def nrm(shape, scale=1.0):
        return scale * jax.random.normal(nxt(), shape, jnp.float32)

    L = DEPTH
    n_pages = PAST_LEN // PAGE_SIZE
    n_used = DEC_BATCH * n_pages
    n_pool = n_used + max(1, n_used // 4)
    page_table = jax.random.permutation(nxt(), n_pool)[:n_used].reshape(DEC_BATCH, n_pages).astype(jnp.int32)
    return {
        'x_prompt': nrm((BATCH, SEQ, D_MODEL)),
        'x_sample': nrm((DEC_BATCH, DEC_SEQ, D_MODEL)),
        'cache_k': nrm((L, n_pool, PAGE_SIZE, DIFF_H, DIFF_DV)),
        'cache_v': nrm((L, n_pool, PAGE_SIZE, DIFF_H, DIFF_DV)),
        'cache_mem_k': nrm((L, DEC_BATCH, N_MEM, CROSS_H, CROSS_DH)),
        'cache_mem_v': nrm((L, DEC_BATCH, N_MEM, CROSS_H, CROSS_DH)),
        'state_rwkv': nrm((L, DEC_BATCH, RWKV_H, RWKV_N, RWKV_N), 0.5),
        'state_shift': nrm((L, DEC_BATCH, D_MODEL)),
        'page_table': page_table,
        'mem_prompt': nrm((BATCH, N_MEM, D_MODEL)),
        'norm_in': 1.0 + nrm((L, D_MODEL), 0.02),
        'w_in': nrm((L, D_MODEL, IN_COLS), D_MODEL ** -0.5),
        'norm_mem': 1.0 + nrm((L, D_MODEL), 0.02),
        'w_mem_kv': nrm((L, D_MODEL, 2 * CROSS_W), D_MODEL ** -0.5),
        'lambda_qk': nrm((L, 4, DIFF_DH), 0.1),
        'diff_subln': 1.0 + nrm((L, DIFF_DV), 0.02),
        'rwkv_mu': jax.random.uniform(nxt(), (L, RWKV_SHIFT_COLS), jnp.float32),
        'rwkv_w0': jax.random.uniform(nxt(), (L, RWKV_W), jnp.float32, -5.0, -0.5),
        'rwkv_w2': nrm((L, DECAY_LORA, RWKV_W), 0.1),
        'rwkv_a0': nrm((L, RWKV_W), 0.1),
        'rwkv_a2': nrm((L, AAA_LORA, RWKV_W), 0.1),
        'rwkv_k_k': 0.85 + nrm((L, RWKV_W), 0.02),
        'rwkv_k_a': 1.0 + nrm((L, RWKV_W), 0.02),
        'rwkv_r_k': nrm((L, RWKV_H, RWKV_N), 0.1),
        'rwkv_ln_g': 1.0 + nrm((L, RWKV_W), 0.02),
        'rwkv_ln_b': nrm((L, RWKV_W), 0.01),
        'w_out': nrm((L, MIX_W, D_MODEL), MIX_W ** -0.5),
        'norm_out': 1.0 + nrm((D_MODEL,), 0.02),
    }


def reference(x_prompt, x_sample, cache_k, cache_v, cache_mem_k, cache_mem_v, state_rwkv, state_shift,
              page_table, mem_prompt, norm_in, w_in, norm_mem, w_mem_kv, lambda_qk, diff_subln,
              rwkv_mu, rwkv_w0, rwkv_w2, rwkv_a0, rwkv_a2, rwkv_k_k, rwkv_k_a, rwkv_r_k,
              rwkv_ln_g, rwkv_ln_b, w_out, norm_out):
    B, T, _ = x_prompt.shape
    DB, DS, _ = x_sample.shape
    n_pages = page_table.shape[1]
    past_len = n_pages * cache_k.shape[2]
    q_pos_s = past_len + jnp.arange(DS, dtype=jnp.int32)
    k_pos_s = jnp.arange(past_len + DS, dtype=jnp.int32)
    xp, xs = x_prompt, x_sample
    nk_p, nv_p, nmk_p, nmv_p, ns_p, nsh_p = [], [], [], [], [], []
    nk_s, nv_s, ns_s, nsh_s = [], [], [], []
    for l in range(DEPTH):
        lam_init = 0.8 - 0.6 * math.exp(-0.3 * l)
        lam = _diff_lambda(lambda_qk[l], lam_init)
        rw = (rwkv_w0[l], rwkv_w2[l], rwkv_a0[l], rwkv_a2[l], rwkv_k_k[l], rwkv_k_a[l],
              rwkv_r_k[l], rwkv_ln_g[l], rwkv_ln_b[l])

        h = _rmsnorm(xp, norm_in[l])
        q, k, v, z, cq, gate = _split(h @ w_in[l])
        d_o = _diff_post(_diff_prompt(q, k, v, lam), diff_subln[l], lam_init)
        z_prev = jnp.concatenate([jnp.zeros_like(z[:, :1]), z[:, :-1]], axis=1)
        s0 = jnp.zeros((B, RWKV_H, RWKV_N, RWKV_N), state_rwkv.dtype)
        r_o, s_p = _rwkv_branch(z + rwkv_mu[l] * (z_prev - z), s0, *rw)
        mk, mv = _mem_kv(mem_prompt, norm_mem[l], w_mem_kv[l])
        c_o = _cross_attend(cq, mk, mv)
        xp = _merge(xp, d_o, r_o, c_o, gate, w_out[l])
        nk_p.append(k)
        nv_p.append(v)
        nmk_p.append(mk)
        nmv_p.append(mv)
        ns_p.append(s_p)
        nsh_p.append(h[:, -1])

        h = _rmsnorm(xs, norm_in[l])
        q, k, v, z, cq, gate = _split(h @ w_in[l])
        k_all = jnp.concatenate([cache_k[l][page_table].reshape(DB, past_len, DIFF_H, DIFF_DV), k], axis=1)
        v_all = jnp.concatenate([cache_v[l][page_table].reshape(DB, past_len, DIFF_H, DIFF_DV), v], axis=1)
        d_o = _diff_post(_diff_attend(q, _halves(k_all), v_all, q_pos_s, k_pos_s, lam), diff_subln[l], lam_init)
        z_first = state_shift[l] @ w_in[l][:, OFF_RWKV:OFF_CQ]
        z_prev = jnp.concatenate([z_first[:, None], z[:, :-1]], axis=1)
        r_o, s_s = _rwkv_branch(z + rwkv_mu[l] * (z_prev - z), state_rwkv[l], *rw)
        c_o = _cross_attend(cq, cache_mem_k[l], cache_mem_v[l])
        xs = _merge(xs, d_o, r_o, c_o, gate, w_out[l])
        nk_s.append(k)
        nv_s.append(v)
        ns_s.append(s_s)
        nsh_s.append(h[:, -1])

    y_prompt = _rmsnorm(xp, norm_out)
    y_sample = _rmsnorm(xs, norm_out)
    return (y_prompt, y_sample,
            jnp.stack(nk_p), jnp.stack(nv_p), jnp.stack(nmk_p), jnp.stack(nmv_p),
            jnp.stack(ns_p), jnp.stack(nsh_p),
            jnp.stack(nk_s), jnp.stack(nv_s), jnp.stack(ns_s), jnp.stack(nsh_s))
```

```python
import functools
import math

import jax
import jax.numpy as jnp
import numpy as np
from jax import lax
from jax.experimental import pallas as pl
from jax.experimental.pallas import tpu as pltpu

F32 = jnp.float32
BF16 = jnp.bfloat16

LANES = 128
MIB = 1024 * 1024

DIFF_H = 4
DIFF_DV = 128
DIFF_DH = 64
RWKV_N = 64
RWKV_H = 16
RWKV_W = RWKV_H * RWKV_N
CROSS_H = 4
CROSS_DH = 128
LORA = 64
RWKV_GN_EPS = 64e-5
CHUNK = 64
NEG = -1e30

OFF_GATE = 0
OFF_R = 2048
OFF_K = 3072
OFF_V = 4096
OFF_DQ = 5120
OFF_DK = 5632
OFF_DV = 6144
OFF_CQ = 6656
OFF_LORA = 7168
IN_COLS = 7296


def _params(sem, vmem_mib):
    return pltpu.CompilerParams(dimension_semantics=sem, vmem_limit_bytes=vmem_mib * MIB)


def _split2(x):
    hi = x.astype(BF16)
    lo = (x - hi.astype(F32)).astype(BF16)
    return hi, lo


def _dot(a, b, dims=(((1,), (0,)), ((), ()))):
    return lax.dot_general(a, b, dims, preferred_element_type=F32)


NN = (((1,), (0,)), ((), ()))
NT = (((1,), (1,)), ((), ()))
TN = (((0,), (0,)), ((), ()))


def _dot3(a, b, dims=NN):
    ah, al = _split2(a)
    bh, bl = _split2(b)
    return _dot(ah, bh, dims) + (_dot(ah, bl, dims) + _dot(al, bh, dims))


def _dot_sel(x, sel):
    h, l = _split2(x)
    return _dot(h, sel) + _dot(l, sel)


def _norm_matmul_kernel(x_ref, g_ref, w_ref, o_ref, h_ref, *, normalize, eps):
    @pl.when(pl.program_id(1) == 0)
    def _():
        x = x_ref[...]
        if normalize:
            x = x * lax.rsqrt(jnp.mean(x * x, axis=-1, keepdims=True) + eps) * g_ref[...]
        h_ref[...] = x.astype(BF16)

    o_ref[...] = jnp.dot(h_ref[...], w_ref[...], preferred_element_type=F32)


def _norm_matmul(x, g, w, *, normalize, tm, tn, vmem_mib):
    m, d = x.shape
    n = w.shape[1]
    assert m % tm == 0 and n % tn == 0
    return pl.pallas_call(
        functools.partial(_norm_matmul_kernel, normalize=normalize, eps=1e-6),
        grid=(m // tm, n // tn),
        in_specs=[pl.BlockSpec((tm, d), lambda i, j: (i, 0)),
                  pl.BlockSpec((1, d), lambda i, j: (0, 0)),
                  pl.BlockSpec((d, tn), lambda i, j: (0, j))],
        out_specs=pl.BlockSpec((tm, tn), lambda i, j: (i, j)),
        out_shape=jax.ShapeDtypeStruct((m, n), F32),
        scratch_shapes=[pltpu.VMEM((tm, d), BF16)],
        compiler_params=_params(("arbitrary", "arbitrary"), vmem_mib),
        name="norm_matmul",
    )(x, g, w)


def _rmsnorm_kernel(x_ref, g_ref, o_ref, *, eps):
    x = x_ref[...]
    o_ref[...] = x * lax.rsqrt(jnp.mean(x * x, axis=-1, keepdims=True) + eps) * g_ref[...]


def _pad_rows(x, mult):
    pad = -x.shape[0] % mult
    return jnp.pad(x, ((0, pad), (0, 0))) if pad else x


def _rmsnorm_rows(x, g):
    n = x.shape[0]
    xp = _pad_rows(x, 8)
    return pl.pallas_call(
        functools.partial(_rmsnorm_kernel, eps=1e-6),
        out_shape=jax.ShapeDtypeStruct(xp.shape, F32),
        name="rmsnorm_rows",
    )(xp, g)[:n]


def _diff_lambda(lq, lam_init):
    t1 = jnp.sum(lq[0:1] * lq[1:2], axis=-1, keepdims=True)
    t2 = jnp.sum(lq[2:3] * lq[3:4], axis=-1, keepdims=True)
    return jnp.exp(t1) - jnp.exp(t2) + lam_init


def _diff_finish(acc, l, lq, subln, rows, lam_init):
    o12 = acc / l
    o = o12[:rows] - _diff_lambda(lq, lam_init) * o12[rows:]
    o = o * lax.rsqrt(jnp.mean(o * o, axis=-1, keepdims=True) + 1e-5) * subln
    return o * (1.0 - lam_init)


def _stack_maps(q):
    q = q * (DIFF_DH ** -0.5)
    lane = lax.broadcasted_iota(jnp.int32, q.shape, 1)
    q1 = jnp.where(lane < DIFF_DH, q, 0.0)
    q2 = jnp.where(lane >= DIFF_DH, q, 0.0)
    return jnp.concatenate([q1, q2], axis=0).astype(BF16)


def _diff_attn_kernel(qi_ref, kj_ref, q_ref, k_ref, v_ref, slope_ref, lq_ref, subln_ref, o_ref,
                      m_ref, l_ref, acc_ref, *, tq, lam_init):
    t = pl.program_id(2)
    i = qi_ref[t]
    j = kj_ref[t]

    @pl.when(j == 0)
    def _():
        m_ref[...] = jnp.full(m_ref.shape, NEG, F32)
        l_ref[...] = jnp.zeros(l_ref.shape, F32)
        acc_ref[...] = jnp.zeros(acc_ref.shape, F32)

    qq = _stack_maps(q_ref[0])
    kb = k_ref[0].astype(BF16)
    vb = v_ref[0].astype(BF16)
    s = _dot(qq, kb, NT)
    row = lax.broadcasted_iota(jnp.int32, s.shape, 0)
    row = jnp.where(row >= tq, row - tq, row)
    col = lax.broadcasted_iota(jnp.int32, s.shape, 1)
    dist = (i - j) * tq + row - col
    slope = slope_ref[0][:, :1]
    s = jnp.where(dist >= 0, s - slope * dist.astype(F32), NEG)

    m_prev = m_ref[...]
    m_new = jnp.maximum(m_prev, jnp.max(s, axis=-1, keepdims=True))
    alpha = jnp.exp(m_prev - m_new)
    p = jnp.exp(s - m_new)
    l_ref[...] = alpha * l_ref[...] + jnp.sum(p, axis=-1, keepdims=True)
    acc_ref[...] = alpha * acc_ref[...] + _dot(p.astype(BF16), vb)
    m_ref[...] = m_new

    @pl.when(j == i)
    def _():
        o_ref[0] = _diff_finish(acc_ref[...], l_ref[...], lq_ref[...], subln_ref[...], tq, lam_init)


def _diff_attn(p3, slopes, lambda_qk, subln, *, tq, lam_init):
    b, t, _ = p3.shape
    nq = t // tq
    pairs = [(i, j) for i in range(nq) for j in range(i + 1)]
    qi = jnp.asarray(np.array([p[0] for p in pairs], np.int32))
    kj = jnp.asarray(np.array([p[1] for p in pairs], np.int32))
    qb, kb, vb = OFF_DQ // DIFF_DV, OFF_DK // DIFF_DV, OFF_DV // DIFF_DV
    grid_spec = pltpu.PrefetchScalarGridSpec(
        num_scalar_prefetch=2,
        grid=(b, DIFF_H, len(pairs)),
        in_specs=[pl.BlockSpec((1, tq, DIFF_DV), lambda bb, h, s, qi, kj: (bb, qi[s], qb + h)),
                  pl.BlockSpec((1, tq, DIFF_DV), lambda bb, h, s, qi, kj: (bb, kj[s], kb + h)),
                  pl.BlockSpec((1, tq, DIFF_DV), lambda bb, h, s, qi, kj: (bb, kj[s], vb + h)),
                  pl.BlockSpec((1, 1, LANES), lambda bb, h, s, qi, kj: (h, 0, 0)),
                  pl.BlockSpec((4, DIFF_DH), lambda bb, h, s, qi, kj: (0, 0)),
                  pl.BlockSpec((1, DIFF_DV), lambda bb, h, s, qi, kj: (0, 0))],
        out_specs=pl.BlockSpec((1, tq, DIFF_DV), lambda bb, h, s, qi, kj: (bb, qi[s], h)),
        scratch_shapes=[pltpu.VMEM((2 * tq, 1), F32), pltpu.VMEM((2 * tq, 1), F32),
                        pltpu.VMEM((2 * tq, DIFF_DV), F32)],
    )
    return pl.pallas_call(
        functools.partial(_diff_attn_kernel, tq=tq, lam_init=lam_init),
        grid_spec=grid_spec,
        out_shape=jax.ShapeDtypeStruct((b, t, DIFF_H * DIFF_DV), F32),
        compiler_params=_params(("arbitrary", "arbitrary", "arbitrary"), 40),
        name="diff_attn",
    )(qi, kj, p3, p3, p3, slopes, lambda_qk, subln)


def _paged_diff_attn_kernel(pt_ref, q_ref, *refs, n_pages, page, n_new, past_len, lam_init):
    k_refs = refs[:n_pages]
    v_refs = refs[n_pages:2 * n_pages]
    kn_ref, vn_ref, slope_ref, lq_ref, subln_ref, o_ref, m_ref, l_ref, acc_ref = refs[2 * n_pages:]
    s_id = pl.program_id(1)
    rows = 2 * DIFF_H * n_new

    @pl.when(s_id == 0)
    def _():
        m_ref[...] = jnp.full(m_ref.shape, NEG, F32)
        l_ref[...] = jnp.zeros(l_ref.shape, F32)
        acc_ref[...] = jnp.zeros(acc_ref.shape, F32)

    qb = (q_ref[0] * (DIFF_DH ** -0.5)).astype(BF16)
    slope = slope_ref[...][:, :1]
    log_new, log_h = n_new.bit_length() - 1, DIFF_H.bit_length() - 1

    def coords(ncol):
        row = lax.broadcasted_iota(jnp.int32, (rows, ncol), 0)
        col = lax.broadcasted_iota(jnp.int32, (rows, ncol), 1)
        row_h = lax.shift_right_logical(row, log_new) & (DIFF_H - 1)
        return row & (n_new - 1), (col & (DIFF_H - 1)) == row_h, lax.shift_right_logical(col, log_h)

    row_t, same_head, tok = coords(page * DIFF_H)

    def update(s, vb, carry):
        m_prev, l_prev, acc_prev = carry
        m_new = jnp.maximum(m_prev, jnp.max(s, axis=-1, keepdims=True))
        alpha = jnp.exp(m_prev - m_new)
        p = jnp.exp(s - m_new)
        l_new = alpha * l_prev + jnp.sum(p, axis=-1, keepdims=True)
        acc_new = alpha * acc_prev + _dot(p.astype(BF16), vb)
        return m_new, l_new, acc_new

    carry = (m_ref[...], l_ref[...], acc_ref[...])
    for i in range(n_pages):
        kb = k_refs[i][0].astype(BF16)
        vb = v_refs[i][0].astype(BF16)
        s = _dot(qb, kb, NT)
        k_pos = (s_id * n_pages + i) * page + tok
        dist = (past_len + row_t - k_pos).astype(F32)
        s = jnp.where(same_head, s - slope * dist, NEG)
        carry = update(s, vb, carry)
    m_ref[...], l_ref[...], acc_ref[...] = carry

    @pl.when(s_id == pl.num_programs(1) - 1)
    def _():
        kb = kn_ref[0].astype(BF16)
        vb = vn_ref[0].astype(BF16)
        s = _dot(qb, kb, NT)
        row_tn, same_head_n, tok_n = coords(n_new * DIFF_H)
        dist = row_tn - tok_n
        s = jnp.where(same_head_n & (dist >= 0), s - slope * dist.astype(F32), NEG)
        _, l, acc = update(s, vb, (m_ref[...], l_ref[...], acc_ref[...]))
        o_ref[0] = _diff_finish(acc, l, lq_ref[...], subln_ref[...], rows // 2, lam_init)


def _paged_diff_attn(qm, cache_k, cache_v, k_new, v_new, page_table, slope_rows, lambda_qk, subln, *,
                     n_pages, lam_init):
    db, rows, _ = qm.shape
    n_pool, page = cache_k.shape[0], cache_k.shape[1] // DIFF_H
    n_tab = page_table.shape[1]
    n_new = rows // (2 * DIFF_H)
    assert n_tab % n_pages == 0 and n_new & (n_new - 1) == 0 and DIFF_H & (DIFF_H - 1) == 0

    def page_spec(i):
        return pl.BlockSpec((1, page * DIFF_H, DIFF_DV), lambda b, s, pt: (pt[b, s * n_pages + i], 0, 0))

    grid_spec = pltpu.PrefetchScalarGridSpec(
        num_scalar_prefetch=1,
        grid=(db, n_tab // n_pages),
        in_specs=([pl.BlockSpec((1, rows, DIFF_DV), lambda b, s, pt: (b, 0, 0))]
                  + [page_spec(i) for i in range(n_pages)] * 2
                  + [pl.BlockSpec((1, n_new * DIFF_H, DIFF_DV), lambda b, s, pt: (b, 0, 0)),
                     pl.BlockSpec((1, n_new * DIFF_H, DIFF_DV), lambda b, s, pt: (b, 0, 0)),
                     pl.BlockSpec((rows, LANES), lambda b, s, pt: (0, 0)),
                     pl.BlockSpec((4, DIFF_DH), lambda b, s, pt: (0, 0)),
                     pl.BlockSpec((1, DIFF_DV), lambda b, s, pt: (0, 0))]),
        out_specs=pl.BlockSpec((1, rows // 2, DIFF_DV), lambda b, s, pt: (b, 0, 0)),
        scratch_shapes=[pltpu.VMEM((rows, 1), F32), pltpu.VMEM((rows, 1), F32), pltpu.VMEM((rows, DIFF_DV), F32)],
    )
    return pl.pallas_call(
        functools.partial(_paged_diff_attn_kernel, n_pages=n_pages, page=page, n_new=n_new,
                          past_len=n_tab * page, lam_init=lam_init),
        grid_spec=grid_spec,
        out_shape=jax.ShapeDtypeStruct((db, rows // 2, DIFF_DV), F32),
        compiler_params=_params(("arbitrary", "arbitrary"), 32),
        name="paged_diff_attn",
    )(page_table, qm, *([cache_k] * n_pages), *([cache_v] * n_pages), k_new, v_new, slope_rows, lambda_qk, subln)


def _cross_attn_kernel(q_ref, k_ref, v_ref, o_ref, *, scale):
    q = q_ref[0].astype(BF16)
    k = k_ref[0].astype(BF16)
    v = v_ref[0].astype(BF16)
    s = _dot(q, k, NT) * scale
    p = jnp.exp(s - jnp.max(s, axis=-1, keepdims=True))
    l = jnp.sum(p, axis=-1, keepdims=True)
    o_ref[0] = _dot(p.astype(BF16), v) / l


def _cross_attn(q_arr, q_off, k_arr, k_off, v_arr, v_off, *, tq):
    b, t, _ = q_arr.shape
    n_mem = k_arr.shape[1]
    qb, kb, vb = q_off // CROSS_DH, k_off // CROSS_DH, v_off // CROSS_DH
    return pl.pallas_call(
        functools.partial(_cross_attn_kernel, scale=CROSS_DH ** -0.5),
        grid=(b, CROSS_H, t // tq),
        in_specs=[pl.BlockSpec((1, tq, CROSS_DH), lambda bb, h, i: (bb, i, qb + h)),
                  pl.BlockSpec((1, n_mem, CROSS_DH), lambda bb, h, i: (bb, 0, kb + h)),
                  pl.BlockSpec((1, n_mem, CROSS_DH), lambda bb, h, i: (bb, 0, vb + h))],
        out_specs=pl.BlockSpec((1, tq, CROSS_DH), lambda bb, h, i: (bb, i, h)),
        out_shape=jax.ShapeDtypeStruct((b, t, CROSS_H * CROSS_DH), F32),
        compiler_params=_params(("arbitrary", "arbitrary", "arbitrary"), 32),
        name="cross_attn",
    )(q_arr, k_arr, v_arr)


def _rwkv_prep_kernel(r_ref, k_ref, v_ref, lo_ref, fr_ref, fk_ref, fv_ref, fl_ref,
                      mur_ref, muk_ref, muv_ref, mul_ref, w0_ref, w2_ref, a0_ref, a2_ref,
                      kk_ref, ka_ref, rk_ref, e1_ref, e2_ref,
                      ro_ref, lwo_ref, ko_ref, vo_ref, ao_ref, bo_ref, bonus_ref,
                      cr_ref, ck_ref, cv_ref, cl_ref):
    @pl.when(pl.program_id(1) == 0)
    def _():
        cr_ref[...] = fr_ref[0]
        ck_ref[...] = fk_ref[0]
        cv_ref[...] = fv_ref[0]
        cl_ref[...] = fl_ref[0]

    def shift_mix(x_ref, carry_ref, mu_ref):
        x = x_ref[0]
        rows = x.shape[0]
        prev = pltpu.roll(x, 1, axis=0)
        rid = lax.broadcasted_iota(jnp.int32, x.shape, 0)
        prev = jnp.where(rid == 0, carry_ref[...], prev)
        carry_ref[...] = x[rows - 1:rows, :]
        return x + mu_ref[...] * (prev - x)

    r = shift_mix(r_ref, cr_ref, mur_ref)
    k = shift_mix(k_ref, ck_ref, muk_ref)
    v = shift_mix(v_ref, cv_ref, muv_ref)
    lo = shift_mix(lo_ref, cl_ref, mul_ref)

    xw = -(w0_ref[...] + _dot3(jnp.tanh(lo), w2_ref[...]))
    softplus = jnp.maximum(xw, 0.0) + jnp.log(1.0 + jnp.exp(-jnp.abs(xw)))
    w = -softplus - 0.5
    lwo_ref[0] = -jnp.exp(w)
    a = jax.nn.sigmoid(a0_ref[...] + _dot3(lo, a2_ref[...]))

    def head_sum(x):
        return _dot_sel(_dot_sel(x, e1_ref[...]), e2_ref[...])

    kkr = k * kk_ref[...]
    kk = kkr / jnp.maximum(jnp.sqrt(head_sum(kkr * kkr)), 1e-12)
    k2 = k * (1.0 + (a - 1.0) * ka_ref[...])
    ro_ref[0] = r
    ko_ref[0] = k2
    vo_ref[0] = v
    ao_ref[0] = -kk
    bo_ref[0] = kk * a
    bonus_ref[0] = head_sum(r * k2 * rk_ref[...]) * v


def _rwkv_prep(p3, firsts, mus, w0, w2p, a0, a2p, k_k, k_a, r_k, e1, e2, *, tm):
    b, t, _ = p3.shape
    nt = t // tm
    wide = lambda blk: pl.BlockSpec((1, tm, RWKV_W), lambda bb, i: (bb, i, blk))
    vec = lambda n: pl.BlockSpec((1, n), lambda bb, i: (0, 0))
    first = lambda n: pl.BlockSpec((1, 1, n), lambda bb, i: (bb, 0, 0))
    out_wide = pl.BlockSpec((1, tm, RWKV_W), lambda bb, i: (bb, i, 0))
    in_specs = [wide(OFF_R // RWKV_W), wide(OFF_K // RWKV_W), wide(OFF_V // RWKV_W),
                pl.BlockSpec((1, tm, LANES), lambda bb, i: (bb, i, OFF_LORA // LANES)),
                first(RWKV_W), first(RWKV_W), first(RWKV_W), first(LANES),
                vec(RWKV_W), vec(RWKV_W), vec(RWKV_W), vec(LANES),
                vec(RWKV_W), pl.BlockSpec((LANES, RWKV_W), lambda bb, i: (0, 0)),
                vec(RWKV_W), pl.BlockSpec((LANES, RWKV_W), lambda bb, i: (0, 0)),
                vec(RWKV_W), vec(RWKV_W), vec(RWKV_W),
                pl.BlockSpec((RWKV_W, LANES), lambda bb, i: (0, 0)),
                pl.BlockSpec((LANES, RWKV_W), lambda bb, i: (0, 0))]
    shape = jax.ShapeDtypeStruct((b, t, RWKV_W), F32)
    return pl.pallas_call(
        _rwkv_prep_kernel,
        grid=(b, nt),
        in_specs=in_specs,
        out_specs=[out_wide] * 7,
        out_shape=[shape] * 7,
        scratch_shapes=[pltpu.VMEM((1, RWKV_W), F32), pltpu.VMEM((1, RWKV_W), F32),
                        pltpu.VMEM((1, RWKV_W), F32), pltpu.VMEM((1, LANES), F32)],
        compiler_params=_params(("arbitrary", "arbitrary"), 48),
        name="rwkv_prep",
    )(p3, p3, p3, p3, *firsts, *mus, w0, w2p, a0, a2p, k_k, k_a, r_k, e1, e2)


def _rwkv_scan_kernel(r_ref, lw_ref, k_ref, v_ref, a_ref, b_ref, bonus_ref, s0_ref, g_ref, bb_ref,
                      y_ref, sout_ref, s_ref, *, chunk, n_pairs):
    c = pl.program_id(1)
    c2 = 2 * chunk

    @pl.when(c == 0)
    def _():
        s_ref[...] = s0_ref[0]

    rid = lax.broadcasted_iota(jnp.int32, (c2, c2), 0)
    cid = lax.broadcasted_iota(jnp.int32, (c2, c2), 1)
    strict = rid > cid
    incl = rid >= cid
    tr = lax.broadcasted_iota(jnp.int32, (chunk, chunk), 0)
    tc = lax.broadcasted_iota(jnp.int32, (chunk, chunk), 1)
    tri = (tr >= tc).astype(BF16)
    srow = lax.broadcasted_iota(jnp.int32, (c2, LANES), 0)
    slane = lax.broadcasted_iota(jnp.int32, (c2, LANES), 1)
    own = (srow < chunk) == (slane < RWKV_N)

    def stack(x):
        return jnp.where(own, jnp.concatenate([x, x], axis=0), 0.0)

    for p in range(n_pairs):
        sl = pl.ds(p * LANES, LANES)
        lw = lw_ref[0, :, sl]
        h1 = lw.astype(BF16)
        r1 = lw - h1.astype(F32)
        h2 = r1.astype(BF16)
        h3 = (r1 - h2.astype(F32)).astype(BF16)
        cum = _dot(tri, h1) + (_dot(tri, h2) + _dot(tri, h3))
        g = jnp.exp(cum)
        gi = jnp.exp(-cum)
        gp = jnp.exp(cum - lw)
        am = stack(a_ref[0, :, sl] * gp)
        rm = stack(r_ref[0, :, sl] * g)
        bm = stack(b_ref[0, :, sl] * gi)
        km = stack(k_ref[0, :, sl] * gi)
        vm = stack(v_ref[0, :, sl])
        s_prev = s_ref[p]

        nm = jnp.where(strict, _dot3(am, bm, NT), 0.0)
        aak = jnp.where(strict, _dot3(am, km, NT), 0.0)
        mrb = jnp.where(incl, _dot3(rm, bm, NT), 0.0)
        mrk = jnp.where(incl, _dot3(rm, km, NT), 0.0)
        u = _dot3(am, s_prev, NT) + _dot3(aak, vm)
        n = 1
        while True:
            u = u + _dot3(nm, u)
            n *= 2
            if n >= chunk:
                break
            nm = _dot3(nm, nm)
        o = _dot3(rm, s_prev, NT) + _dot3(mrb, u) + _dot3(mrk, vm)
        g_end = g[chunk - 1:chunk, :]
        s_ref[p] = (s_prev + _dot3(u, bm, TN) + _dot3(vm, km, TN)) * g_end

        mu = jnp.sum(o, axis=-1, keepdims=True) * (1.0 / RWKV_N)
        d = jnp.where(own, o - mu, 0.0)
        var = jnp.sum(d * d, axis=-1, keepdims=True) * (1.0 / RWKV_N)
        yn = d * lax.rsqrt(var + RWKV_GN_EPS)
        y = yn[:chunk] + yn[chunk:]
        y_ref[0, :, sl] = y * g_ref[:, sl] + bb_ref[:, sl] + bonus_ref[0, :, sl]

    @pl.when(c == pl.num_programs(1) - 1)
    def _():
        sout_ref[0] = s_ref[...]


def _rwkv_scan(r, lw, k, v, a, b, bonus, s0_pairs, ln_g, ln_b):
    bsz, t, _ = r.shape
    n_pairs = RWKV_H // 2
    blk = pl.BlockSpec((1, CHUNK, RWKV_W), lambda bb, c: (bb, c, 0))
    st = pl.BlockSpec((1, n_pairs, LANES, LANES), lambda bb, c: (bb, 0, 0, 0))
    vec = pl.BlockSpec((1, RWKV_W), lambda bb, c: (0, 0))
    return pl.pallas_call(
        functools.partial(_rwkv_scan_kernel, chunk=CHUNK, n_pairs=n_pairs),
        grid=(bsz, t // CHUNK),
        in_specs=[blk] * 7 + [st, vec, vec],
        out_specs=[blk, st],
        out_shape=[jax.ShapeDtypeStruct((bsz, t, RWKV_W), F32),
                   jax.ShapeDtypeStruct((bsz, n_pairs, LANES, LANES), F32)],
        scratch_shapes=[pltpu.VMEM((n_pairs, LANES, LANES), F32)],
        compiler_params=_params(("arbitrary", "arbitrary"), 32),
        name="rwkv_scan",
    )(r, lw, k, v, a, b, bonus, s0_pairs, ln_g, ln_b)


def _pair_states(s):
    bsz = s.shape[0]
    s = s.reshape(bsz, RWKV_H // 2, 2, RWKV_N, RWKV_N)
    z = jnp.zeros_like(s[:, :, 0])
    top = jnp.concatenate([s[:, :, 0], z], axis=-1)
    bot = jnp.concatenate([z, s[:, :, 1]], axis=-1)
    return jnp.concatenate([top, bot], axis=-2)


def _unpair_states(sp):
    bsz = sp.shape[0]
    even = sp[:, :, :RWKV_N, :RWKV_N]
    odd = sp[:, :, RWKV_N:, RWKV_N:]
    return jnp.stack([even, odd], axis=2).reshape(bsz, RWKV_H, RWKV_N, RWKV_N)


def _merge_out_kernel(x_ref, d_ref, r_ref, c_ref, gate_ref, w_ref, g_ref, o_ref, *, eps):
    gate = gate_ref[...]
    sg = gate * jax.nn.sigmoid(gate)
    nd, nr = d_ref.shape[1], r_ref.shape[1]
    acc = _dot((d_ref[...] * sg[:, :nd]).astype(BF16), w_ref[0:nd, :])
    acc += _dot((r_ref[...] * sg[:, nd:nd + nr]).astype(BF16), w_ref[nd:nd + nr, :])
    acc += _dot((c_ref[...] * sg[:, nd + nr:]).astype(BF16), w_ref[nd + nr:, :])
    x = x_ref[...] + acc
    o_ref[...] = x * lax.rsqrt(jnp.mean(x * x, axis=-1, keepdims=True) + eps) * g_ref[...]


def _merge_out(x, d_o, r_o, c_o, p2, w_out, norm_out, *, tm):
    m, d = x.shape
    row = lambda n: pl.BlockSpec((tm, n), lambda i: (i, 0))
    return pl.pallas_call(
        functools.partial(_merge_out_kernel, eps=1e-6),
        grid=(m // tm,),
        in_specs=[row(d), row(d_o.shape[1]), row(r_o.shape[1]), row(c_o.shape[1]),
                  pl.BlockSpec((tm, d), lambda i: (i, OFF_GATE // d)),
                  pl.BlockSpec((d, d), lambda i: (0, 0)),
                  pl.BlockSpec((1, d), lambda i: (0, 0))],
        out_specs=row(d),
        out_shape=jax.ShapeDtypeStruct((m, d), F32),
        compiler_params=_params(("arbitrary",), 48),
        name="merge_out",
    )(x, d_o, r_o, c_o, p2, w_out, norm_out)


def _reorder_in_cols(w):
    dq, z, cq, gate = 0, 1536, 1536 + 3200, 1536 + 3200 + 512
    return jnp.concatenate([w[..., gate:], w[..., z:z + 3 * RWKV_W], w[..., dq:z], w[..., cq:gate],
                            w[..., z + 3 * RWKV_W:cq]], axis=-1)


def kernel(x_prompt, x_sample, cache_k, cache_v, cache_mem_k, cache_mem_v, state_rwkv, state_shift, page_table,
           mem_prompt, norm_in, w_in, norm_mem, w_mem_kv, lambda_qk, diff_subln, rwkv_mu, rwkv_w0, rwkv_w2, rwkv_a0,
           rwkv_a2, rwkv_k_k, rwkv_k_a, rwkv_r_k, rwkv_ln_g, rwkv_ln_b, w_out, norm_out):
    bsz, seq, d = x_prompt.shape
    db, ds, _ = x_sample.shape
    depth = w_in.shape[0]
    assert depth == 1
    l = 0
    lam_init = 0.8 - 0.6 * math.exp(-0.3 * l)
    n_mem = mem_prompt.shape[1]

    w_in_b = _reorder_in_cols(w_in[l]).astype(BF16)
    w_shift_b = jnp.concatenate([w_in_b[:, OFF_R:OFF_DQ], w_in_b[:, OFF_LORA:]], axis=1)
    w_mem_b = w_mem_kv[l].astype(BF16)
    w_out_b = w_out[l].astype(BF16)
    g_in = norm_in[l][None]
    g_mem = norm_mem[l][None]
    g_out = norm_out[None]
    mu = rwkv_mu[l]
    mus = (mu[None, :RWKV_W], mu[None, RWKV_W:2 * RWKV_W], mu[None, 2 * RWKV_W:3 * RWKV_W], mu[None, 3 * RWKV_W:])
    zl = jnp.zeros((LORA, RWKV_W), F32)
    w2p = jnp.concatenate([rwkv_w2[l], zl], axis=0)
    a2p = jnp.concatenate([zl, rwkv_a2[l]], axis=0)
    vec = lambda x: x.reshape(1, RWKV_W)
    w0, a0, k_k, k_a, r_k = vec(rwkv_w0[l]), vec(rwkv_a0[l]), vec(rwkv_k_k[l]), vec(rwkv_k_a[l]), vec(rwkv_r_k[l])
    ln_g, ln_b = vec(rwkv_ln_g[l]), vec(rwkv_ln_b[l])
    head_of = np.arange(RWKV_W) // RWKV_N
    e1 = jnp.asarray((head_of[:, None] == np.arange(LANES)[None, :]).astype(np.float32), BF16)
    e2 = jnp.asarray((np.arange(LANES)[:, None] == head_of[None, :]).astype(np.float32), BF16)
    slopes = 2.0 ** (-8.0 * np.arange(1, DIFF_H + 1, dtype=np.float64) / DIFF_H)
    slopes_h = jnp.asarray(np.broadcast_to(slopes[:, None, None], (DIFF_H, 1, LANES)).astype(np.float32))
    subln = diff_subln[l][None]
    lq = lambda_qk[l]

    xp2 = x_prompt.reshape(bsz * seq, d)
    p2 = _norm_matmul(xp2, g_in, w_in_b, normalize=True, tm=512, tn=IN_COLS // 3, vmem_mib=56)
    p3 = p2.reshape(bsz, seq, IN_COLS)
    hs_p = _rmsnorm_rows(x_prompt[:, -1], g_in)
    d_o = _diff_attn(p3, slopes_h, lq, subln, tq=512, lam_init=lam_init)
    zeros_first = (jnp.zeros((bsz, 1, RWKV_W), F32),) * 3 + (jnp.zeros((bsz, 1, LANES), F32),)
    r, lw, k2, v, a, b, bonus = _rwkv_prep(p3, zeros_first, mus, w0, w2p, a0, a2p, k_k, k_a, r_k, e1, e2, tm=256)
    s0_p = jnp.zeros((bsz, RWKV_H // 2, LANES, LANES), F32)
    r_o, sp_p = _rwkv_scan(r, lw, k2, v, a, b, bonus, s0_p, ln_g, ln_b)
    mem_kv = _norm_matmul(mem_prompt.reshape(bsz * n_mem, d), g_mem, w_mem_b, normalize=True, tm=256, tn=512,
                          vmem_mib=32).reshape(bsz, n_mem, 2 * CROSS_H * CROSS_DH)
    c_o = _cross_attn(p3, OFF_CQ, mem_kv, 0, mem_kv, CROSS_H * CROSS_DH, tq=1024)
    y_p = _merge_out(xp2, d_o.reshape(bsz * seq, -1), r_o.reshape(bsz * seq, -1), c_o.reshape(bsz * seq, -1), p2,
                     w_out_b, g_out, tm=256).reshape(bsz, seq, d)
    nk_p = p3[:, :, OFF_DK:OFF_DV].reshape(1, bsz, seq, DIFF_H, DIFF_DV)
    nv_p = p3[:, :, OFF_DV:OFF_CQ].reshape(1, bsz, seq, DIFF_H, DIFF_DV)
    nmk_p = mem_kv[:, :, :CROSS_H * CROSS_DH].reshape(1, bsz, n_mem, CROSS_H, CROSS_DH)
    nmv_p = mem_kv[:, :, CROSS_H * CROSS_DH:].reshape(1, bsz, n_mem, CROSS_H, CROSS_DH)
    ns_p = _unpair_states(sp_p)[None]

    xs2 = x_sample.reshape(db * ds, d)
    ps2 = _norm_matmul(xs2, g_in, w_in_b, normalize=True, tm=db * ds, tn=IN_COLS // 3, vmem_mib=56)
    ps3 = ps2.reshape(db, ds, IN_COLS)
    hs_s = _rmsnorm_rows(x_sample[:, -1], g_in)
    shift_rows = _pad_rows(state_shift[l], 16)
    z_first = _norm_matmul(shift_rows, g_in, w_shift_b, normalize=False, tm=shift_rows.shape[0],
                           tn=w_shift_b.shape[1], vmem_mib=56)[:db]
    firsts = (z_first[:, None, :RWKV_W], z_first[:, None, RWKV_W:2 * RWKV_W],
              z_first[:, None, 2 * RWKV_W:3 * RWKV_W], z_first[:, None, 3 * RWKV_W:])

    q5 = ps3[:, :, OFF_DQ:OFF_DK].reshape(db, ds, DIFF_H, 2, DIFF_DH)
    zq = jnp.zeros_like(q5[..., 0, :])
    qm = jnp.stack([jnp.concatenate([q5[..., 0, :], zq], -1), jnp.concatenate([zq, q5[..., 1, :]], -1)], axis=1)
    qm = qm.transpose(0, 1, 3, 2, 4).reshape(db, 2 * DIFF_H * ds, DIFF_DV)
    k_new = ps3[:, :, OFF_DK:OFF_DV].reshape(db, ds * DIFF_H, DIFF_DV)
    v_new = ps3[:, :, OFF_DV:OFF_CQ].reshape(db, ds * DIFF_H, DIFF_DV)
    row_head = (np.arange(2 * DIFF_H * ds) // ds) % DIFF_H
    slope_rows = jnp.asarray(np.broadcast_to(slopes[row_head][:, None], (2 * DIFF_H * ds, LANES)).astype(np.float32))
    n_pool, page = cache_k.shape[1], cache_k.shape[2]
    ck = cache_k[l].reshape(n_pool, page * DIFF_H, DIFF_DV)
    cv = cache_v[l].reshape(n_pool, page * DIFF_H, DIFF_DV)
    ds_o = _paged_diff_attn(qm, ck, cv, k_new, v_new, page_table, slope_rows, lq, subln, n_pages=8,
                            lam_init=lam_init)
    ds_o = ds_o.reshape(db, DIFF_H, ds, DIFF_DV).transpose(0, 2, 1, 3).reshape(db * ds, DIFF_H * DIFF_DV)

    pad_t = 8
    ps3_pad = jnp.pad(ps3, ((0, 0), (0, pad_t - ds), (0, 0)))
    outs = _rwkv_prep(ps3_pad, firsts, mus, w0, w2p, a0, a2p, k_k, k_a, r_k, e1, e2, tm=pad_t)
    outs = [jnp.pad(o[:, :ds], ((0, 0), (0, CHUNK - ds), (0, 0))) for o in outs]
    rs_o, sp_s = _rwkv_scan(*outs, _pair_states(state_rwkv[l]), ln_g, ln_b)
    rs_o = rs_o[:, :ds].reshape(db * ds, RWKV_W)

    cq_pad = jnp.pad(ps3[:, :, OFF_CQ:OFF_LORA], ((0, 0), (0, 16 - ds), (0, 0)))
    mk_s = cache_mem_k[l].reshape(db, n_mem, CROSS_H * CROSS_DH)
    mv_s = cache_mem_v[l].reshape(db, n_mem, CROSS_H * CROSS_DH)
    cs_o = _cross_attn(cq_pad, 0, mk_s, 0, mv_s, 0, tq=16)[:, :ds].reshape(db * ds, CROSS_H * CROSS_DH)
    y_s = _merge_out(xs2, ds_o, rs_o, cs_o, ps2, w_out_b, g_out, tm=db * ds).reshape(db, ds, d)
    nk_s = ps3[:, :, OFF_DK:OFF_DV].reshape(1, db, ds, DIFF_H, DIFF_DV)
    nv_s = ps3[:, :, OFF_DV:OFF_CQ].reshape(1, db, ds, DIFF_H, DIFF_DV)
    ns_s = _unpair_states(sp_s)[None]

    return (y_p, y_s, nk_p, nv_p, nmk_p, nmv_p, ns_p, hs_p[None], nk_s, nv_s, ns_s, hs_s[None])
```

```python
import functools
import math

import jax
import jax.numpy as jnp
import numpy as np
from jax import lax
from jax.experimental import pallas as pl
from jax.experimental.pallas import tpu as pltpu

F32 = jnp.float32
BF16 = jnp.bfloat16

LANES = 128
MIB = 1024 * 1024

DIFF_H = 4
DIFF_DV = 128
DIFF_DH = 64
RWKV_N = 64
RWKV_H = 16
RWKV_W = RWKV_H * RWKV_N
CROSS_H = 4
CROSS_DH = 128
LORA = 64
RWKV_GN_EPS = 64e-5
CHUNK = 64
NEG = -1e30

OFF_GATE = 0
OFF_R = 2048
OFF_K = 3072
OFF_V = 4096
OFF_DQ = 5120
OFF_DK = 5632
OFF_DV = 6144
OFF_CQ = 6656
OFF_LORA = 7168
IN_COLS = 7296


def _params(sem, vmem_mib):
    return pltpu.CompilerParams(dimension_semantics=sem, vmem_limit_bytes=vmem_mib * MIB)


def _split2(x):
    hi = x.astype(BF16)
    lo = (x - hi.astype(F32)).astype(BF16)
    return hi, lo


def _dot(a, b, dims=(((1,), (0,)), ((), ()))):
    return lax.dot_general(a, b, dims, preferred_element_type=F32)


NN = (((1,), (0,)), ((), ()))
NT = (((1,), (1,)), ((), ()))
TN = (((0,), (0,)), ((), ()))


def _dot3(a, b, dims=NN):
    ah, al = _split2(a)
    bh, bl = _split2(b)
    return _dot(ah, bh, dims) + (_dot(ah, bl, dims) + _dot(al, bh, dims))


def _dot3s(a, b, dims=NN):
    ah, al = a
    bh, bl = b
    return _dot(ah, bh, dims) + (_dot(ah, bl, dims) + _dot(al, bh, dims))


def _dot_sel(x, sel):
    h, l = _split2(x)
    return _dot(h, sel) + _dot(l, sel)


def _norm_matmul_kernel(x_ref, g_ref, w_ref, o_ref, h_ref, *, normalize, eps):
    @pl.when(pl.program_id(1) == 0)
    def _():
        x = x_ref[...]
        if normalize:
            x = x * lax.rsqrt(jnp.mean(x * x, axis=-1, keepdims=True) + eps) * g_ref[...]
        h_ref[...] = x.astype(BF16)

    o_ref[...] = jnp.dot(h_ref[...], w_ref[...], preferred_element_type=F32)


def _norm_matmul(x, g, w, *, normalize, tm, tn, vmem_mib):
    m, d = x.shape
    n = w.shape[1]
    assert m % tm == 0 and n % tn == 0
    return pl.pallas_call(
        functools.partial(_norm_matmul_kernel, normalize=normalize, eps=1e-6),
        grid=(m // tm, n // tn),
        in_specs=[pl.BlockSpec((tm, d), lambda i, j: (i, 0)),
                  pl.BlockSpec((1, d), lambda i, j: (0, 0)),
                  pl.BlockSpec((d, tn), lambda i, j: (0, j))],
        out_specs=pl.BlockSpec((tm, tn), lambda i, j: (i, j)),
        out_shape=jax.ShapeDtypeStruct((m, n), F32),
        scratch_shapes=[pltpu.VMEM((tm, d), BF16)],
        compiler_params=_params(("arbitrary", "arbitrary"), vmem_mib),
        name="norm_matmul",
    )(x, g, w)


def _rmsnorm_kernel(x_ref, g_ref, o_ref, *, eps):
    x = x_ref[...]
    o_ref[...] = x * lax.rsqrt(jnp.mean(x * x, axis=-1, keepdims=True) + eps) * g_ref[...]


def _pad_rows(x, mult):
    pad = -x.shape[0] % mult
    return jnp.pad(x, ((0, pad), (0, 0))) if pad else x


def _rmsnorm_rows(x, g):
    n = x.shape[0]
    xp = _pad_rows(x, 8)
    return pl.pallas_call(
        functools.partial(_rmsnorm_kernel, eps=1e-6),
        out_shape=jax.ShapeDtypeStruct(xp.shape, F32),
        name="rmsnorm_rows",
    )(xp, g)[:n]


def _diff_lambda(lq, lam_init):
    t1 = jnp.sum(lq[0:1] * lq[1:2], axis=-1, keepdims=True)
    t2 = jnp.sum(lq[2:3] * lq[3:4], axis=-1, keepdims=True)
    return jnp.exp(t1) - jnp.exp(t2) + lam_init


def _diff_finish(acc, l, lq, subln, rows, lam_init):
    o12 = acc / l
    o = o12[:rows] - _diff_lambda(lq, lam_init) * o12[rows:]
    o = o * lax.rsqrt(jnp.mean(o * o, axis=-1, keepdims=True) + 1e-5) * subln
    return o * (1.0 - lam_init)


def _stack_maps(q):
    q = q * (DIFF_DH ** -0.5)
    lane = lax.broadcasted_iota(jnp.int32, q.shape, 1)
    q1 = jnp.where(lane < DIFF_DH, q, 0.0)
    q2 = jnp.where(lane >= DIFF_DH, q, 0.0)
    return jnp.concatenate([q1, q2], axis=0).astype(BF16)


def _diff_attn_kernel(qi_ref, kj_ref, q_ref, k_ref, v_ref, slope_ref, lq_ref, subln_ref, o_ref,
                      qq_ref, m_ref, l_ref, acc_ref, *, tq, lam_init):
    t = pl.program_id(2)
    i = qi_ref[t]
    j = kj_ref[t]

    @pl.when(j == 0)
    def _():
        qq_ref[...] = _stack_maps(q_ref[0])
        m_ref[...] = jnp.full(m_ref.shape, NEG, F32)
        l_ref[...] = jnp.zeros(l_ref.shape, F32)
        acc_ref[...] = jnp.zeros(acc_ref.shape, F32)

    kb = k_ref[0].astype(BF16)
    vb = v_ref[0].astype(BF16)
    krow = lax.broadcasted_iota(jnp.int32, (tq, LANES), 0)
    kbias = slope_ref[0][:, :1] * ((j - i) * tq + krow).astype(F32)
    kbias = jnp.tile(kbias, (1, 2 * tq // LANES))

    def accumulate(diagonal):
        s = _dot(kb, qq_ref[...], NT) + kbias
        if diagonal:
            qry = lax.broadcasted_iota(jnp.int32, s.shape, 1)
            qry = jnp.where(qry >= tq, qry - tq, qry)
            s = jnp.where(lax.broadcasted_iota(jnp.int32, s.shape, 0) <= qry, s, NEG)
        m_prev = m_ref[...]
        m_new = jnp.maximum(m_prev, jnp.max(s, axis=0, keepdims=True))
        alpha = jnp.exp(m_prev - m_new)
        p = jnp.exp(s - m_new)
        l_ref[...] = alpha * l_ref[...] + jnp.sum(p, axis=0, keepdims=True)
        acc_ref[...] = alpha * acc_ref[...] + _dot(vb, p.astype(BF16), TN)
        m_ref[...] = m_new

    @pl.when(j < i)
    def _():
        accumulate(False)

    @pl.when(j == i)
    def _():
        accumulate(True)
        o12 = acc_ref[...] / l_ref[...]
        o = o12[:, :tq] - _diff_lambda(lq_ref[...], lam_init) * o12[:, tq:]
        o = o * lax.rsqrt(jnp.mean(o * o, axis=0, keepdims=True) + 1e-5) * subln_ref[...]
        o_ref[0] = (o * (1.0 - lam_init)).T


def _diff_attn(p3, slopes, lambda_qk, subln_col, *, tq, lam_init):
    b, t, _ = p3.shape
    nq = t // tq
    pairs = [(i, j) for i in range(nq) for j in range(i + 1)]
    qi = jnp.asarray(np.array([p[0] for p in pairs], np.int32))
    kj = jnp.asarray(np.array([p[1] for p in pairs], np.int32))
    qb, kb, vb = OFF_DQ // DIFF_DV, OFF_DK // DIFF_DV, OFF_DV // DIFF_DV
    grid_spec = pltpu.PrefetchScalarGridSpec(
        num_scalar_prefetch=2,
        grid=(b, DIFF_H, len(pairs)),
        in_specs=[pl.BlockSpec((1, tq, DIFF_DV), lambda bb, h, s, qi, kj: (bb, qi[s], qb + h)),
                  pl.BlockSpec((1, tq, DIFF_DV), lambda bb, h, s, qi, kj: (bb, kj[s], kb + h)),
                  pl.BlockSpec((1, tq, DIFF_DV), lambda bb, h, s, qi, kj: (bb, kj[s], vb + h)),
                  pl.BlockSpec((1, 1, LANES), lambda bb, h, s, qi, kj: (h, 0, 0)),
                  pl.BlockSpec((4, DIFF_DH), lambda bb, h, s, qi, kj: (0, 0)),
                  pl.BlockSpec((DIFF_DV, 1), lambda bb, h, s, qi, kj: (0, 0))],
        out_specs=pl.BlockSpec((1, tq, DIFF_DV), lambda bb, h, s, qi, kj: (bb, qi[s], h)),
        scratch_shapes=[pltpu.VMEM((2 * tq, DIFF_DV), BF16), pltpu.VMEM((1, 2 * tq), F32),
                        pltpu.VMEM((1, 2 * tq), F32), pltpu.VMEM((DIFF_DV, 2 * tq), F32)],
    )
    return pl.pallas_call(
        functools.partial(_diff_attn_kernel, tq=tq, lam_init=lam_init),
        grid_spec=grid_spec,
        out_shape=jax.ShapeDtypeStruct((b, t, DIFF_H * DIFF_DV), F32),
        compiler_params=_params(("arbitrary", "arbitrary", "arbitrary"), 40),
        name="diff_attn",
    )(qi, kj, p3, p3, p3, slopes, lambda_qk, subln_col)


def _paged_diff_attn_kernel(pt_ref, q_ref, *refs, n_pages, page, n_new, past_len, lam_init):
    k_refs = refs[:n_pages]
    v_refs = refs[n_pages:2 * n_pages]
    kn_ref, vn_ref, slope_ref, lq_ref, subln_ref, o_ref, m_ref, l_ref, acc_ref = refs[2 * n_pages:]
    s_id = pl.program_id(1)
    rows = 2 * DIFF_H * n_new

    @pl.when(s_id == 0)
    def _():
        m_ref[...] = jnp.full(m_ref.shape, NEG, F32)
        l_ref[...] = jnp.zeros(l_ref.shape, F32)
        acc_ref[...] = jnp.zeros(acc_ref.shape, F32)

    qb = (q_ref[0] * (DIFF_DH ** -0.5)).astype(BF16)
    slope = slope_ref[...][:, :1]
    log_new, log_h = n_new.bit_length() - 1, DIFF_H.bit_length() - 1

    def coords(ncol):
        row = lax.broadcasted_iota(jnp.int32, (rows, ncol), 0)
        col = lax.broadcasted_iota(jnp.int32, (rows, ncol), 1)
        row_h = lax.shift_right_logical(row, log_new) & (DIFF_H - 1)
        return row & (n_new - 1), (col & (DIFF_H - 1)) == row_h, lax.shift_right_logical(col, log_h)

    def update(ss, vbs, carry):
        m_prev, l_prev, acc_prev = carry
        m_new = m_prev
        for s in ss:
            m_new = jnp.maximum(m_new, jnp.max(s, axis=-1, keepdims=True))
        alpha = jnp.exp(m_prev - m_new)
        l_new = alpha * l_prev
        acc_new = alpha * acc_prev
        for s, vb in zip(ss, vbs):
            p = jnp.exp(s - m_new)
            l_new = l_new + jnp.sum(p, axis=-1, keepdims=True)
            acc_new = acc_new + _dot(p.astype(BF16), vb)
        return m_new, l_new, acc_new

    _, same_head, tok = coords(page * DIFF_H)
    ss, vbs = [], []
    for i in range(n_pages):
        kb = k_refs[i][0].astype(BF16)
        vbs.append(v_refs[i][0].astype(BF16))
        k_pos = ((s_id * n_pages + i) * page - past_len + tok).astype(F32)
        ss.append(jnp.where(same_head, _dot(qb, kb, NT) + slope * k_pos, NEG))
    m_ref[...], l_ref[...], acc_ref[...] = update(ss, vbs, (m_ref[...], l_ref[...], acc_ref[...]))

    @pl.when(s_id == pl.num_programs(1) - 1)
    def _():
        kb = kn_ref[0].astype(BF16)
        vb = vn_ref[0].astype(BF16)
        row_tn, same_head_n, tok_n = coords(n_new * DIFF_H)
        k_pos = tok_n.astype(F32)
        s = jnp.where(same_head_n & (tok_n <= row_tn), _dot(qb, kb, NT) + slope * k_pos, NEG)
        _, l, acc = update([s], [vb], (m_ref[...], l_ref[...], acc_ref[...]))
        o_ref[0] = _diff_finish(acc, l, lq_ref[...], subln_ref[...], rows // 2, lam_init)


def _paged_diff_attn(qm, cache_k, cache_v, k_new, v_new, page_table, slope_rows, lambda_qk, subln, *,
                     n_pages, lam_init):
    db, rows, _ = qm.shape
    n_pool, page = cache_k.shape[0], cache_k.shape[1] // DIFF_H
    n_tab = page_table.shape[1]
    n_new = rows // (2 * DIFF_H)
    assert n_tab % n_pages == 0 and n_new & (n_new - 1) == 0 and DIFF_H & (DIFF_H - 1) == 0

    def page_spec(i):
        return pl.BlockSpec((1, page * DIFF_H, DIFF_DV), lambda b, s, pt: (pt[b, s * n_pages + i], 0, 0))

    grid_spec = pltpu.PrefetchScalarGridSpec(
        num_scalar_prefetch=1,
        grid=(db, n_tab // n_pages),
        in_specs=([pl.BlockSpec((1, rows, DIFF_DV), lambda b, s, pt: (b, 0, 0))]
                  + [page_spec(i) for i in range(n_pages)] * 2
                  + [pl.BlockSpec((1, n_new * DIFF_H, DIFF_DV), lambda b, s, pt: (b, 0, 0)),
                     pl.BlockSpec((1, n_new * DIFF_H, DIFF_DV), lambda b, s, pt: (b, 0, 0)),
                     pl.BlockSpec((rows, LANES), lambda b, s, pt: (0, 0)),
                     pl.BlockSpec((4, DIFF_DH), lambda b, s, pt: (0, 0)),
                     pl.BlockSpec((1, DIFF_DV), lambda b, s, pt: (0, 0))]),
        out_specs=pl.BlockSpec((1, rows // 2, DIFF_DV), lambda b, s, pt: (b, 0, 0)),
        scratch_shapes=[pltpu.VMEM((rows, 1), F32), pltpu.VMEM((rows, 1), F32), pltpu.VMEM((rows, DIFF_DV), F32)],
    )
    return pl.pallas_call(
        functools.partial(_paged_diff_attn_kernel, n_pages=n_pages, page=page, n_new=n_new,
                          past_len=n_tab * page, lam_init=lam_init),
        grid_spec=grid_spec,
        out_shape=jax.ShapeDtypeStruct((db, rows // 2, DIFF_DV), F32),
        compiler_params=_params(("arbitrary", "arbitrary"), 32),
        name="paged_diff_attn",
    )(page_table, qm, *([cache_k] * n_pages), *([cache_v] * n_pages), k_new, v_new, slope_rows, lambda_qk, subln)


def _cross_attn_kernel(q_ref, k_ref, v_ref, o_ref, *, scale):
    q = q_ref[0].astype(BF16)
    k = k_ref[0].astype(BF16)
    v = v_ref[0].astype(BF16)
    s = _dot(q, k, NT) * scale
    p = jnp.exp(s - jnp.max(s, axis=-1, keepdims=True))
    l = jnp.sum(p, axis=-1, keepdims=True)
    o_ref[0] = _dot(p.astype(BF16), v) / l


def _cross_attn(q_arr, q_off, k_arr, k_off, v_arr, v_off, *, tq):
    b, t, _ = q_arr.shape
    n_mem = k_arr.shape[1]
    qb, kb, vb = q_off // CROSS_DH, k_off // CROSS_DH, v_off // CROSS_DH
    return pl.pallas_call(
        functools.partial(_cross_attn_kernel, scale=CROSS_DH ** -0.5),
        grid=(b, CROSS_H, t // tq),
        in_specs=[pl.BlockSpec((1, tq, CROSS_DH), lambda bb, h, i: (bb, i, qb + h)),
                  pl.BlockSpec((1, n_mem, CROSS_DH), lambda bb, h, i: (bb, 0, kb + h)),
                  pl.BlockSpec((1, n_mem, CROSS_DH), lambda bb, h, i: (bb, 0, vb + h))],
        out_specs=pl.BlockSpec((1, tq, CROSS_DH), lambda bb, h, i: (bb, i, h)),
        out_shape=jax.ShapeDtypeStruct((b, t, CROSS_H * CROSS_DH), F32),
        compiler_params=_params(("arbitrary", "arbitrary", "arbitrary"), 32),
        name="cross_attn",
    )(q_arr, k_arr, v_arr)


def _rwkv_prep_kernel(r_ref, k_ref, v_ref, lo_ref, fr_ref, fk_ref, fv_ref, fl_ref,
                      mur_ref, muk_ref, muv_ref, mul_ref, w0_ref, w2_ref, a0_ref, a2_ref,
                      kk_ref, ka_ref, rk_ref, e1_ref, e2_ref,
                      ro_ref, lwo_ref, ko_ref, vo_ref, ao_ref, bo_ref, bonus_ref,
                      cr_ref, ck_ref, cv_ref, cl_ref):
    @pl.when(pl.program_id(1) == 0)
    def _():
        cr_ref[...] = fr_ref[0]
        ck_ref[...] = fk_ref[0]
        cv_ref[...] = fv_ref[0]
        cl_ref[...] = fl_ref[0]

    def shift_mix(x_ref, carry_ref, mu_ref):
        x = x_ref[0]
        rows = x.shape[0]
        prev = pltpu.roll(x, 1, axis=0)
        rid = lax.broadcasted_iota(jnp.int32, x.shape, 0)
        prev = jnp.where(rid == 0, carry_ref[...], prev)
        carry_ref[...] = x[rows - 1:rows, :]
        return x + mu_ref[...] * (prev - x)

    r = shift_mix(r_ref, cr_ref, mur_ref)
    k = shift_mix(k_ref, ck_ref, muk_ref)
    v = shift_mix(v_ref, cv_ref, muv_ref)
    lo = shift_mix(lo_ref, cl_ref, mul_ref)

    xw = -(w0_ref[...] + _dot3(jnp.tanh(lo), w2_ref[...]))
    softplus = jnp.maximum(xw, 0.0) + jnp.log(1.0 + jnp.exp(-jnp.abs(xw)))
    w = -softplus - 0.5
    lwo_ref[0] = -jnp.exp(w)
    a = jax.nn.sigmoid(a0_ref[...] + _dot3(lo, a2_ref[...]))

    def head_sum(x):
        return _dot_sel(_dot_sel(x, e1_ref[...]), e2_ref[...])

    kkr = k * kk_ref[...]
    kk = kkr / jnp.maximum(jnp.sqrt(head_sum(kkr * kkr)), 1e-12)
    k2 = k * (1.0 + (a - 1.0) * ka_ref[...])
    ro_ref[0] = r
    ko_ref[0] = k2
    vo_ref[0] = v
    ao_ref[0] = -kk
    bo_ref[0] = kk * a
    bonus_ref[0] = head_sum(r * k2 * rk_ref[...]) * v


def _rwkv_prep(p3, firsts, mus, w0, w2p, a0, a2p, k_k, k_a, r_k, e1, e2, *, tm):
    b, t, _ = p3.shape
    nt = t // tm
    wide = lambda blk: pl.BlockSpec((1, tm, RWKV_W), lambda bb, i: (bb, i, blk))
    vec = lambda n: pl.BlockSpec((1, n), lambda bb, i: (0, 0))
    first = lambda n: pl.BlockSpec((1, 1, n), lambda bb, i: (bb, 0, 0))
    out_wide = pl.BlockSpec((1, tm, RWKV_W), lambda bb, i: (bb, i, 0))
    in_specs = [wide(OFF_R // RWKV_W), wide(OFF_K // RWKV_W), wide(OFF_V // RWKV_W),
                pl.BlockSpec((1, tm, LANES), lambda bb, i: (bb, i, OFF_LORA // LANES)),
                first(RWKV_W), first(RWKV_W), first(RWKV_W), first(LANES),
                vec(RWKV_W), vec(RWKV_W), vec(RWKV_W), vec(LANES),
                vec(RWKV_W), pl.BlockSpec((LANES, RWKV_W), lambda bb, i: (0, 0)),
                vec(RWKV_W), pl.BlockSpec((LANES, RWKV_W), lambda bb, i: (0, 0)),
                vec(RWKV_W), vec(RWKV_W), vec(RWKV_W),
                pl.BlockSpec((RWKV_W, LANES), lambda bb, i: (0, 0)),
                pl.BlockSpec((LANES, RWKV_W), lambda bb, i: (0, 0))]
    shape = jax.ShapeDtypeStruct((b, t, RWKV_W), F32)
    return pl.pallas_call(
        _rwkv_prep_kernel,
        grid=(b, nt),
        in_specs=in_specs,
        out_specs=[out_wide] * 7,
        out_shape=[shape] * 7,
        scratch_shapes=[pltpu.VMEM((1, RWKV_W), F32), pltpu.VMEM((1, RWKV_W), F32),
                        pltpu.VMEM((1, RWKV_W), F32), pltpu.VMEM((1, LANES), F32)],
        compiler_params=_params(("arbitrary", "arbitrary"), 48),
        name="rwkv_prep",
    )(p3, p3, p3, p3, *firsts, *mus, w0, w2p, a0, a2p, k_k, k_a, r_k, e1, e2)


def _rwkv_scan_kernel(r_ref, lw_ref, k_ref, v_ref, a_ref, b_ref, bonus_ref, s0_ref, g_ref, bb_ref,
                      y_ref, sout_ref, s_ref, *, chunk, n_pairs):
    c = pl.program_id(1)
    c2 = 2 * chunk

    @pl.when(c == 0)
    def _():
        s_ref[...] = s0_ref[0]

    rid = lax.broadcasted_iota(jnp.int32, (c2, c2), 0)
    cid = lax.broadcasted_iota(jnp.int32, (c2, c2), 1)
    strict = rid > cid
    incl2 = jnp.concatenate([rid >= cid, rid >= cid], axis=1)
    tr = lax.broadcasted_iota(jnp.int32, (chunk, chunk), 0)
    tc = lax.broadcasted_iota(jnp.int32, (chunk, chunk), 1)
    tri = (tr >= tc).astype(BF16)
    srow = lax.broadcasted_iota(jnp.int32, (c2, LANES), 0)
    slane = lax.broadcasted_iota(jnp.int32, (c2, LANES), 1)
    own = (srow < chunk) == (slane < RWKV_N)

    def stack(x):
        return jnp.where(own, jnp.concatenate([x, x], axis=0), 0.0)

    def cat2(x, y, axis):
        return tuple(jnp.concatenate([p, q], axis=axis) for p, q in zip(x, y))

    pairs = range(n_pairs)
    sls = [pl.ds(p * LANES, LANES) for p in pairs]
    lhs, rhs, vm, g_end, s_prev = [], [], [], [], []
    for sl in sls:
        lw = lw_ref[0, :, sl]
        h1 = lw.astype(BF16)
        r1 = lw - h1.astype(F32)
        h2 = r1.astype(BF16)
        h3 = (r1 - h2.astype(F32)).astype(BF16)
        cum = _dot(tri, h1) + (_dot(tri, h2) + _dot(tri, h3))
        g = jnp.exp(cum)
        gi = jnp.exp(-cum)
        gp = jnp.exp(cum - lw)
        am = stack(a_ref[0, :, sl] * gp)
        rm = stack(r_ref[0, :, sl] * g)
        bm = stack(b_ref[0, :, sl] * gi)
        km = stack(k_ref[0, :, sl] * gi)
        lhs.append(_split2(jnp.concatenate([am, rm], axis=0)))
        rhs.append(_split2(jnp.concatenate([bm, km], axis=0)))
        vm.append(_split2(stack(v_ref[0, :, sl])))
        g_end.append(g[chunk - 1:chunk, :])
    for p in pairs:
        s_prev.append(s_ref[p])

    p1 = [_dot3s(lhs[p], rhs[p], NT) for p in pairs]
    p2 = [_dot3s(lhs[p], _split2(s_prev[p]), NT) for p in pairs]
    nm = [_split2(jnp.where(strict, p1[p][:c2, :c2], 0.0)) for p in pairs]
    aak = [_split2(jnp.where(strict, p1[p][:c2, c2:], 0.0)) for p in pairs]
    m2 = [_split2(jnp.where(incl2, p1[p][c2:, :], 0.0)) for p in pairs]
    u = [p2[p][:c2] + _dot3s(aak[p], vm[p]) for p in pairs]
    n = 1
    while 2 * n < chunk:
        us = [_split2(u[p]) for p in pairs]
        y = [_dot3s(nm[p], cat2(nm[p], us[p], 1)) for p in pairs]
        u = [u[p] + y[p][:, c2:] for p in pairs]
        nm = [_split2(y[p][:, :c2]) for p in pairs]
        n *= 2
    us = [_split2(u[p]) for p in pairs]
    u = [u[p] + _dot3s(nm[p], us[p]) for p in pairs]
    uv = [cat2(_split2(u[p]), vm[p], 0) for p in pairs]
    o = [p2[p][c2:] + _dot3s(m2[p], uv[p]) for p in pairs]
    for p in pairs:
        s_ref[p] = (s_prev[p] + _dot3s(uv[p], rhs[p], TN)) * g_end[p]

    for p, sl in zip(pairs, sls):
        mu = jnp.sum(o[p], axis=-1, keepdims=True) * (1.0 / RWKV_N)
        d = jnp.where(own, o[p] - mu, 0.0)
        var = jnp.sum(d * d, axis=-1, keepdims=True) * (1.0 / RWKV_N)
        yn = d * lax.rsqrt(var + RWKV_GN_EPS)
        y = yn[:chunk] + yn[chunk:]
        y_ref[0, :, sl] = y * g_ref[:, sl] + bb_ref[:, sl] + bonus_ref[0, :, sl]

    @pl.when(c == pl.num_programs(1) - 1)
    def _():
        sout_ref[0] = s_ref[...]


def _rwkv_scan(r, lw, k, v, a, b, bonus, s0_pairs, ln_g, ln_b):
    bsz, t, _ = r.shape
    n_pairs = RWKV_H // 2
    blk = pl.BlockSpec((1, CHUNK, RWKV_W), lambda bb, c: (bb, c, 0))
    st = pl.BlockSpec((1, n_pairs, LANES, LANES), lambda bb, c: (bb, 0, 0, 0))
    vec = pl.BlockSpec((1, RWKV_W), lambda bb, c: (0, 0))
    return pl.pallas_call(
        functools.partial(_rwkv_scan_kernel, chunk=CHUNK, n_pairs=n_pairs),
        grid=(bsz, t // CHUNK),
        in_specs=[blk] * 7 + [st, vec, vec],
        out_specs=[blk, st],
        out_shape=[jax.ShapeDtypeStruct((bsz, t, RWKV_W), F32),
                   jax.ShapeDtypeStruct((bsz, n_pairs, LANES, LANES), F32)],
        scratch_shapes=[pltpu.VMEM((n_pairs, LANES, LANES), F32)],
        compiler_params=_params(("arbitrary", "arbitrary"), 32),
        name="rwkv_scan",
    )(r, lw, k, v, a, b, bonus, s0_pairs, ln_g, ln_b)


def _pair_states(s):
    bsz = s.shape[0]
    s = s.reshape(bsz, RWKV_H // 2, 2, RWKV_N, RWKV_N)
    z = jnp.zeros_like(s[:, :, 0])
    top = jnp.concatenate([s[:, :, 0], z], axis=-1)
    bot = jnp.concatenate([z, s[:, :, 1]], axis=-1)
    return jnp.concatenate([top, bot], axis=-2)


def _unpair_states(sp):
    bsz = sp.shape[0]
    even = sp[:, :, :RWKV_N, :RWKV_N]
    odd = sp[:, :, RWKV_N:, RWKV_N:]
    return jnp.stack([even, odd], axis=2).reshape(bsz, RWKV_H, RWKV_N, RWKV_N)


def _merge_out_kernel(x_ref, d_ref, r_ref, c_ref, gate_ref, w_ref, g_ref, o_ref, *, eps):
    gate = gate_ref[...]
    sg = gate * jax.nn.sigmoid(gate)
    nd, nr = d_ref.shape[1], r_ref.shape[1]
    acc = _dot((d_ref[...] * sg[:, :nd]).astype(BF16), w_ref[0:nd, :])
    acc += _dot((r_ref[...] * sg[:, nd:nd + nr]).astype(BF16), w_ref[nd:nd + nr, :])
    acc += _dot((c_ref[...] * sg[:, nd + nr:]).astype(BF16), w_ref[nd + nr:, :])
    x = x_ref[...] + acc
    o_ref[...] = x * lax.rsqrt(jnp.mean(x * x, axis=-1, keepdims=True) + eps) * g_ref[...]


def _merge_out(x, d_o, r_o, c_o, p2, w_out, norm_out, *, tm):
    m, d = x.shape
    row = lambda n: pl.BlockSpec((tm, n), lambda i: (i, 0))
    return pl.pallas_call(
        functools.partial(_merge_out_kernel, eps=1e-6),
        grid=(m // tm,),
        in_specs=[row(d), row(d_o.shape[1]), row(r_o.shape[1]), row(c_o.shape[1]),
                  pl.BlockSpec((tm, d), lambda i: (i, OFF_GATE // d)),
                  pl.BlockSpec((d, d), lambda i: (0, 0)),
                  pl.BlockSpec((1, d), lambda i: (0, 0))],
        out_specs=row(d),
        out_shape=jax.ShapeDtypeStruct((m, d), F32),
        compiler_params=_params(("arbitrary",), 48),
        name="merge_out",
    )(x, d_o, r_o, c_o, p2, w_out, norm_out)


def _reorder_in_cols(w):
    dq, z, cq, gate = 0, 1536, 1536 + 3200, 1536 + 3200 + 512
    return jnp.concatenate([w[..., gate:], w[..., z:z + 3 * RWKV_W], w[..., dq:z], w[..., cq:gate],
                            w[..., z + 3 * RWKV_W:cq]], axis=-1)


def kernel(x_prompt, x_sample, cache_k, cache_v, cache_mem_k, cache_mem_v, state_rwkv, state_shift, page_table,
           mem_prompt, norm_in, w_in, norm_mem, w_mem_kv, lambda_qk, diff_subln, rwkv_mu, rwkv_w0, rwkv_w2, rwkv_a0,
           rwkv_a2, rwkv_k_k, rwkv_k_a, rwkv_r_k, rwkv_ln_g, rwkv_ln_b, w_out, norm_out):
    bsz, seq, d = x_prompt.shape
    db, ds, _ = x_sample.shape
    depth = w_in.shape[0]
    assert depth == 1
    l = 0
    lam_init = 0.8 - 0.6 * math.exp(-0.3 * l)
    n_mem = mem_prompt.shape[1]

    w_in_b = _reorder_in_cols(w_in[l]).astype(BF16)
    w_shift_b = jnp.concatenate([w_in_b[:, OFF_R:OFF_DQ], w_in_b[:, OFF_LORA:]], axis=1)
    w_mem_b = w_mem_kv[l].astype(BF16)
    w_out_b = w_out[l].astype(BF16)
    g_in = norm_in[l][None]
    g_mem = norm_mem[l][None]
    g_out = norm_out[None]
    mu = rwkv_mu[l]
    mus = (mu[None, :RWKV_W], mu[None, RWKV_W:2 * RWKV_W], mu[None, 2 * RWKV_W:3 * RWKV_W], mu[None, 3 * RWKV_W:])
    zl = jnp.zeros((LORA, RWKV_W), F32)
    w2p = jnp.concatenate([rwkv_w2[l], zl], axis=0)
    a2p = jnp.concatenate([zl, rwkv_a2[l]], axis=0)
    vec = lambda x: x.reshape(1, RWKV_W)
    w0, a0, k_k, k_a, r_k = vec(rwkv_w0[l]), vec(rwkv_a0[l]), vec(rwkv_k_k[l]), vec(rwkv_k_a[l]), vec(rwkv_r_k[l])
    ln_g, ln_b = vec(rwkv_ln_g[l]), vec(rwkv_ln_b[l])
    head_of = np.arange(RWKV_W) // RWKV_N
    e1 = jnp.asarray((head_of[:, None] == np.arange(LANES)[None, :]).astype(np.float32), BF16)
    e2 = jnp.asarray((np.arange(LANES)[:, None] == head_of[None, :]).astype(np.float32), BF16)
    slopes = 2.0 ** (-8.0 * np.arange(1, DIFF_H + 1, dtype=np.float64) / DIFF_H)
    slopes_h = jnp.asarray(np.broadcast_to(slopes[:, None, None], (DIFF_H, 1, LANES)).astype(np.float32))
    subln = diff_subln[l][None]
    lq = lambda_qk[l]

    xp2 = x_prompt.reshape(bsz * seq, d)
    p2 = _norm_matmul(xp2, g_in, w_in_b, normalize=True, tm=512, tn=IN_COLS // 3, vmem_mib=56)
    p3 = p2.reshape(bsz, seq, IN_COLS)
    hs_p = _rmsnorm_rows(x_prompt[:, -1], g_in)
    d_o = _diff_attn(p3, slopes_h, lq, subln.reshape(DIFF_DV, 1), tq=512, lam_init=lam_init)
    zeros_first = (jnp.zeros((bsz, 1, RWKV_W), F32),) * 3 + (jnp.zeros((bsz, 1, LANES), F32),)
    r, lw, k2, v, a, b, bonus = _rwkv_prep(p3, zeros_first, mus, w0, w2p, a0, a2p, k_k, k_a, r_k, e1, e2, tm=256)
    s0_p = jnp.zeros((bsz, RWKV_H // 2, LANES, LANES), F32)
    r_o, sp_p = _rwkv_scan(r, lw, k2, v, a, b, bonus, s0_p, ln_g, ln_b)
    mem_kv = _norm_matmul(mem_prompt.reshape(bsz * n_mem, d), g_mem, w_mem_b, normalize=True, tm=256, tn=512,
                          vmem_mib=32).reshape(bsz, n_mem, 2 * CROSS_H * CROSS_DH)
    c_o = _cross_attn(p3, OFF_CQ, mem_kv, 0, mem_kv, CROSS_H * CROSS_DH, tq=1024)
    y_p = _merge_out(xp2, d_o.reshape(bsz * seq, -1), r_o.reshape(bsz * seq, -1), c_o.reshape(bsz * seq, -1), p2,
                     w_out_b, g_out, tm=256).reshape(bsz, seq, d)
    nk_p = p3[:, :, OFF_DK:OFF_DV].reshape(1, bsz, seq, DIFF_H, DIFF_DV)
    nv_p = p3[:, :, OFF_DV:OFF_CQ].reshape(1, bsz, seq, DIFF_H, DIFF_DV)
    nmk_p = mem_kv[:, :, :CROSS_H * CROSS_DH].reshape(1, bsz, n_mem, CROSS_H, CROSS_DH)
    nmv_p = mem_kv[:, :, CROSS_H * CROSS_DH:].reshape(1, bsz, n_mem, CROSS_H, CROSS_DH)
    ns_p = _unpair_states(sp_p)[None]

    xs2 = x_sample.reshape(db * ds, d)
    ps2 = _norm_matmul(xs2, g_in, w_in_b, normalize=True, tm=db * ds, tn=IN_COLS // 3, vmem_mib=56)
    ps3 = ps2.reshape(db, ds, IN_COLS)
    hs_s = _rmsnorm_rows(x_sample[:, -1], g_in)
    shift_rows = _pad_rows(state_shift[l], 16)
    z_first = _norm_matmul(shift_rows, g_in, w_shift_b, normalize=False, tm=shift_rows.shape[0],
                           tn=w_shift_b.shape[1], vmem_mib=56)[:db]
    firsts = (z_first[:, None, :RWKV_W], z_first[:, None, RWKV_W:2 * RWKV_W],
              z_first[:, None, 2 * RWKV_W:3 * RWKV_W], z_first[:, None, 3 * RWKV_W:])

    q5 = ps3[:, :, OFF_DQ:OFF_DK].reshape(db, ds, DIFF_H, 2, DIFF_DH)
    zq = jnp.zeros_like(q5[..., 0, :])
    qm = jnp.stack([jnp.concatenate([q5[..., 0, :], zq], -1), jnp.concatenate([zq, q5[..., 1, :]], -1)], axis=1)
    qm = qm.transpose(0, 1, 3, 2, 4).reshape(db, 2 * DIFF_H * ds, DIFF_DV)
    k_new = ps3[:, :, OFF_DK:OFF_DV].reshape(db, ds * DIFF_H, DIFF_DV)
    v_new = ps3[:, :, OFF_DV:OFF_CQ].reshape(db, ds * DIFF_H, DIFF_DV)
    row_head = (np.arange(2 * DIFF_H * ds) // ds) % DIFF_H
    slope_rows = jnp.asarray(np.broadcast_to(slopes[row_head][:, None], (2 * DIFF_H * ds, LANES)).astype(np.float32))
    n_pool, page = cache_k.shape[1], cache_k.shape[2]
    ck = cache_k[l].reshape(n_pool, page * DIFF_H, DIFF_DV)
    cv = cache_v[l].reshape(n_pool, page * DIFF_H, DIFF_DV)
    ds_o = _paged_diff_attn(qm, ck, cv, k_new, v_new, page_table, slope_rows, lq, subln, n_pages=8,
                            lam_init=lam_init)
    ds_o = ds_o.reshape(db, DIFF_H, ds, DIFF_DV).transpose(0, 2, 1, 3).reshape(db * ds, DIFF_H * DIFF_DV)

    pad_t = 8
    ps3_pad = jnp.pad(ps3, ((0, 0), (0, pad_t - ds), (0, 0)))
    outs = _rwkv_prep(ps3_pad, firsts, mus, w0, w2p, a0, a2p, k_k, k_a, r_k, e1, e2, tm=pad_t)
    outs = [jnp.pad(o[:, :ds], ((0, 0), (0, CHUNK - ds), (0, 0))) for o in outs]
    rs_o, sp_s = _rwkv_scan(*outs, _pair_states(state_rwkv[l]), ln_g, ln_b)
    rs_o = rs_o[:, :ds].reshape(db * ds, RWKV_W)

    cq_pad = jnp.pad(ps3[:, :, OFF_CQ:OFF_LORA], ((0, 0), (0, 16 - ds), (0, 0)))
    mk_s = cache_mem_k[l].reshape(db, n_mem, CROSS_H * CROSS_DH)
    mv_s = cache_mem_v[l].reshape(db, n_mem, CROSS_H * CROSS_DH)
    cs_o = _cross_attn(cq_pad, 0, mk_s, 0, mv_s, 0, tq=16)[:, :ds].reshape(db * ds, CROSS_H * CROSS_DH)
    y_s = _merge_out(xs2, ds_o, rs_o, cs_o, ps2, w_out_b, g_out, tm=db * ds).reshape(db, ds, d)
    nk_s = ps3[:, :, OFF_DK:OFF_DV].reshape(1, db, ds, DIFF_H, DIFF_DV)
    nv_s = ps3[:, :, OFF_DV:OFF_CQ].reshape(1, db, ds, DIFF_H, DIFF_DV)
    ns_s = _unpair_states(sp_s)[None]

    return (y_p, y_s, nk_p, nv_p, nmk_p, nmv_p, ns_p, hs_p[None], nk_s, nv_s, ns_s, hs_s[None])
```

```python
import functools
import math

import jax
import jax.numpy as jnp
import numpy as np
from jax import lax
from jax.experimental import pallas as pl
from jax.experimental.pallas import tpu as pltpu

F32 = jnp.float32
BF16 = jnp.bfloat16

LANES = 128
MIB = 1024 * 1024

DIFF_H = 4
DIFF_DV = 128
DIFF_DH = 64
RWKV_N = 64
RWKV_H = 16
RWKV_W = RWKV_H * RWKV_N
CROSS_H = 4
CROSS_DH = 128
LORA = 64
RWKV_GN_EPS = 64e-5
CHUNK = 64
NEG = -1e30

OFF_GATE = 0
OFF_R = 2048
OFF_K = 3072
OFF_V = 4096
OFF_DQ = 5120
OFF_DK = 5632
OFF_DV = 6144
OFF_CQ = 6656
OFF_LORA = 7168
IN_COLS = 7296


def _params(sem, vmem_mib):
    return pltpu.CompilerParams(dimension_semantics=sem, vmem_limit_bytes=vmem_mib * MIB)


def _split2(x):
    hi = x.astype(BF16)
    lo = (x - hi.astype(F32)).astype(BF16)
    return hi, lo


def _dot(a, b, dims=(((1,), (0,)), ((), ()))):
    return lax.dot_general(a, b, dims, preferred_element_type=F32)


NN = (((1,), (0,)), ((), ()))
NT = (((1,), (1,)), ((), ()))
TN = (((0,), (0,)), ((), ()))


def _dot3(a, b, dims=NN):
    ah, al = _split2(a)
    bh, bl = _split2(b)
    return _dot(ah, bh, dims) + (_dot(ah, bl, dims) + _dot(al, bh, dims))


def _dot3s(a, b, dims=NN, passes=3):
    ah, al = a
    bh, bl = b
    out = _dot(ah, bh, dims)
    if passes >= 2:
        out = out + _dot(al, bh, dims)
    if passes >= 3:
        out = out + _dot(ah, bl, dims)
    return out


SCAN_PASSES = dict(intra=1, state_read=1, av=1, solve=1, out=1, update=1)


def _dot_sel(x, sel):
    h, l = _split2(x)
    return _dot(h, sel) + _dot(l, sel)


def _norm_matmul_kernel(x_ref, g_ref, w_ref, o_ref, h_ref, *, normalize, eps):
    @pl.when(pl.program_id(1) == 0)
    def _():
        x = x_ref[...]
        if normalize:
            x = x * lax.rsqrt(jnp.mean(x * x, axis=-1, keepdims=True) + eps) * g_ref[...]
        h_ref[...] = x.astype(BF16)

    o_ref[...] = jnp.dot(h_ref[...], w_ref[...], preferred_element_type=F32)


def _norm_matmul(x, g, w, *, normalize, tm, tn, vmem_mib):
    m, d = x.shape
    n = w.shape[1]
    assert m % tm == 0 and n % tn == 0
    return pl.pallas_call(
        functools.partial(_norm_matmul_kernel, normalize=normalize, eps=1e-6),
        grid=(m // tm, n // tn),
        in_specs=[pl.BlockSpec((tm, d), lambda i, j: (i, 0)),
                  pl.BlockSpec((1, d), lambda i, j: (0, 0)),
                  pl.BlockSpec((d, tn), lambda i, j: (0, j))],
        out_specs=pl.BlockSpec((tm, tn), lambda i, j: (i, j)),
        out_shape=jax.ShapeDtypeStruct((m, n), F32),
        scratch_shapes=[pltpu.VMEM((tm, d), BF16)],
        compiler_params=_params(("arbitrary", "arbitrary"), vmem_mib),
        name="norm_matmul",
    )(x, g, w)


def _rmsnorm_kernel(x_ref, g_ref, o_ref, *, eps):
    x = x_ref[...]
    o_ref[...] = x * lax.rsqrt(jnp.mean(x * x, axis=-1, keepdims=True) + eps) * g_ref[...]


def _pad_rows(x, mult):
    pad = -x.shape[0] % mult
    return jnp.pad(x, ((0, pad), (0, 0))) if pad else x


def _rmsnorm_rows(x, g):
    n = x.shape[0]
    xp = _pad_rows(x, 8)
    return pl.pallas_call(
        functools.partial(_rmsnorm_kernel, eps=1e-6),
        out_shape=jax.ShapeDtypeStruct(xp.shape, F32),
        name="rmsnorm_rows",
    )(xp, g)[:n]


def _diff_lambda(lq, lam_init):
    t1 = jnp.sum(lq[0:1] * lq[1:2], axis=-1, keepdims=True)
    t2 = jnp.sum(lq[2:3] * lq[3:4], axis=-1, keepdims=True)
    return jnp.exp(t1) - jnp.exp(t2) + lam_init


def _diff_finish(acc, l, lq, subln, rows, lam_init):
    o12 = acc / l
    o = o12[:rows] - _diff_lambda(lq, lam_init) * o12[rows:]
    o = o * lax.rsqrt(jnp.mean(o * o, axis=-1, keepdims=True) + 1e-5) * subln
    return o * (1.0 - lam_init)


def _stack_maps(q):
    q = q * (DIFF_DH ** -0.5)
    lane = lax.broadcasted_iota(jnp.int32, q.shape, 1)
    q1 = jnp.where(lane < DIFF_DH, q, 0.0)
    q2 = jnp.where(lane >= DIFF_DH, q, 0.0)
    return jnp.concatenate([q1, q2], axis=0).astype(BF16)


def _diff_attn_kernel(qi_ref, kj_ref, q_ref, k_ref, v_ref, slope_ref, lq_ref, subln_ref, o_ref, ko_ref, vo_ref,
                      qq_ref, m_ref, l_ref, acc_ref, *, tq, lam_init):
    t = pl.program_id(2)
    i = qi_ref[t]
    j = kj_ref[t]

    @pl.when(j == 0)
    def _():
        qq_ref[...] = _stack_maps(q_ref[0])
        m_ref[...] = jnp.full(m_ref.shape, NEG, F32)
        l_ref[...] = jnp.zeros(l_ref.shape, F32)
        acc_ref[...] = jnp.zeros(acc_ref.shape, F32)

    kb = k_ref[0].astype(BF16)
    vb = v_ref[0].astype(BF16)
    krow = lax.broadcasted_iota(jnp.int32, (tq, LANES), 0)
    kbias = slope_ref[0][:, :1] * ((j - i) * tq + krow).astype(F32)
    kbias = jnp.tile(kbias, (1, 2 * tq // LANES))

    def accumulate(diagonal):
        s = _dot(kb, qq_ref[...], NT) + kbias
        if diagonal:
            qry = lax.broadcasted_iota(jnp.int32, s.shape, 1)
            qry = jnp.where(qry >= tq, qry - tq, qry)
            s = jnp.where(lax.broadcasted_iota(jnp.int32, s.shape, 0) <= qry, s, NEG)
        m_prev = m_ref[...]
        m_new = jnp.maximum(m_prev, jnp.max(s, axis=0, keepdims=True))
        alpha = jnp.exp(m_prev - m_new)
        p = jnp.exp(s - m_new)
        l_ref[...] = alpha * l_ref[...] + jnp.sum(p, axis=0, keepdims=True)
        acc_ref[...] = alpha * acc_ref[...] + _dot(vb, p.astype(BF16), TN)
        m_ref[...] = m_new

    @pl.when(j < i)
    def _():
        accumulate(False)

    @pl.when(j == i)
    def _():
        accumulate(True)
        o12 = acc_ref[...] / l_ref[...]
        o = o12[:, :tq] - _diff_lambda(lq_ref[...], lam_init) * o12[:, tq:]
        o = o * lax.rsqrt(jnp.mean(o * o, axis=0, keepdims=True) + 1e-5) * subln_ref[...]
        o_ref[0] = (o * (1.0 - lam_init)).T
        ko_ref[0] = k_ref[0]
        vo_ref[0] = v_ref[0]


def _diff_attn(p3, slopes, lambda_qk, subln_col, *, tq, lam_init):
    b, t, _ = p3.shape
    nq = t // tq
    pairs = [(i, j) for i in range(nq) for j in range(i + 1)]
    qi = jnp.asarray(np.array([p[0] for p in pairs], np.int32))
    kj = jnp.asarray(np.array([p[1] for p in pairs], np.int32))
    qb, kb, vb = OFF_DQ // DIFF_DV, OFF_DK // DIFF_DV, OFF_DV // DIFF_DV
    grid_spec = pltpu.PrefetchScalarGridSpec(
        num_scalar_prefetch=2,
        grid=(b, DIFF_H, len(pairs)),
        in_specs=[pl.BlockSpec((1, tq, DIFF_DV), lambda bb, h, s, qi, kj: (bb, qi[s], qb + h)),
                  pl.BlockSpec((1, tq, DIFF_DV), lambda bb, h, s, qi, kj: (bb, kj[s], kb + h)),
                  pl.BlockSpec((1, tq, DIFF_DV), lambda bb, h, s, qi, kj: (bb, kj[s], vb + h)),
                  pl.BlockSpec((1, 1, LANES), lambda bb, h, s, qi, kj: (h, 0, 0)),
                  pl.BlockSpec((4, DIFF_DH), lambda bb, h, s, qi, kj: (0, 0)),
                  pl.BlockSpec((DIFF_DV, 1), lambda bb, h, s, qi, kj: (0, 0))],
        out_specs=[pl.BlockSpec((1, tq, DIFF_DV), lambda bb, h, s, qi, kj: (bb, qi[s], h))] * 3,
        scratch_shapes=[pltpu.VMEM((2 * tq, DIFF_DV), BF16), pltpu.VMEM((1, 2 * tq), F32),
                        pltpu.VMEM((1, 2 * tq), F32), pltpu.VMEM((DIFF_DV, 2 * tq), F32)],
    )
    return pl.pallas_call(
        functools.partial(_diff_attn_kernel, tq=tq, lam_init=lam_init),
        grid_spec=grid_spec,
        out_shape=[jax.ShapeDtypeStruct((b, t, DIFF_H * DIFF_DV), F32)] * 3,
        compiler_params=_params(("arbitrary", "arbitrary", "arbitrary"), 40),
        name="diff_attn",
    )(qi, kj, p3, p3, p3, slopes, lambda_qk, subln_col)


def _paged_diff_attn_kernel(pt_ref, q_ref, *refs, n_pages, page, n_new, past_len, lam_init):
    k_refs = refs[:n_pages]
    v_refs = refs[n_pages:2 * n_pages]
    kn_ref, vn_ref, slope_ref, lq_ref, subln_ref, o_ref, m_ref, l_ref, acc_ref = refs[2 * n_pages:]
    s_id = pl.program_id(1)
    rows = 2 * DIFF_H * n_new

    @pl.when(s_id == 0)
    def _():
        m_ref[...] = jnp.full(m_ref.shape, NEG, F32)
        l_ref[...] = jnp.zeros(l_ref.shape, F32)
        acc_ref[...] = jnp.zeros(acc_ref.shape, F32)

    qb = (q_ref[0] * (DIFF_DH ** -0.5)).astype(BF16)
    slope = slope_ref[...][:, :1]
    log_new, log_h = n_new.bit_length() - 1, DIFF_H.bit_length() - 1

    def coords(ncol):
        row = lax.broadcasted_iota(jnp.int32, (rows, ncol), 0)
        col = lax.broadcasted_iota(jnp.int32, (rows, ncol), 1)
        row_h = lax.shift_right_logical(row, log_new) & (DIFF_H - 1)
        return row & (n_new - 1), (col & (DIFF_H - 1)) == row_h, lax.shift_right_logical(col, log_h)

    def update(ss, vbs, carry):
        m_prev, l_prev, acc_prev = carry
        m_new = m_prev
        for s in ss:
            m_new = jnp.maximum(m_new, jnp.max(s, axis=-1, keepdims=True))
        alpha = jnp.exp(m_prev - m_new)
        l_new = alpha * l_prev
        acc_new = alpha * acc_prev
        for s, vb in zip(ss, vbs):
            p = jnp.exp(s - m_new)
            l_new = l_new + jnp.sum(p, axis=-1, keepdims=True)
            acc_new = acc_new + _dot(p.astype(BF16), vb)
        return m_new, l_new, acc_new

    _, same_head, tok = coords(page * DIFF_H)
    ss, vbs = [], []
    for i in range(n_pages):
        kb = k_refs[i][0].astype(BF16)
        vbs.append(v_refs[i][0].astype(BF16))
        k_pos = ((s_id * n_pages + i) * page - past_len + tok).astype(F32)
        ss.append(jnp.where(same_head, _dot(qb, kb, NT) + slope * k_pos, NEG))
    m_ref[...], l_ref[...], acc_ref[...] = update(ss, vbs, (m_ref[...], l_ref[...], acc_ref[...]))

    @pl.when(s_id == pl.num_programs(1) - 1)
    def _():
        kb = kn_ref[0].astype(BF16)
        vb = vn_ref[0].astype(BF16)
        row_tn, same_head_n, tok_n = coords(n_new * DIFF_H)
        k_pos = tok_n.astype(F32)
        s = jnp.where(same_head_n & (tok_n <= row_tn), _dot(qb, kb, NT) + slope * k_pos, NEG)
        _, l, acc = update([s], [vb], (m_ref[...], l_ref[...], acc_ref[...]))
        o_ref[0] = _diff_finish(acc, l, lq_ref[...], subln_ref[...], rows // 2, lam_init)


def _paged_diff_attn(qm, cache_k, cache_v, k_new, v_new, page_table, slope_rows, lambda_qk, subln, *,
                     n_pages, lam_init):
    db, rows, _ = qm.shape
    n_pool, page = cache_k.shape[0], cache_k.shape[1] // DIFF_H
    n_tab = page_table.shape[1]
    n_new = rows // (2 * DIFF_H)
    assert n_tab % n_pages == 0 and n_new & (n_new - 1) == 0 and DIFF_H & (DIFF_H - 1) == 0

    def page_spec(i):
        return pl.BlockSpec((1, page * DIFF_H, DIFF_DV), lambda b, s, pt: (pt[b, s * n_pages + i], 0, 0))

    grid_spec = pltpu.PrefetchScalarGridSpec(
        num_scalar_prefetch=1,
        grid=(db, n_tab // n_pages),
        in_specs=([pl.BlockSpec((1, rows, DIFF_DV), lambda b, s, pt: (b, 0, 0))]
                  + [page_spec(i) for i in range(n_pages)] * 2
                  + [pl.BlockSpec((1, n_new * DIFF_H, DIFF_DV), lambda b, s, pt: (b, 0, 0)),
                     pl.BlockSpec((1, n_new * DIFF_H, DIFF_DV), lambda b, s, pt: (b, 0, 0)),
                     pl.BlockSpec((rows, LANES), lambda b, s, pt: (0, 0)),
                     pl.BlockSpec((4, DIFF_DH), lambda b, s, pt: (0, 0)),
                     pl.BlockSpec((1, DIFF_DV), lambda b, s, pt: (0, 0))]),
        out_specs=pl.BlockSpec((1, rows // 2, DIFF_DV), lambda b, s, pt: (b, 0, 0)),
        scratch_shapes=[pltpu.VMEM((rows, 1), F32), pltpu.VMEM((rows, 1), F32), pltpu.VMEM((rows, DIFF_DV), F32)],
    )
    return pl.pallas_call(
        functools.partial(_paged_diff_attn_kernel, n_pages=n_pages, page=page, n_new=n_new,
                          past_len=n_tab * page, lam_init=lam_init),
        grid_spec=grid_spec,
        out_shape=jax.ShapeDtypeStruct((db, rows // 2, DIFF_DV), F32),
        compiler_params=_params(("arbitrary", "arbitrary"), 32),
        name="paged_diff_attn",
    )(page_table, qm, *([cache_k] * n_pages), *([cache_v] * n_pages), k_new, v_new, slope_rows, lambda_qk, subln)


def _cross_attn_kernel(q_ref, k_ref, v_ref, o_ref, *, scale):
    q = q_ref[0].astype(BF16)
    k = k_ref[0].astype(BF16)
    v = v_ref[0].astype(BF16)
    s = _dot(q, k, NT) * scale
    p = jnp.exp(s - jnp.max(s, axis=-1, keepdims=True))
    l = jnp.sum(p, axis=-1, keepdims=True)
    o_ref[0] = _dot(p.astype(BF16), v) / l


def _cross_attn(q_arr, q_off, k_arr, k_off, v_arr, v_off, *, tq):
    b, t, _ = q_arr.shape
    n_mem = k_arr.shape[1]
    qb, kb, vb = q_off // CROSS_DH, k_off // CROSS_DH, v_off // CROSS_DH
    return pl.pallas_call(
        functools.partial(_cross_attn_kernel, scale=CROSS_DH ** -0.5),
        grid=(b, CROSS_H, t // tq),
        in_specs=[pl.BlockSpec((1, tq, CROSS_DH), lambda bb, h, i: (bb, i, qb + h)),
                  pl.BlockSpec((1, n_mem, CROSS_DH), lambda bb, h, i: (bb, 0, kb + h)),
                  pl.BlockSpec((1, n_mem, CROSS_DH), lambda bb, h, i: (bb, 0, vb + h))],
        out_specs=pl.BlockSpec((1, tq, CROSS_DH), lambda bb, h, i: (bb, i, h)),
        out_shape=jax.ShapeDtypeStruct((b, t, CROSS_H * CROSS_DH), F32),
        compiler_params=_params(("arbitrary", "arbitrary", "arbitrary"), 32),
        name="cross_attn",
    )(q_arr, k_arr, v_arr)


def _rwkv_prep_kernel(r_ref, k_ref, v_ref, lo_ref, fr_ref, fk_ref, fv_ref, fl_ref,
                      mur_ref, muk_ref, muv_ref, mul_ref, w0_ref, w2_ref, a0_ref, a2_ref,
                      kk_ref, ka_ref, rk_ref, e1_ref, e2_ref,
                      ro_ref, lwo_ref, ko_ref, vo_ref, ao_ref, bo_ref, bonus_ref,
                      cr_ref, ck_ref, cv_ref, cl_ref):
    @pl.when(pl.program_id(1) == 0)
    def _():
        cr_ref[...] = fr_ref[0]
        ck_ref[...] = fk_ref[0]
        cv_ref[...] = fv_ref[0]
        cl_ref[...] = fl_ref[0]

    def shift_mix(x_ref, carry_ref, mu_ref):
        x = x_ref[0]
        rows = x.shape[0]
        prev = pltpu.roll(x, 1, axis=0)
        rid = lax.broadcasted_iota(jnp.int32, x.shape, 0)
        prev = jnp.where(rid == 0, carry_ref[...], prev)
        carry_ref[...] = x[rows - 1:rows, :]
        return x + mu_ref[...] * (prev - x)

    r = shift_mix(r_ref, cr_ref, mur_ref)
    k = shift_mix(k_ref, ck_ref, muk_ref)
    v = shift_mix(v_ref, cv_ref, muv_ref)
    lo = shift_mix(lo_ref, cl_ref, mul_ref)

    xw = -(w0_ref[...] + _dot3(jnp.tanh(lo), w2_ref[...]))
    softplus = jnp.maximum(xw, 0.0) + jnp.log(1.0 + jnp.exp(-jnp.abs(xw)))
    w = -softplus - 0.5
    lwo_ref[0] = -jnp.exp(w)
    a = jax.nn.sigmoid(a0_ref[...] + _dot3(lo, a2_ref[...]))

    def head_sum(x):
        return _dot_sel(_dot_sel(x, e1_ref[...]), e2_ref[...])

    kkr = k * kk_ref[...]
    kk = kkr / jnp.maximum(jnp.sqrt(head_sum(kkr * kkr)), 1e-12)
    k2 = k * (1.0 + (a - 1.0) * ka_ref[...])
    ro_ref[0] = r
    ko_ref[0] = k2
    vo_ref[0] = v
    ao_ref[0] = -kk
    bo_ref[0] = kk * a
    bonus_ref[0] = head_sum(r * k2 * rk_ref[...]) * v


def _rwkv_prep(p3, firsts, mus, w0, w2p, a0, a2p, k_k, k_a, r_k, e1, e2, *, tm):
    b, t, _ = p3.shape
    nt = t // tm
    wide = lambda blk: pl.BlockSpec((1, tm, RWKV_W), lambda bb, i: (bb, i, blk))
    vec = lambda n: pl.BlockSpec((1, n), lambda bb, i: (0, 0))
    first = lambda n: pl.BlockSpec((1, 1, n), lambda bb, i: (bb, 0, 0))
    out_wide = pl.BlockSpec((1, tm, RWKV_W), lambda bb, i: (bb, i, 0))
    in_specs = [wide(OFF_R // RWKV_W), wide(OFF_K // RWKV_W), wide(OFF_V // RWKV_W),
                pl.BlockSpec((1, tm, LANES), lambda bb, i: (bb, i, OFF_LORA // LANES)),
                first(RWKV_W), first(RWKV_W), first(RWKV_W), first(LANES),
                vec(RWKV_W), vec(RWKV_W), vec(RWKV_W), vec(LANES),
                vec(RWKV_W), pl.BlockSpec((LANES, RWKV_W), lambda bb, i: (0, 0)),
                vec(RWKV_W), pl.BlockSpec((LANES, RWKV_W), lambda bb, i: (0, 0)),
                vec(RWKV_W), vec(RWKV_W), vec(RWKV_W),
                pl.BlockSpec((RWKV_W, LANES), lambda bb, i: (0, 0)),
                pl.BlockSpec((LANES, RWKV_W), lambda bb, i: (0, 0))]
    shape = jax.ShapeDtypeStruct((b, t, RWKV_W), F32)
    return pl.pallas_call(
        _rwkv_prep_kernel,
        grid=(b, nt),
        in_specs=in_specs,
        out_specs=[out_wide] * 7,
        out_shape=[shape] * 7,
        scratch_shapes=[pltpu.VMEM((1, RWKV_W), F32), pltpu.VMEM((1, RWKV_W), F32),
                        pltpu.VMEM((1, RWKV_W), F32), pltpu.VMEM((1, LANES), F32)],
        compiler_params=_params(("arbitrary", "arbitrary"), 48),
        name="rwkv_prep",
    )(p3, p3, p3, p3, *firsts, *mus, w0, w2p, a0, a2p, k_k, k_a, r_k, e1, e2)


def _rwkv_scan_kernel(r_ref, lw_ref, k_ref, v_ref, a_ref, b_ref, bonus_ref, s0_ref, g_ref, bb_ref,
                      y_ref, sout_ref, s_ref, *, chunk, n_pairs):
    c = pl.program_id(1)
    c2 = 2 * chunk

    @pl.when(c == 0)
    def _():
        s_ref[...] = s0_ref[0]

    rid = lax.broadcasted_iota(jnp.int32, (c2, c2), 0)
    cid = lax.broadcasted_iota(jnp.int32, (c2, c2), 1)
    strict = rid > cid
    incl2 = jnp.concatenate([rid >= cid, rid >= cid], axis=1)
    tr = lax.broadcasted_iota(jnp.int32, (chunk, chunk), 0)
    tc = lax.broadcasted_iota(jnp.int32, (chunk, chunk), 1)
    tri = (tr >= tc).astype(BF16)
    srow = lax.broadcasted_iota(jnp.int32, (c2, LANES), 0)
    slane = lax.broadcasted_iota(jnp.int32, (c2, LANES), 1)
    own = (srow < chunk) == (slane < RWKV_N)

    def stack(x):
        return jnp.where(own, jnp.concatenate([x, x], axis=0), 0.0)

    def cat2(x, y, axis):
        return tuple(jnp.concatenate([p, q], axis=axis) for p, q in zip(x, y))

    pairs = range(n_pairs)
    sls = [pl.ds(p * LANES, LANES) for p in pairs]
    lhs, rhs, vm, g_end, s_prev = [], [], [], [], []
    for sl in sls:
        lw = lw_ref[0, :, sl]
        h1 = lw.astype(BF16)
        r1 = lw - h1.astype(F32)
        h2 = r1.astype(BF16)
        h3 = (r1 - h2.astype(F32)).astype(BF16)
        cum = _dot(tri, h1) + (_dot(tri, h2) + _dot(tri, h3))
        g = jnp.exp(cum)
        gi = jnp.exp(-cum)
        gp = jnp.exp(cum - lw)
        am = stack(a_ref[0, :, sl] * gp)
        rm = stack(r_ref[0, :, sl] * g)
        bm = stack(b_ref[0, :, sl] * gi)
        km = stack(k_ref[0, :, sl] * gi)
        lhs.append(_split2(jnp.concatenate([am, rm], axis=0)))
        rhs.append(_split2(jnp.concatenate([bm, km], axis=0)))
        vm.append(_split2(stack(v_ref[0, :, sl])))
        g_end.append(g[chunk - 1:chunk, :])
    for p in pairs:
        s_prev.append(s_ref[p])

    ps = SCAN_PASSES
    p1 = [_dot3s(lhs[p], rhs[p], NT, ps['intra']) for p in pairs]
    p2 = [_dot3s(lhs[p], _split2(s_prev[p]), NT, ps['state_read']) for p in pairs]
    nm = [_split2(jnp.where(strict, p1[p][:c2, :c2], 0.0)) for p in pairs]
    aak = [_split2(jnp.where(strict, p1[p][:c2, c2:], 0.0)) for p in pairs]
    m2 = [_split2(jnp.where(incl2, p1[p][c2:, :], 0.0)) for p in pairs]
    u = [p2[p][:c2] + _dot3s(aak[p], vm[p], NN, ps['av']) for p in pairs]
    n = 1
    while 2 * n < chunk:
        us = [_split2(u[p]) for p in pairs]
        y = [_dot3s(nm[p], cat2(nm[p], us[p], 1), NN, ps['solve']) for p in pairs]
        u = [u[p] + y[p][:, c2:] for p in pairs]
        nm = [_split2(y[p][:, :c2]) for p in pairs]
        n *= 2
    us = [_split2(u[p]) for p in pairs]
    u = [u[p] + _dot3s(nm[p], us[p], NN, ps['solve']) for p in pairs]
    uv = [cat2(_split2(u[p]), vm[p], 0) for p in pairs]
    o = [p2[p][c2:] + _dot3s(m2[p], uv[p], NN, ps['out']) for p in pairs]
    for p in pairs:
        s_ref[p] = (s_prev[p] + _dot3s(uv[p], rhs[p], TN, ps['update'])) * g_end[p]

    for p, sl in zip(pairs, sls):
        mu = jnp.sum(o[p], axis=-1, keepdims=True) * (1.0 / RWKV_N)
        d = jnp.where(own, o[p] - mu, 0.0)
        var = jnp.sum(d * d, axis=-1, keepdims=True) * (1.0 / RWKV_N)
        yn = d * lax.rsqrt(var + RWKV_GN_EPS)
        y = yn[:chunk] + yn[chunk:]
        y_ref[0, :, sl] = y * g_ref[:, sl] + bb_ref[:, sl] + bonus_ref[0, :, sl]

    @pl.when(c == pl.num_programs(1) - 1)
    def _():
        sout_ref[0] = s_ref[...]


def _rwkv_scan(r, lw, k, v, a, b, bonus, s0_pairs, ln_g, ln_b):
    bsz, t, _ = r.shape
    n_pairs = RWKV_H // 2
    blk = pl.BlockSpec((1, CHUNK, RWKV_W), lambda bb, c: (bb, c, 0))
    st = pl.BlockSpec((1, n_pairs, LANES, LANES), lambda bb, c: (bb, 0, 0, 0))
    vec = pl.BlockSpec((1, RWKV_W), lambda bb, c: (0, 0))
    return pl.pallas_call(
        functools.partial(_rwkv_scan_kernel, chunk=CHUNK, n_pairs=n_pairs),
        grid=(bsz, t // CHUNK),
        in_specs=[blk] * 7 + [st, vec, vec],
        out_specs=[blk, st],
        out_shape=[jax.ShapeDtypeStruct((bsz, t, RWKV_W), F32),
                   jax.ShapeDtypeStruct((bsz, n_pairs, LANES, LANES), F32)],
        scratch_shapes=[pltpu.VMEM((n_pairs, LANES, LANES), F32)],
        compiler_params=_params(("arbitrary", "arbitrary"), 32),
        name="rwkv_scan",
    )(r, lw, k, v, a, b, bonus, s0_pairs, ln_g, ln_b)


def _pair_states(s):
    bsz = s.shape[0]
    s = s.reshape(bsz, RWKV_H // 2, 2, RWKV_N, RWKV_N)
    z = jnp.zeros_like(s[:, :, 0])
    top = jnp.concatenate([s[:, :, 0], z], axis=-1)
    bot = jnp.concatenate([z, s[:, :, 1]], axis=-1)
    return jnp.concatenate([top, bot], axis=-2)


def _unpair_states(sp):
    bsz = sp.shape[0]
    even = sp[:, :, :RWKV_N, :RWKV_N]
    odd = sp[:, :, RWKV_N:, RWKV_N:]
    return jnp.stack([even, odd], axis=2).reshape(bsz, RWKV_H, RWKV_N, RWKV_N)


def _merge_out_kernel(x_ref, d_ref, r_ref, c_ref, gate_ref, w_ref, g_ref, o_ref, *, eps):
    gate = gate_ref[...]
    sg = gate * jax.nn.sigmoid(gate)
    nd, nr = d_ref.shape[1], r_ref.shape[1]
    acc = _dot((d_ref[...] * sg[:, :nd]).astype(BF16), w_ref[0:nd, :])
    acc += _dot((r_ref[...] * sg[:, nd:nd + nr]).astype(BF16), w_ref[nd:nd + nr, :])
    acc += _dot((c_ref[...] * sg[:, nd + nr:]).astype(BF16), w_ref[nd + nr:, :])
    x = x_ref[...] + acc
    o_ref[...] = x * lax.rsqrt(jnp.mean(x * x, axis=-1, keepdims=True) + eps) * g_ref[...]


def _merge_out(x, d_o, r_o, c_o, p2, w_out, norm_out, *, tm):
    m, d = x.shape
    row = lambda n: pl.BlockSpec((tm, n), lambda i: (i, 0))
    return pl.pallas_call(
        functools.partial(_merge_out_kernel, eps=1e-6),
        grid=(m // tm,),
        in_specs=[row(d), row(d_o.shape[1]), row(r_o.shape[1]), row(c_o.shape[1]),
                  pl.BlockSpec((tm, d), lambda i: (i, OFF_GATE // d)),
                  pl.BlockSpec((d, d), lambda i: (0, 0)),
                  pl.BlockSpec((1, d), lambda i: (0, 0))],
        out_specs=row(d),
        out_shape=jax.ShapeDtypeStruct((m, d), F32),
        compiler_params=_params(("arbitrary",), 48),
        name="merge_out",
    )(x, d_o, r_o, c_o, p2, w_out, norm_out)


def _reorder_in_cols(w):
    dq, z, cq, gate = 0, 1536, 1536 + 3200, 1536 + 3200 + 512
    parts = [w[..., gate:], w[..., z:z + 3 * RWKV_W], w[..., dq:z], w[..., cq:gate], w[..., z + 3 * RWKV_W:cq]]
    return jnp.concatenate([p.astype(BF16) for p in parts], axis=-1)


def kernel(x_prompt, x_sample, cache_k, cache_v, cache_mem_k, cache_mem_v, state_rwkv, state_shift, page_table,
           mem_prompt, norm_in, w_in, norm_mem, w_mem_kv, lambda_qk, diff_subln, rwkv_mu, rwkv_w0, rwkv_w2, rwkv_a0,
           rwkv_a2, rwkv_k_k, rwkv_k_a, rwkv_r_k, rwkv_ln_g, rwkv_ln_b, w_out, norm_out):
    bsz, seq, d = x_prompt.shape
    db, ds, _ = x_sample.shape
    depth = w_in.shape[0]
    assert depth == 1
    l = 0
    lam_init = 0.8 - 0.6 * math.exp(-0.3 * l)
    n_mem = mem_prompt.shape[1]

    w_in_b = _reorder_in_cols(w_in[l])
    w_shift_b = jnp.concatenate([w_in_b[:, OFF_R:OFF_DQ], w_in_b[:, OFF_LORA:]], axis=1)
    w_mem_b = w_mem_kv[l].astype(BF16)
    w_out_b = w_out[l].astype(BF16)
    g_in = norm_in[l][None]
    g_mem = norm_mem[l][None]
    g_out = norm_out[None]
    mu = rwkv_mu[l]
    mus = (mu[None, :RWKV_W], mu[None, RWKV_W:2 * RWKV_W], mu[None, 2 * RWKV_W:3 * RWKV_W], mu[None, 3 * RWKV_W:])
    zl = jnp.zeros((LORA, RWKV_W), F32)
    w2p = jnp.concatenate([rwkv_w2[l], zl], axis=0)
    a2p = jnp.concatenate([zl, rwkv_a2[l]], axis=0)
    vec = lambda x: x.reshape(1, RWKV_W)
    w0, a0, k_k, k_a, r_k = vec(rwkv_w0[l]), vec(rwkv_a0[l]), vec(rwkv_k_k[l]), vec(rwkv_k_a[l]), vec(rwkv_r_k[l])
    ln_g, ln_b = vec(rwkv_ln_g[l]), vec(rwkv_ln_b[l])
    head_of = np.arange(RWKV_W) // RWKV_N
    e1 = jnp.asarray((head_of[:, None] == np.arange(LANES)[None, :]).astype(np.float32), BF16)
    e2 = jnp.asarray((np.arange(LANES)[:, None] == head_of[None, :]).astype(np.float32), BF16)
    slopes = 2.0 ** (-8.0 * np.arange(1, DIFF_H + 1, dtype=np.float64) / DIFF_H)
    slopes_h = jnp.asarray(np.broadcast_to(slopes[:, None, None], (DIFF_H, 1, LANES)).astype(np.float32))
    subln = diff_subln[l][None]
    lq = lambda_qk[l]

    xp2 = x_prompt.reshape(bsz * seq, d)
    p2 = _norm_matmul(xp2, g_in, w_in_b, normalize=True, tm=512, tn=IN_COLS // 3, vmem_mib=56)
    p3 = p2.reshape(bsz, seq, IN_COLS)
    hs_p = _rmsnorm_rows(x_prompt[:, -1], g_in)
    d_o, k_p, v_p = _diff_attn(p3, slopes_h, lq, subln.reshape(DIFF_DV, 1), tq=512, lam_init=lam_init)
    zeros_first = (jnp.zeros((bsz, 1, RWKV_W), F32),) * 3 + (jnp.zeros((bsz, 1, LANES), F32),)
    r, lw, k2, v, a, b, bonus = _rwkv_prep(p3, zeros_first, mus, w0, w2p, a0, a2p, k_k, k_a, r_k, e1, e2, tm=256)
    s0_p = jnp.zeros((bsz, RWKV_H // 2, LANES, LANES), F32)
    r_o, sp_p = _rwkv_scan(r, lw, k2, v, a, b, bonus, s0_p, ln_g, ln_b)
    mem_kv = _norm_matmul(mem_prompt.reshape(bsz * n_mem, d), g_mem, w_mem_b, normalize=True, tm=256, tn=512,
                          vmem_mib=32).reshape(bsz, n_mem, 2 * CROSS_H * CROSS_DH)
    c_o = _cross_attn(p3, OFF_CQ, mem_kv, 0, mem_kv, CROSS_H * CROSS_DH, tq=1024)
    y_p = _merge_out(xp2, d_o.reshape(bsz * seq, -1), r_o.reshape(bsz * seq, -1), c_o.reshape(bsz * seq, -1), p2,
                     w_out_b, g_out, tm=256).reshape(bsz, seq, d)
    nk_p = k_p.reshape(1, bsz, seq, DIFF_H, DIFF_DV)
    nv_p = v_p.reshape(1, bsz, seq, DIFF_H, DIFF_DV)
    nmk_p = mem_kv[:, :, :CROSS_H * CROSS_DH].reshape(1, bsz, n_mem, CROSS_H, CROSS_DH)
    nmv_p = mem_kv[:, :, CROSS_H * CROSS_DH:].reshape(1, bsz, n_mem, CROSS_H, CROSS_DH)
    ns_p = _unpair_states(sp_p)[None]

    xs2 = x_sample.reshape(db * ds, d)
    ps2 = _norm_matmul(xs2, g_in, w_in_b, normalize=True, tm=db * ds, tn=IN_COLS // 3, vmem_mib=56)
    ps3 = ps2.reshape(db, ds, IN_COLS)
    hs_s = _rmsnorm_rows(x_sample[:, -1], g_in)
    shift_rows = _pad_rows(state_shift[l], 16)
    z_first = _norm_matmul(shift_rows, g_in, w_shift_b, normalize=False, tm=shift_rows.shape[0],
                           tn=w_shift_b.shape[1], vmem_mib=56)[:db]
    firsts = (z_first[:, None, :RWKV_W], z_first[:, None, RWKV_W:2 * RWKV_W],
              z_first[:, None, 2 * RWKV_W:3 * RWKV_W], z_first[:, None, 3 * RWKV_W:])

    q5 = ps3[:, :, OFF_DQ:OFF_DK].reshape(db, ds, DIFF_H, 2, DIFF_DH)
    zq = jnp.zeros_like(q5[..., 0, :])
    qm = jnp.stack([jnp.concatenate([q5[..., 0, :], zq], -1), jnp.concatenate([zq, q5[..., 1, :]], -1)], axis=1)
    qm = qm.transpose(0, 1, 3, 2, 4).reshape(db, 2 * DIFF_H * ds, DIFF_DV)
    k_new = ps3[:, :, OFF_DK:OFF_DV].reshape(db, ds * DIFF_H, DIFF_DV)
    v_new = ps3[:, :, OFF_DV:OFF_CQ].reshape(db, ds * DIFF_H, DIFF_DV)
    row_head = (np.arange(2 * DIFF_H * ds) // ds) % DIFF_H
    slope_rows = jnp.asarray(np.broadcast_to(slopes[row_head][:, None], (2 * DIFF_H * ds, LANES)).astype(np.float32))
    n_pool, page = cache_k.shape[1], cache_k.shape[2]
    ck = cache_k[l].reshape(n_pool, page * DIFF_H, DIFF_DV)
    cv = cache_v[l].reshape(n_pool, page * DIFF_H, DIFF_DV)
    ds_o = _paged_diff_attn(qm, ck, cv, k_new, v_new, page_table, slope_rows, lq, subln, n_pages=8,
                            lam_init=lam_init)
    ds_o = ds_o.reshape(db, DIFF_H, ds, DIFF_DV).transpose(0, 2, 1, 3).reshape(db * ds, DIFF_H * DIFF_DV)

    pad_t = 8
    ps3_pad = jnp.pad(ps3, ((0, 0), (0, pad_t - ds), (0, 0)))
    outs = _rwkv_prep(ps3_pad, firsts, mus, w0, w2p, a0, a2p, k_k, k_a, r_k, e1, e2, tm=pad_t)
    outs = [jnp.pad(o[:, :ds], ((0, 0), (0, CHUNK - ds), (0, 0))) for o in outs]
    rs_o, sp_s = _rwkv_scan(*outs, _pair_states(state_rwkv[l]), ln_g, ln_b)
    rs_o = rs_o[:, :ds].reshape(db * ds, RWKV_W)

    cq_pad = jnp.pad(ps3[:, :, OFF_CQ:OFF_LORA], ((0, 0), (0, 16 - ds), (0, 0)))
    mk_s = cache_mem_k[l].reshape(db, n_mem, CROSS_H * CROSS_DH)
    mv_s = cache_mem_v[l].reshape(db, n_mem, CROSS_H * CROSS_DH)
    cs_o = _cross_attn(cq_pad, 0, mk_s, 0, mv_s, 0, tq=16)[:, :ds].reshape(db * ds, CROSS_H * CROSS_DH)
    y_s = _merge_out(xs2, ds_o, rs_o, cs_o, ps2, w_out_b, g_out, tm=db * ds).reshape(db, ds, d)
    nk_s = ps3[:, :, OFF_DK:OFF_DV].reshape(1, db, ds, DIFF_H, DIFF_DV)
    nv_s = ps3[:, :, OFF_DV:OFF_CQ].reshape(1, db, ds, DIFF_H, DIFF_DV)
    ns_s = _unpair_states(sp_s)[None]

    return (y_p, y_s, nk_p, nv_p, nmk_p, nmv_p, ns_p, hs_p[None], nk_s, nv_s, ns_s, hs_s[None])
```

```python
import functools
import math

import jax
import jax.numpy as jnp
import numpy as np
from jax import lax
from jax.experimental import pallas as pl
from jax.experimental.pallas import tpu as pltpu

F32 = jnp.float32
BF16 = jnp.bfloat16

LANES = 128
MIB = 1024 * 1024

DIFF_H = 4
DIFF_DV = 128
DIFF_DH = 64
RWKV_N = 64
RWKV_H = 16
RWKV_W = RWKV_H * RWKV_N
CROSS_H = 4
CROSS_DH = 128
LORA = 64
RWKV_GN_EPS = 64e-5
CHUNK = 64
NEG = -1e30

OFF_GATE = 0
OFF_R = 2048
OFF_K = 3072
OFF_V = 4096
OFF_DQ = 5120
OFF_DK = 5632
OFF_DV = 6144
OFF_CQ = 6656
OFF_LORA = 7168
IN_COLS = 7296


def _params(sem, vmem_mib):
    return pltpu.CompilerParams(dimension_semantics=sem, vmem_limit_bytes=vmem_mib * MIB)


def _split2(x):
    hi = x.astype(BF16)
    lo = (x - hi.astype(F32)).astype(BF16)
    return hi, lo


def _dot(a, b, dims=(((1,), (0,)), ((), ()))):
    return lax.dot_general(a, b, dims, preferred_element_type=F32)


NN = (((1,), (0,)), ((), ()))
NT = (((1,), (1,)), ((), ()))
TN = (((0,), (0,)), ((), ()))


def _dot3(a, b, dims=NN):
    ah, al = _split2(a)
    bh, bl = _split2(b)
    return _dot(ah, bh, dims) + (_dot(ah, bl, dims) + _dot(al, bh, dims))


def _dot3s(a, b, dims=NN, passes=3):
    ah, al = a
    bh, bl = b
    out = _dot(ah, bh, dims)
    if passes >= 2:
        out = out + _dot(al, bh, dims)
    if passes >= 3:
        out = out + _dot(ah, bl, dims)
    return out


SCAN_PASSES = dict(intra=1, state_read=1, av=1, solve=1, out=1, update=1)


def _dot_sel(x, sel):
    h, l = _split2(x)
    return _dot(h, sel) + _dot(l, sel)


def _norm_matmul_kernel(x_ref, g_ref, w_ref, o_ref, h_ref, *, normalize, eps):
    @pl.when(pl.program_id(1) == 0)
    def _():
        x = x_ref[...]
        if normalize:
            x = x * lax.rsqrt(jnp.mean(x * x, axis=-1, keepdims=True) + eps) * g_ref[...]
        h_ref[...] = x.astype(BF16)

    o_ref[...] = jnp.dot(h_ref[...], w_ref[...], preferred_element_type=F32)


def _norm_matmul(x, g, w, *, normalize, tm, tn, vmem_mib):
    m, d = x.shape
    n = w.shape[1]
    assert m % tm == 0 and n % tn == 0
    return pl.pallas_call(
        functools.partial(_norm_matmul_kernel, normalize=normalize, eps=1e-6),
        grid=(m // tm, n // tn),
        in_specs=[pl.BlockSpec((tm, d), lambda i, j: (i, 0)),
                  pl.BlockSpec((1, d), lambda i, j: (0, 0)),
                  pl.BlockSpec((d, tn), lambda i, j: (0, j))],
        out_specs=pl.BlockSpec((tm, tn), lambda i, j: (i, j)),
        out_shape=jax.ShapeDtypeStruct((m, n), F32),
        scratch_shapes=[pltpu.VMEM((tm, d), BF16)],
        compiler_params=_params(("arbitrary", "arbitrary"), vmem_mib),
        name="norm_matmul",
    )(x, g, w)


def _rmsnorm_kernel(x_ref, g_ref, o_ref, *, eps):
    x = x_ref[...]
    o_ref[...] = x * lax.rsqrt(jnp.mean(x * x, axis=-1, keepdims=True) + eps) * g_ref[...]


def _pad_rows(x, mult):
    pad = -x.shape[0] % mult
    return jnp.pad(x, ((0, pad), (0, 0))) if pad else x


def _rmsnorm_rows(x, g):
    n = x.shape[0]
    xp = _pad_rows(x, 8)
    return pl.pallas_call(
        functools.partial(_rmsnorm_kernel, eps=1e-6),
        out_shape=jax.ShapeDtypeStruct(xp.shape, F32),
        name="rmsnorm_rows",
    )(xp, g)[:n]


def _diff_lambda(lq, lam_init):
    t1 = jnp.sum(lq[0:1] * lq[1:2], axis=-1, keepdims=True)
    t2 = jnp.sum(lq[2:3] * lq[3:4], axis=-1, keepdims=True)
    return jnp.exp(t1) - jnp.exp(t2) + lam_init


def _diff_finish(acc, l, lq, subln, rows, lam_init):
    o12 = acc / l
    o = o12[:rows] - _diff_lambda(lq, lam_init) * o12[rows:]
    o = o * lax.rsqrt(jnp.mean(o * o, axis=-1, keepdims=True) + 1e-5) * subln
    return o * (1.0 - lam_init)


def _stack_maps(q):
    q = q * (DIFF_DH ** -0.5)
    lane = lax.broadcasted_iota(jnp.int32, q.shape, 1)
    q1 = jnp.where(lane < DIFF_DH, q, 0.0)
    q2 = jnp.where(lane >= DIFF_DH, q, 0.0)
    return jnp.concatenate([q1, q2], axis=0).astype(BF16)


def _diff_attn_kernel(qi_ref, kj_ref, q_ref, k_ref, v_ref, slope_ref, lq_ref, subln_ref, o_ref, ko_ref, vo_ref,
                      qq_ref, m_ref, l_ref, acc_ref, *, tq, qw, lam_init):
    t = pl.program_id(2)
    i = qi_ref[t]
    j = kj_ref[t]

    @pl.when(j == 0)
    def _():
        qq_ref[...] = _stack_maps(q_ref[0])
        m_ref[...] = jnp.full(m_ref.shape, NEG, F32)
        l_ref[...] = jnp.zeros(l_ref.shape, F32)
        acc_ref[...] = jnp.zeros(acc_ref.shape, F32)

    kb = k_ref[0].astype(BF16)
    vb = v_ref[0].astype(BF16)
    krow = lax.broadcasted_iota(jnp.int32, (tq, LANES), 0)
    kbias = slope_ref[0][:, :1] * ((j - i) * tq + krow).astype(F32)
    kbias = jnp.tile(kbias, (1, qw // LANES))

    def accumulate(diagonal):
        tiles = [pl.ds(c0, qw) for c0 in range(0, 2 * tq, qw)]
        scores = []
        for c0, cols in zip(range(0, 2 * tq, qw), tiles):
            s = _dot(kb, qq_ref[cols, :], NT) + kbias
            if diagonal:
                qry = c0 % tq + lax.broadcasted_iota(jnp.int32, s.shape, 1)
                s = jnp.where(lax.broadcasted_iota(jnp.int32, s.shape, 0) <= qry, s, NEG)
            scores.append(s)
        for s, cols in zip(scores, tiles):
            m_prev = m_ref[:, cols]
            m_new = jnp.maximum(m_prev, jnp.max(s, axis=0, keepdims=True))
            alpha = jnp.exp(m_prev - m_new)
            p = jnp.exp(s - m_new)
            l_ref[:, cols] = alpha * l_ref[:, cols] + jnp.sum(p, axis=0, keepdims=True)
            acc_ref[:, cols] = alpha * acc_ref[:, cols] + _dot(vb, p.astype(BF16), TN)
            m_ref[:, cols] = m_new

    @pl.when(j < i)
    def _():
        accumulate(False)

    @pl.when(j == i)
    def _():
        accumulate(True)
        o12 = acc_ref[...] / l_ref[...]
        o = o12[:, :tq] - _diff_lambda(lq_ref[...], lam_init) * o12[:, tq:]
        o = o * lax.rsqrt(jnp.mean(o * o, axis=0, keepdims=True) + 1e-5) * subln_ref[...]
        o_ref[0] = (o * (1.0 - lam_init)).T
        ko_ref[0] = k_ref[0]
        vo_ref[0] = v_ref[0]


def _diff_attn(p3, slopes, lambda_qk, subln_col, *, tq, qw, lam_init):
    b, t, _ = p3.shape
    nq = t // tq
    pairs = [(i, j) for i in range(nq) for j in range(i + 1)]
    qi = jnp.asarray(np.array([p[0] for p in pairs], np.int32))
    kj = jnp.asarray(np.array([p[1] for p in pairs], np.int32))
    qb, kb, vb = OFF_DQ // DIFF_DV, OFF_DK // DIFF_DV, OFF_DV // DIFF_DV
    grid_spec = pltpu.PrefetchScalarGridSpec(
        num_scalar_prefetch=2,
        grid=(b, DIFF_H, len(pairs)),
        in_specs=[pl.BlockSpec((1, tq, DIFF_DV), lambda bb, h, s, qi, kj: (bb, qi[s], qb + h)),
                  pl.BlockSpec((1, tq, DIFF_DV), lambda bb, h, s, qi, kj: (bb, kj[s], kb + h)),
                  pl.BlockSpec((1, tq, DIFF_DV), lambda bb, h, s, qi, kj: (bb, kj[s], vb + h)),
                  pl.BlockSpec((1, 1, LANES), lambda bb, h, s, qi, kj: (h, 0, 0)),
                  pl.BlockSpec((4, DIFF_DH), lambda bb, h, s, qi, kj: (0, 0)),
                  pl.BlockSpec((DIFF_DV, 1), lambda bb, h, s, qi, kj: (0, 0))],
        out_specs=[pl.BlockSpec((1, tq, DIFF_DV), lambda bb, h, s, qi, kj: (bb, qi[s], h))] * 3,
        scratch_shapes=[pltpu.VMEM((2 * tq, DIFF_DV), BF16), pltpu.VMEM((1, 2 * tq), F32),
                        pltpu.VMEM((1, 2 * tq), F32), pltpu.VMEM((DIFF_DV, 2 * tq), F32)],
    )
    return pl.pallas_call(
        functools.partial(_diff_attn_kernel, tq=tq, qw=qw, lam_init=lam_init),
        grid_spec=grid_spec,
        out_shape=[jax.ShapeDtypeStruct((b, t, DIFF_H * DIFF_DV), F32)] * 3,
        compiler_params=_params(("arbitrary", "arbitrary", "arbitrary"), 40),
        name="diff_attn",
    )(qi, kj, p3, p3, p3, slopes, lambda_qk, subln_col)


def _paged_diff_attn_kernel(pt_ref, q_ref, *refs, n_pages, page, n_new, past_len, lam_init):
    k_refs = refs[:n_pages]
    v_refs = refs[n_pages:2 * n_pages]
    kn_ref, vn_ref, slope_ref, lq_ref, subln_ref, o_ref, m_ref, l_ref, acc_ref = refs[2 * n_pages:]
    s_id = pl.program_id(1)
    rows = 2 * DIFF_H * n_new

    @pl.when(s_id == 0)
    def _():
        m_ref[...] = jnp.full(m_ref.shape, NEG, F32)
        l_ref[...] = jnp.zeros(l_ref.shape, F32)
        acc_ref[...] = jnp.zeros(acc_ref.shape, F32)

    qb = (q_ref[0] * (DIFF_DH ** -0.5)).astype(BF16)
    slope = slope_ref[...][:, :1]
    log_new, log_h = n_new.bit_length() - 1, DIFF_H.bit_length() - 1

    def coords(ncol):
        row = lax.broadcasted_iota(jnp.int32, (rows, ncol), 0)
        col = lax.broadcasted_iota(jnp.int32, (rows, ncol), 1)
        row_h = lax.shift_right_logical(row, log_new) & (DIFF_H - 1)
        return row & (n_new - 1), (col & (DIFF_H - 1)) == row_h, lax.shift_right_logical(col, log_h)

    def update(ss, vbs, carry):
        m_prev, l_prev, acc_prev = carry
        m_new = m_prev
        for s in ss:
            m_new = jnp.maximum(m_new, jnp.max(s, axis=-1, keepdims=True))
        alpha = jnp.exp(m_prev - m_new)
        l_new = alpha * l_prev
        acc_new = alpha * acc_prev
        for s, vb in zip(ss, vbs):
            p = jnp.exp(s - m_new)
            l_new = l_new + jnp.sum(p, axis=-1, keepdims=True)
            acc_new = acc_new + _dot(p.astype(BF16), vb)
        return m_new, l_new, acc_new

    _, same_head, tok = coords(page * DIFF_H)
    ss, vbs = [], []
    for i in range(n_pages):
        kb = k_refs[i][0].astype(BF16)
        vbs.append(v_refs[i][0].astype(BF16))
        k_pos = ((s_id * n_pages + i) * page - past_len + tok).astype(F32)
        ss.append(jnp.where(same_head, _dot(qb, kb, NT) + slope * k_pos, NEG))
    m_ref[...], l_ref[...], acc_ref[...] = update(ss, vbs, (m_ref[...], l_ref[...], acc_ref[...]))

    @pl.when(s_id == pl.num_programs(1) - 1)
    def _():
        kb = kn_ref[0].astype(BF16)
        vb = vn_ref[0].astype(BF16)
        row_tn, same_head_n, tok_n = coords(n_new * DIFF_H)
        k_pos = tok_n.astype(F32)
        s = jnp.where(same_head_n & (tok_n <= row_tn), _dot(qb, kb, NT) + slope * k_pos, NEG)
        _, l, acc = update([s], [vb], (m_ref[...], l_ref[...], acc_ref[...]))
        o_ref[0] = _diff_finish(acc, l, lq_ref[...], subln_ref[...], rows // 2, lam_init)


def _paged_diff_attn(qm, cache_k, cache_v, k_new, v_new, page_table, slope_rows, lambda_qk, subln, *,
                     n_pages, lam_init):
    db, rows, _ = qm.shape
    n_pool, page = cache_k.shape[0], cache_k.shape[1] // DIFF_H
    n_tab = page_table.shape[1]
    n_new = rows // (2 * DIFF_H)
    assert n_tab % n_pages == 0 and n_new & (n_new - 1) == 0 and DIFF_H & (DIFF_H - 1) == 0

    def page_spec(i):
        return pl.BlockSpec((1, page * DIFF_H, DIFF_DV), lambda b, s, pt: (pt[b, s * n_pages + i], 0, 0))

    grid_spec = pltpu.PrefetchScalarGridSpec(
        num_scalar_prefetch=1,
        grid=(db, n_tab // n_pages),
        in_specs=([pl.BlockSpec((1, rows, DIFF_DV), lambda b, s, pt: (b, 0, 0))]
                  + [page_spec(i) for i in range(n_pages)] * 2
                  + [pl.BlockSpec((1, n_new * DIFF_H, DIFF_DV), lambda b, s, pt: (b, 0, 0)),
                     pl.BlockSpec((1, n_new * DIFF_H, DIFF_DV), lambda b, s, pt: (b, 0, 0)),
                     pl.BlockSpec((rows, LANES), lambda b, s, pt: (0, 0)),
                     pl.BlockSpec((4, DIFF_DH), lambda b, s, pt: (0, 0)),
                     pl.BlockSpec((1, DIFF_DV), lambda b, s, pt: (0, 0))]),
        out_specs=pl.BlockSpec((1, rows // 2, DIFF_DV), lambda b, s, pt: (b, 0, 0)),
        scratch_shapes=[pltpu.VMEM((rows, 1), F32), pltpu.VMEM((rows, 1), F32), pltpu.VMEM((rows, DIFF_DV), F32)],
    )
    return pl.pallas_call(
        functools.partial(_paged_diff_attn_kernel, n_pages=n_pages, page=page, n_new=n_new,
                          past_len=n_tab * page, lam_init=lam_init),
        grid_spec=grid_spec,
        out_shape=jax.ShapeDtypeStruct((db, rows // 2, DIFF_DV), F32),
        compiler_params=_params(("arbitrary", "arbitrary"), 32),
        name="paged_diff_attn",
    )(page_table, qm, *([cache_k] * n_pages), *([cache_v] * n_pages), k_new, v_new, slope_rows, lambda_qk, subln)


def _cross_attn_kernel(q_ref, k_ref, v_ref, o_ref, *, scale):
    q = q_ref[0].astype(BF16)
    k = k_ref[0].astype(BF16)
    v = v_ref[0].astype(BF16)
    s = _dot(q, k, NT) * scale
    p = jnp.exp(s - jnp.max(s, axis=-1, keepdims=True))
    l = jnp.sum(p, axis=-1, keepdims=True)
    o_ref[0] = _dot(p.astype(BF16), v) / l


def _cross_attn(q_arr, q_off, k_arr, k_off, v_arr, v_off, *, tq):
    b, t, _ = q_arr.shape
    n_mem = k_arr.shape[1]
    qb, kb, vb = q_off // CROSS_DH, k_off // CROSS_DH, v_off // CROSS_DH
    return pl.pallas_call(
        functools.partial(_cross_attn_kernel, scale=CROSS_DH ** -0.5),
        grid=(b, CROSS_H, t // tq),
        in_specs=[pl.BlockSpec((1, tq, CROSS_DH), lambda bb, h, i: (bb, i, qb + h)),
                  pl.BlockSpec((1, n_mem, CROSS_DH), lambda bb, h, i: (bb, 0, kb + h)),
                  pl.BlockSpec((1, n_mem, CROSS_DH), lambda bb, h, i: (bb, 0, vb + h))],
        out_specs=pl.BlockSpec((1, tq, CROSS_DH), lambda bb, h, i: (bb, i, h)),
        out_shape=jax.ShapeDtypeStruct((b, t, CROSS_H * CROSS_DH), F32),
        compiler_params=_params(("arbitrary", "arbitrary", "arbitrary"), 32),
        name="cross_attn",
    )(q_arr, k_arr, v_arr)


def _rwkv_prep_kernel(r_ref, k_ref, v_ref, lo_ref, fr_ref, fk_ref, fv_ref, fl_ref,
                      mur_ref, muk_ref, muv_ref, mul_ref, w0_ref, w2_ref, a0_ref, a2_ref,
                      kk_ref, ka_ref, rk_ref, e1_ref, e2_ref,
                      ro_ref, lwo_ref, ko_ref, vo_ref, ao_ref, bo_ref, bonus_ref,
                      cr_ref, ck_ref, cv_ref, cl_ref):
    @pl.when(pl.program_id(1) == 0)
    def _():
        cr_ref[...] = fr_ref[0]
        ck_ref[...] = fk_ref[0]
        cv_ref[...] = fv_ref[0]
        cl_ref[...] = fl_ref[0]

    def shift_mix(x_ref, carry_ref, mu_ref):
        x = x_ref[0]
        rows = x.shape[0]
        prev = pltpu.roll(x, 1, axis=0)
        rid = lax.broadcasted_iota(jnp.int32, x.shape, 0)
        prev = jnp.where(rid == 0, carry_ref[...], prev)
        carry_ref[...] = x[rows - 1:rows, :]
        return x + mu_ref[...] * (prev - x)

    r = shift_mix(r_ref, cr_ref, mur_ref)
    k = shift_mix(k_ref, ck_ref, muk_ref)
    v = shift_mix(v_ref, cv_ref, muv_ref)
    lo = shift_mix(lo_ref, cl_ref, mul_ref)

    xw = -(w0_ref[...] + _dot3(jnp.tanh(lo), w2_ref[...]))
    softplus = jnp.maximum(xw, 0.0) + jnp.log(1.0 + jnp.exp(-jnp.abs(xw)))
    w = -softplus - 0.5
    lwo_ref[0] = -jnp.exp(w)
    a = jax.nn.sigmoid(a0_ref[...] + _dot3(lo, a2_ref[...]))

    def head_sum(x):
        return _dot_sel(_dot_sel(x, e1_ref[...]), e2_ref[...])

    kkr = k * kk_ref[...]
    kk = kkr / jnp.maximum(jnp.sqrt(head_sum(kkr * kkr)), 1e-12)
    k2 = k * (1.0 + (a - 1.0) * ka_ref[...])
    ro_ref[0] = r
    ko_ref[0] = k2
    vo_ref[0] = v
    ao_ref[0] = -kk
    bo_ref[0] = kk * a
    bonus_ref[0] = head_sum(r * k2 * rk_ref[...]) * v


def _rwkv_prep(p3, firsts, mus, w0, w2p, a0, a2p, k_k, k_a, r_k, e1, e2, *, tm):
    b, t, _ = p3.shape
    nt = t // tm
    wide = lambda blk: pl.BlockSpec((1, tm, RWKV_W), lambda bb, i: (bb, i, blk))
    vec = lambda n: pl.BlockSpec((1, n), lambda bb, i: (0, 0))
    first = lambda n: pl.BlockSpec((1, 1, n), lambda bb, i: (bb, 0, 0))
    out_wide = pl.BlockSpec((1, tm, RWKV_W), lambda bb, i: (bb, i, 0))
    in_specs = [wide(OFF_R // RWKV_W), wide(OFF_K // RWKV_W), wide(OFF_V // RWKV_W),
                pl.BlockSpec((1, tm, LANES), lambda bb, i: (bb, i, OFF_LORA // LANES)),
                first(RWKV_W), first(RWKV_W), first(RWKV_W), first(LANES),
                vec(RWKV_W), vec(RWKV_W), vec(RWKV_W), vec(LANES),
                vec(RWKV_W), pl.BlockSpec((LANES, RWKV_W), lambda bb, i: (0, 0)),
                vec(RWKV_W), pl.BlockSpec((LANES, RWKV_W), lambda bb, i: (0, 0)),
                vec(RWKV_W), vec(RWKV_W), vec(RWKV_W),
                pl.BlockSpec((RWKV_W, LANES), lambda bb, i: (0, 0)),
                pl.BlockSpec((LANES, RWKV_W), lambda bb, i: (0, 0))]
    shape = jax.ShapeDtypeStruct((b, t, RWKV_W), F32)
    return pl.pallas_call(
        _rwkv_prep_kernel,
        grid=(b, nt),
        in_specs=in_specs,
        out_specs=[out_wide] * 7,
        out_shape=[shape] * 7,
        scratch_shapes=[pltpu.VMEM((1, RWKV_W), F32), pltpu.VMEM((1, RWKV_W), F32),
                        pltpu.VMEM((1, RWKV_W), F32), pltpu.VMEM((1, LANES), F32)],
        compiler_params=_params(("arbitrary", "arbitrary"), 48),
        name="rwkv_prep",
    )(p3, p3, p3, p3, *firsts, *mus, w0, w2p, a0, a2p, k_k, k_a, r_k, e1, e2)


def _rwkv_scan_kernel(r_ref, lw_ref, k_ref, v_ref, a_ref, b_ref, bonus_ref, s0_ref, g_ref, bb_ref,
                      y_ref, sout_ref, s_ref, *, chunk, n_pairs, nb):
    c = pl.program_id(1)
    c2 = 2 * chunk

    @pl.when(c == 0)
    def _():
        s_ref[...] = s0_ref[...]

    rid = lax.broadcasted_iota(jnp.int32, (c2, c2), 0)
    cid = lax.broadcasted_iota(jnp.int32, (c2, c2), 1)
    strict = rid > cid
    incl2 = jnp.concatenate([rid >= cid, rid >= cid], axis=1)
    tr = lax.broadcasted_iota(jnp.int32, (chunk, chunk), 0)
    tc = lax.broadcasted_iota(jnp.int32, (chunk, chunk), 1)
    tri = (tr >= tc).astype(BF16)
    srow = lax.broadcasted_iota(jnp.int32, (c2, LANES), 0)
    slane = lax.broadcasted_iota(jnp.int32, (c2, LANES), 1)
    own = (srow < chunk) == (slane < RWKV_N)

    def stack(x):
        return jnp.where(own, jnp.concatenate([x, x], axis=0), 0.0)

    def cat2(x, y, axis):
        return tuple(jnp.concatenate([p, q], axis=axis) for p, q in zip(x, y))

    units = [(bi, pi) for bi in range(nb) for pi in range(n_pairs)]
    pairs = range(len(units))
    sls = [(bi, slice(None), pl.ds(pi * LANES, LANES)) for bi, pi in units]
    lhs, rhs, vm, g_end, s_prev = [], [], [], [], []
    for sl in sls:
        lw = lw_ref[sl]
        h1 = lw.astype(BF16)
        r1 = lw - h1.astype(F32)
        h2 = r1.astype(BF16)
        h3 = (r1 - h2.astype(F32)).astype(BF16)
        cum = _dot(tri, h1) + (_dot(tri, h2) + _dot(tri, h3))
        g = jnp.exp(cum)
        gi = jnp.exp(-cum)
        gp = jnp.exp(cum - lw)
        am = stack(a_ref[sl] * gp)
        rm = stack(r_ref[sl] * g)
        bm = stack(b_ref[sl] * gi)
        km = stack(k_ref[sl] * gi)
        lhs.append(_split2(jnp.concatenate([am, rm], axis=0)))
        rhs.append(_split2(jnp.concatenate([bm, km], axis=0)))
        vm.append(_split2(stack(v_ref[sl])))
        g_end.append(g[chunk - 1:chunk, :])
    for bi, pi in units:
        s_prev.append(s_ref[bi, pi])

    ps = SCAN_PASSES
    p1 = [_dot3s(lhs[p], rhs[p], NT, ps['intra']) for p in pairs]
    p2 = [_dot3s(lhs[p], _split2(s_prev[p]), NT, ps['state_read']) for p in pairs]
    nm = [_split2(jnp.where(strict, p1[p][:c2, :c2], 0.0)) for p in pairs]
    aak = [_split2(jnp.where(strict, p1[p][:c2, c2:], 0.0)) for p in pairs]
    m2 = [_split2(jnp.where(incl2, p1[p][c2:, :], 0.0)) for p in pairs]
    u = [p2[p][:c2] + _dot3s(aak[p], vm[p], NN, ps['av']) for p in pairs]
    n = 1
    while 2 * n < chunk:
        us = [_split2(u[p]) for p in pairs]
        y = [_dot3s(nm[p], cat2(nm[p], us[p], 1), NN, ps['solve']) for p in pairs]
        u = [u[p] + y[p][:, c2:] for p in pairs]
        nm = [_split2(y[p][:, :c2]) for p in pairs]
        n *= 2
    us = [_split2(u[p]) for p in pairs]
    u = [u[p] + _dot3s(nm[p], us[p], NN, ps['solve']) for p in pairs]
    uv = [cat2(_split2(u[p]), vm[p], 0) for p in pairs]
    o = [p2[p][c2:] + _dot3s(m2[p], uv[p], NN, ps['out']) for p in pairs]
    for p, (bi, pi) in zip(pairs, units):
        s_ref[bi, pi] = (s_prev[p] + _dot3s(uv[p], rhs[p], TN, ps['update'])) * g_end[p]

    for p, sl in zip(pairs, sls):
        mu = jnp.sum(o[p], axis=-1, keepdims=True) * (1.0 / RWKV_N)
        d = jnp.where(own, o[p] - mu, 0.0)
        var = jnp.sum(d * d, axis=-1, keepdims=True) * (1.0 / RWKV_N)
        yn = d * lax.rsqrt(var + RWKV_GN_EPS)
        y = yn[:chunk] + yn[chunk:]
        y_ref[sl] = y * g_ref[:, sl[2]] + bb_ref[:, sl[2]] + bonus_ref[sl]

    @pl.when(c == pl.num_programs(1) - 1)
    def _():
        sout_ref[...] = s_ref[...]


def _rwkv_scan(r, lw, k, v, a, b, bonus, s0_pairs, ln_g, ln_b, *, nb):
    bsz, t, _ = r.shape
    n_pairs = RWKV_H // 2
    assert bsz % nb == 0
    blk = pl.BlockSpec((nb, CHUNK, RWKV_W), lambda bb, c: (bb, c, 0))
    st = pl.BlockSpec((nb, n_pairs, LANES, LANES), lambda bb, c: (bb, 0, 0, 0))
    vec = pl.BlockSpec((1, RWKV_W), lambda bb, c: (0, 0))
    return pl.pallas_call(
        functools.partial(_rwkv_scan_kernel, chunk=CHUNK, n_pairs=n_pairs, nb=nb),
        grid=(bsz // nb, t // CHUNK),
        in_specs=[blk] * 7 + [st, vec, vec],
        out_specs=[blk, st],
        out_shape=[jax.ShapeDtypeStruct((bsz, t, RWKV_W), F32),
                   jax.ShapeDtypeStruct((bsz, n_pairs, LANES, LANES), F32)],
        scratch_shapes=[pltpu.VMEM((nb, n_pairs, LANES, LANES), F32)],
        compiler_params=_params(("arbitrary", "arbitrary"), 40),
        name="rwkv_scan",
    )(r, lw, k, v, a, b, bonus, s0_pairs, ln_g, ln_b)


def _pair_states(s):
    bsz = s.shape[0]
    s = s.reshape(bsz, RWKV_H // 2, 2, RWKV_N, RWKV_N)
    z = jnp.zeros_like(s[:, :, 0])
    top = jnp.concatenate([s[:, :, 0], z], axis=-1)
    bot = jnp.concatenate([z, s[:, :, 1]], axis=-1)
    return jnp.concatenate([top, bot], axis=-2)


def _unpair_states(sp):
    bsz = sp.shape[0]
    even = sp[:, :, :RWKV_N, :RWKV_N]
    odd = sp[:, :, RWKV_N:, RWKV_N:]
    return jnp.stack([even, odd], axis=2).reshape(bsz, RWKV_H, RWKV_N, RWKV_N)


def _merge_out_kernel(x_ref, d_ref, r_ref, c_ref, gate_ref, w_ref, g_ref, o_ref, *, eps):
    gate = gate_ref[...]
    sg = gate * jax.nn.sigmoid(gate)
    nd, nr = d_ref.shape[1], r_ref.shape[1]
    acc = _dot((d_ref[...] * sg[:, :nd]).astype(BF16), w_ref[0:nd, :])
    acc += _dot((r_ref[...] * sg[:, nd:nd + nr]).astype(BF16), w_ref[nd:nd + nr, :])
    acc += _dot((c_ref[...] * sg[:, nd + nr:]).astype(BF16), w_ref[nd + nr:, :])
    x = x_ref[...] + acc
    o_ref[...] = x * lax.rsqrt(jnp.mean(x * x, axis=-1, keepdims=True) + eps) * g_ref[...]


def _merge_out(x, d_o, r_o, c_o, p2, w_out, norm_out, *, tm):
    m, d = x.shape
    row = lambda n: pl.BlockSpec((tm, n), lambda i: (i, 0))
    return pl.pallas_call(
        functools.partial(_merge_out_kernel, eps=1e-6),
        grid=(m // tm,),
        in_specs=[row(d), row(d_o.shape[1]), row(r_o.shape[1]), row(c_o.shape[1]),
                  pl.BlockSpec((tm, d), lambda i: (i, OFF_GATE // d)),
                  pl.BlockSpec((d, d), lambda i: (0, 0)),
                  pl.BlockSpec((1, d), lambda i: (0, 0))],
        out_specs=row(d),
        out_shape=jax.ShapeDtypeStruct((m, d), F32),
        compiler_params=_params(("arbitrary",), 56),
        name="merge_out",
    )(x, d_o, r_o, c_o, p2, w_out, norm_out)


def _reorder_in_cols(w):
    dq, z, cq, gate = 0, 1536, 1536 + 3200, 1536 + 3200 + 512
    parts = [w[..., gate:], w[..., z:z + 3 * RWKV_W], w[..., dq:z], w[..., cq:gate], w[..., z + 3 * RWKV_W:cq]]
    return jnp.concatenate([p.astype(BF16) for p in parts], axis=-1)


def kernel(x_prompt, x_sample, cache_k, cache_v, cache_mem_k, cache_mem_v, state_rwkv, state_shift, page_table,
           mem_prompt, norm_in, w_in, norm_mem, w_mem_kv, lambda_qk, diff_subln, rwkv_mu, rwkv_w0, rwkv_w2, rwkv_a0,
           rwkv_a2, rwkv_k_k, rwkv_k_a, rwkv_r_k, rwkv_ln_g, rwkv_ln_b, w_out, norm_out):
    bsz, seq, d = x_prompt.shape
    db, ds, _ = x_sample.shape
    depth = w_in.shape[0]
    assert depth == 1
    l = 0
    lam_init = 0.8 - 0.6 * math.exp(-0.3 * l)
    n_mem = mem_prompt.shape[1]

    w_in_b = _reorder_in_cols(w_in[l])
    w_shift_b = jnp.concatenate([w_in_b[:, OFF_R:OFF_DQ], w_in_b[:, OFF_LORA:]], axis=1)
    w_mem_b = w_mem_kv[l].astype(BF16)
    w_out_b = w_out[l].astype(BF16)
    g_in = norm_in[l][None]
    g_mem = norm_mem[l][None]
    g_out = norm_out[None]
    mu = rwkv_mu[l]
    mus = (mu[None, :RWKV_W], mu[None, RWKV_W:2 * RWKV_W], mu[None, 2 * RWKV_W:3 * RWKV_W], mu[None, 3 * RWKV_W:])
    zl = jnp.zeros((LORA, RWKV_W), F32)
    w2p = jnp.concatenate([rwkv_w2[l], zl], axis=0)
    a2p = jnp.concatenate([zl, rwkv_a2[l]], axis=0)
    vec = lambda x: x.reshape(1, RWKV_W)
    w0, a0, k_k, k_a, r_k = vec(rwkv_w0[l]), vec(rwkv_a0[l]), vec(rwkv_k_k[l]), vec(rwkv_k_a[l]), vec(rwkv_r_k[l])
    ln_g, ln_b = vec(rwkv_ln_g[l]), vec(rwkv_ln_b[l])
    head_of = np.arange(RWKV_W) // RWKV_N
    e1 = jnp.asarray((head_of[:, None] == np.arange(LANES)[None, :]).astype(np.float32), BF16)
    e2 = jnp.asarray((np.arange(LANES)[:, None] == head_of[None, :]).astype(np.float32), BF16)
    slopes = 2.0 ** (-8.0 * np.arange(1, DIFF_H + 1, dtype=np.float64) / DIFF_H)
    slopes_h = jnp.asarray(np.broadcast_to(slopes[:, None, None], (DIFF_H, 1, LANES)).astype(np.float32))
    subln = diff_subln[l][None]
    lq = lambda_qk[l]

    xp2 = x_prompt.reshape(bsz * seq, d)
    p2 = _norm_matmul(xp2, g_in, w_in_b, normalize=True, tm=512, tn=IN_COLS // 3, vmem_mib=56)
    p3 = p2.reshape(bsz, seq, IN_COLS)
    hs_p = _rmsnorm_rows(x_prompt[:, -1], g_in)
    d_o, k_p, v_p = _diff_attn(p3, slopes_h, lq, subln.reshape(DIFF_DV, 1), tq=512, qw=256, lam_init=lam_init)
    zeros_first = (jnp.zeros((bsz, 1, RWKV_W), F32),) * 3 + (jnp.zeros((bsz, 1, LANES), F32),)
    r, lw, k2, v, a, b, bonus = _rwkv_prep(p3, zeros_first, mus, w0, w2p, a0, a2p, k_k, k_a, r_k, e1, e2, tm=256)
    s0_p = jnp.zeros((bsz, RWKV_H // 2, LANES, LANES), F32)
    r_o, sp_p = _rwkv_scan(r, lw, k2, v, a, b, bonus, s0_p, ln_g, ln_b, nb=2)
    mem_kv = _norm_matmul(mem_prompt.reshape(bsz * n_mem, d), g_mem, w_mem_b, normalize=True, tm=256, tn=512,
                          vmem_mib=32).reshape(bsz, n_mem, 2 * CROSS_H * CROSS_DH)
    c_o = _cross_attn(p3, OFF_CQ, mem_kv, 0, mem_kv, CROSS_H * CROSS_DH, tq=1024)
    y_p = _merge_out(xp2, d_o.reshape(bsz * seq, -1), r_o.reshape(bsz * seq, -1), c_o.reshape(bsz * seq, -1), p2,
                     w_out_b, g_out, tm=512).reshape(bsz, seq, d)
    nk_p = k_p.reshape(1, bsz, seq, DIFF_H, DIFF_DV)
    nv_p = v_p.reshape(1, bsz, seq, DIFF_H, DIFF_DV)
    nmk_p = mem_kv[:, :, :CROSS_H * CROSS_DH].reshape(1, bsz, n_mem, CROSS_H, CROSS_DH)
    nmv_p = mem_kv[:, :, CROSS_H * CROSS_DH:].reshape(1, bsz, n_mem, CROSS_H, CROSS_DH)
    ns_p = _unpair_states(sp_p)[None]

    xs2 = x_sample.reshape(db * ds, d)
    ps2 = _norm_matmul(xs2, g_in, w_in_b, normalize=True, tm=db * ds, tn=IN_COLS // 3, vmem_mib=56)
    ps3 = ps2.reshape(db, ds, IN_COLS)
    hs_s = _rmsnorm_rows(x_sample[:, -1], g_in)
    shift_rows = _pad_rows(state_shift[l], 16)
    z_first = _norm_matmul(shift_rows, g_in, w_shift_b, normalize=False, tm=shift_rows.shape[0],
                           tn=w_shift_b.shape[1], vmem_mib=56)[:db]
    firsts = (z_first[:, None, :RWKV_W], z_first[:, None, RWKV_W:2 * RWKV_W],
              z_first[:, None, 2 * RWKV_W:3 * RWKV_W], z_first[:, None, 3 * RWKV_W:])

    q5 = ps3[:, :, OFF_DQ:OFF_DK].reshape(db, ds, DIFF_H, 2, DIFF_DH)
    zq = jnp.zeros_like(q5[..., 0, :])
    qm = jnp.stack([jnp.concatenate([q5[..., 0, :], zq], -1), jnp.concatenate([zq, q5[..., 1, :]], -1)], axis=1)
    qm = qm.transpose(0, 1, 3, 2, 4).reshape(db, 2 * DIFF_H * ds, DIFF_DV)
    k_new = ps3[:, :, OFF_DK:OFF_DV].reshape(db, ds * DIFF_H, DIFF_DV)
    v_new = ps3[:, :, OFF_DV:OFF_CQ].reshape(db, ds * DIFF_H, DIFF_DV)
    row_head = (np.arange(2 * DIFF_H * ds) // ds) % DIFF_H
    slope_rows = jnp.asarray(np.broadcast_to(slopes[row_head][:, None], (2 * DIFF_H * ds, LANES)).astype(np.float32))
    n_pool, page = cache_k.shape[1], cache_k.shape[2]
    ck = cache_k[l].reshape(n_pool, page * DIFF_H, DIFF_DV)
    cv = cache_v[l].reshape(n_pool, page * DIFF_H, DIFF_DV)
    ds_o = _paged_diff_attn(qm, ck, cv, k_new, v_new, page_table, slope_rows, lq, subln, n_pages=16,
                            lam_init=lam_init)
    ds_o = ds_o.reshape(db, DIFF_H, ds, DIFF_DV).transpose(0, 2, 1, 3).reshape(db * ds, DIFF_H * DIFF_DV)

    pad_t = 8
    ps3_pad = jnp.pad(ps3, ((0, 0), (0, pad_t - ds), (0, 0)))
    outs = _rwkv_prep(ps3_pad, firsts, mus, w0, w2p, a0, a2p, k_k, k_a, r_k, e1, e2, tm=pad_t)
    outs = [jnp.pad(o[:, :ds], ((0, 0), (0, CHUNK - ds), (0, 0))) for o in outs]
    rs_o, sp_s = _rwkv_scan(*outs, _pair_states(state_rwkv[l]), ln_g, ln_b, nb=2)
    rs_o = rs_o[:, :ds].reshape(db * ds, RWKV_W)

    cq_pad = jnp.pad(ps3[:, :, OFF_CQ:OFF_LORA], ((0, 0), (0, 16 - ds), (0, 0)))
    mk_s = cache_mem_k[l].reshape(db, n_mem, CROSS_H * CROSS_DH)
    mv_s = cache_mem_v[l].reshape(db, n_mem, CROSS_H * CROSS_DH)
    cs_o = _cross_attn(cq_pad, 0, mk_s, 0, mv_s, 0, tq=16)[:, :ds].reshape(db * ds, CROSS_H * CROSS_DH)
    y_s = _merge_out(xs2, ds_o, rs_o, cs_o, ps2, w_out_b, g_out, tm=db * ds).reshape(db, ds, d)
    nk_s = ps3[:, :, OFF_DK:OFF_DV].reshape(1, db, ds, DIFF_H, DIFF_DV)
    nv_s = ps3[:, :, OFF_DV:OFF_CQ].reshape(1, db, ds, DIFF_H, DIFF_DV)
    ns_s = _unpair_states(sp_s)[None]

    return (y_p, y_s, nk_p, nv_p, nmk_p, nmv_p, ns_p, hs_p[None], nk_s, nv_s, ns_s, hs_s[None])
```

```python
import functools
import math

import jax
import jax.numpy as jnp
import numpy as np
from jax import lax
from jax.experimental import pallas as pl
from jax.experimental.pallas import tpu as pltpu

F32 = jnp.float32
BF16 = jnp.bfloat16

LANES = 128
MIB = 1024 * 1024

DIFF_H = 4
DIFF_DV = 128
DIFF_DH = 64
RWKV_N = 64
RWKV_H = 16
RWKV_W = RWKV_H * RWKV_N
CROSS_H = 4
CROSS_DH = 128
LORA = 64
RWKV_GN_EPS = 64e-5
CHUNK = 64
NEG = -1e30

OFF_GATE = 0
OFF_R = 2048
OFF_K = 3072
OFF_V = 4096
OFF_DQ = 5120
OFF_DK = 5632
OFF_DV = 6144
OFF_CQ = 6656
OFF_LORA = 7168
IN_COLS = 7296


def _params(sem, vmem_mib):
    return pltpu.CompilerParams(dimension_semantics=sem, vmem_limit_bytes=vmem_mib * MIB)


def _split2(x):
    hi = x.astype(BF16)
    lo = (x - hi.astype(F32)).astype(BF16)
    return hi, lo


def _dot(a, b, dims=(((1,), (0,)), ((), ()))):
    return lax.dot_general(a, b, dims, preferred_element_type=F32)


NN = (((1,), (0,)), ((), ()))
NT = (((1,), (1,)), ((), ()))
TN = (((0,), (0,)), ((), ()))


def _dot3(a, b, dims=NN):
    ah, al = _split2(a)
    bh, bl = _split2(b)
    return _dot(ah, bh, dims) + (_dot(ah, bl, dims) + _dot(al, bh, dims))


def _dot3s(a, b, dims=NN, passes=3):
    ah, al = a
    bh, bl = b
    out = _dot(ah, bh, dims)
    if passes >= 2:
        out = out + _dot(al, bh, dims)
    if passes >= 3:
        out = out + _dot(ah, bl, dims)
    return out


SCAN_PASSES = dict(intra=1, state_read=1, av=1, solve=1, out=1, update=1)


def _dot_sel(x, sel):
    h, l = _split2(x)
    return _dot(h, sel) + _dot(l, sel)


def _norm_matmul_kernel(x_ref, g_ref, w_ref, o_ref, h_ref, *, normalize, eps):
    @pl.when(pl.program_id(1) == 0)
    def _():
        x = x_ref[...]
        if normalize:
            x = x * lax.rsqrt(jnp.mean(x * x, axis=-1, keepdims=True) + eps) * g_ref[...]
        h_ref[...] = x.astype(BF16)

    o_ref[...] = jnp.dot(h_ref[...], w_ref[...], preferred_element_type=F32)


def _norm_matmul(x, g, w, *, normalize, tm, tn, vmem_mib):
    m, d = x.shape
    n = w.shape[1]
    assert m % tm == 0 and n % tn == 0
    return pl.pallas_call(
        functools.partial(_norm_matmul_kernel, normalize=normalize, eps=1e-6),
        grid=(m // tm, n // tn),
        in_specs=[pl.BlockSpec((tm, d), lambda i, j: (i, 0)),
                  pl.BlockSpec((1, d), lambda i, j: (0, 0)),
                  pl.BlockSpec((d, tn), lambda i, j: (0, j))],
        out_specs=pl.BlockSpec((tm, tn), lambda i, j: (i, j)),
        out_shape=jax.ShapeDtypeStruct((m, n), F32),
        scratch_shapes=[pltpu.VMEM((tm, d), BF16)],
        compiler_params=_params(("arbitrary", "arbitrary"), vmem_mib),
        name="norm_matmul",
    )(x, g, w)


def _rmsnorm_kernel(x_ref, g_ref, o_ref, *, eps):
    x = x_ref[...]
    o_ref[...] = x * lax.rsqrt(jnp.mean(x * x, axis=-1, keepdims=True) + eps) * g_ref[...]


def _pad_rows(x, mult):
    pad = -x.shape[0] % mult
    return jnp.pad(x, ((0, pad), (0, 0))) if pad else x


def _rmsnorm_rows(x, g):
    n = x.shape[0]
    xp = _pad_rows(x, 8)
    return pl.pallas_call(
        functools.partial(_rmsnorm_kernel, eps=1e-6),
        out_shape=jax.ShapeDtypeStruct(xp.shape, F32),
        name="rmsnorm_rows",
    )(xp, g)[:n]


def _diff_lambda(lq, lam_init):
    t1 = jnp.sum(lq[0:1] * lq[1:2], axis=-1, keepdims=True)
    t2 = jnp.sum(lq[2:3] * lq[3:4], axis=-1, keepdims=True)
    return jnp.exp(t1) - jnp.exp(t2) + lam_init


def _diff_finish(acc, l, lq, subln, rows, lam_init):
    o12 = acc / l
    o = o12[:rows] - _diff_lambda(lq, lam_init) * o12[rows:]
    o = o * lax.rsqrt(jnp.mean(o * o, axis=-1, keepdims=True) + 1e-5) * subln
    return o * (1.0 - lam_init)


LOG2E = math.log2(math.e)
POS_RADIX = 16
N_SLOPE_PIECES = 3


def _alibi_lanes(slopes, tq):
    assert tq <= POS_RADIX * 256
    c = np.asarray(slopes, np.float64) * LOG2E
    pieces, rest = [], c
    for _ in range(N_SLOPE_PIECES):
        piece = rest.astype(np.float32).astype(BF16).astype(np.float64)
        pieces.append(piece)
        rest = rest - piece
    q_lanes = np.zeros((len(c), 1, LANES), np.float32)
    k_lanes = np.zeros((tq, LANES), np.float32)
    pos = np.arange(tq)
    for n, piece in enumerate(pieces):
        q_lanes[:, 0, 2 * n] = POS_RADIX * piece
        q_lanes[:, 0, 2 * n + 1] = piece
        k_lanes[:, 2 * n] = pos // POS_RADIX
        k_lanes[:, 2 * n + 1] = pos % POS_RADIX
    c_sum = np.broadcast_to(sum(pieces)[:, None, None], (len(c), 1, LANES)).astype(np.float32)
    return jnp.asarray(q_lanes), jnp.asarray(k_lanes.astype(BF16)), jnp.asarray(c_sum)


def _stack_maps(q, q_lanes):
    q = q * (DIFF_DH ** -0.5 * LOG2E)
    lane = lax.broadcasted_iota(jnp.int32, q.shape, 1)
    q1 = jnp.where(lane < DIFF_DH, q, 0.0)
    q2 = jnp.where(lane >= DIFF_DH, q, 0.0)
    qq = jnp.concatenate([q1, q2], axis=0)
    return jnp.concatenate([qq, jnp.broadcast_to(q_lanes, qq.shape)], axis=1).astype(BF16)


def _diff_attn_kernel(qi_ref, kj_ref, q_ref, k_ref, v_ref, qlane_ref, klane_ref, slope_ref, lq_ref, subln_ref,
                      o_ref, ko_ref, vo_ref, qq_ref, m_ref, l_ref, acc_ref, *, tq, qw, lam_init):
    t = pl.program_id(2)
    i = qi_ref[t]
    j = kj_ref[t]

    @pl.when(j == 0)
    def _():
        qq_ref[...] = _stack_maps(q_ref[0], qlane_ref[0])
        m_ref[...] = jnp.full(m_ref.shape, NEG, F32)
        l_ref[...] = jnp.zeros(l_ref.shape, F32)
        acc_ref[...] = jnp.zeros(acc_ref.shape, F32)

    kb = jnp.concatenate([k_ref[0].astype(BF16), klane_ref[...]], axis=1)
    vb = v_ref[0].astype(BF16)
    block_off = slope_ref[0][:, :1] * jnp.full((1, 1), (j - i) * tq, jnp.int32).astype(F32)

    def accumulate(diagonal):
        starts = list(range(0, 2 * tq, qw))
        tiles = [pl.ds(c0, qw) for c0 in starts]
        n_keys = [c0 % tq + qw if diagonal else tq for c0 in starts]
        scores = []
        for c0, cols, nk in zip(starts, tiles, n_keys):
            s = _dot(kb[:nk], qq_ref[cols, :], NT)
            if diagonal:
                qry = c0 % tq + lax.broadcasted_iota(jnp.int32, s.shape, 1)
                s = jnp.where(lax.broadcasted_iota(jnp.int32, s.shape, 0) <= qry, s, NEG)
            scores.append(s)
        for s, cols, nk in zip(scores, tiles, n_keys):
            m_prev = m_ref[:, cols] - block_off
            m_new = jnp.maximum(m_prev, jnp.max(s, axis=0, keepdims=True))
            alpha = jnp.exp2(m_prev - m_new)
            p = jnp.exp2(s - m_new)
            l_ref[:, cols] = alpha * l_ref[:, cols] + jnp.sum(p, axis=0, keepdims=True)
            acc_ref[:, cols] = alpha * acc_ref[:, cols] + _dot(vb[:nk], p.astype(BF16), TN)
            m_ref[:, cols] = m_new + block_off

    @pl.when(j < i)
    def _():
        accumulate(False)

    @pl.when(j == i)
    def _():
        accumulate(True)
        o12 = acc_ref[...] / l_ref[...]
        o = o12[:, :tq] - _diff_lambda(lq_ref[...], lam_init) * o12[:, tq:]
        o = o * lax.rsqrt(jnp.mean(o * o, axis=0, keepdims=True) + 1e-5) * subln_ref[...]
        o_ref[0] = (o * (1.0 - lam_init)).T
        ko_ref[0] = k_ref[0]
        vo_ref[0] = v_ref[0]


def _diff_attn(p3, slopes, lambda_qk, subln_col, *, tq, qw, lam_init):
    b, t, _ = p3.shape
    nq = t // tq
    q_lanes, k_lanes, c_sum = _alibi_lanes(slopes, tq)
    pairs = [(i, j) for i in range(nq) for j in range(i + 1)]
    qi = jnp.asarray(np.array([p[0] for p in pairs], np.int32))
    kj = jnp.asarray(np.array([p[1] for p in pairs], np.int32))
    qb, kb, vb = OFF_DQ // DIFF_DV, OFF_DK // DIFF_DV, OFF_DV // DIFF_DV
    grid_spec = pltpu.PrefetchScalarGridSpec(
        num_scalar_prefetch=2,
        grid=(b, DIFF_H, len(pairs)),
        in_specs=[pl.BlockSpec((1, tq, DIFF_DV), lambda bb, h, s, qi, kj: (bb, qi[s], qb + h)),
                  pl.BlockSpec((1, tq, DIFF_DV), lambda bb, h, s, qi, kj: (bb, kj[s], kb + h)),
                  pl.BlockSpec((1, tq, DIFF_DV), lambda bb, h, s, qi, kj: (bb, kj[s], vb + h)),
                  pl.BlockSpec((1, 1, LANES), lambda bb, h, s, qi, kj: (h, 0, 0)),
                  pl.BlockSpec((tq, LANES), lambda bb, h, s, qi, kj: (0, 0)),
                  pl.BlockSpec((1, 1, LANES), lambda bb, h, s, qi, kj: (h, 0, 0)),
                  pl.BlockSpec((4, DIFF_DH), lambda bb, h, s, qi, kj: (0, 0)),
                  pl.BlockSpec((DIFF_DV, 1), lambda bb, h, s, qi, kj: (0, 0))],
        out_specs=[pl.BlockSpec((1, tq, DIFF_DV), lambda bb, h, s, qi, kj: (bb, qi[s], h))] * 3,
        scratch_shapes=[pltpu.VMEM((2 * tq, 2 * DIFF_DV), BF16), pltpu.VMEM((1, 2 * tq), F32),
                        pltpu.VMEM((1, 2 * tq), F32), pltpu.VMEM((DIFF_DV, 2 * tq), F32)],
    )
    return pl.pallas_call(
        functools.partial(_diff_attn_kernel, tq=tq, qw=qw, lam_init=lam_init),
        grid_spec=grid_spec,
        out_shape=[jax.ShapeDtypeStruct((b, t, DIFF_H * DIFF_DV), F32)] * 3,
        compiler_params=_params(("arbitrary", "arbitrary", "arbitrary"), 40),
        name="diff_attn",
    )(qi, kj, p3, p3, p3, q_lanes, k_lanes, c_sum, lambda_qk, subln_col)


def _paged_diff_attn_kernel(pt_ref, q_ref, *refs, n_pages, page, n_new, past_len, lam_init):
    k_refs = refs[:n_pages]
    v_refs = refs[n_pages:2 * n_pages]
    kn_ref, vn_ref, slope_ref, lq_ref, subln_ref, o_ref, m_ref, l_ref, acc_ref = refs[2 * n_pages:]
    s_id = pl.program_id(1)
    rows = 2 * DIFF_H * n_new

    @pl.when(s_id == 0)
    def _():
        m_ref[...] = jnp.full(m_ref.shape, NEG, F32)
        l_ref[...] = jnp.zeros(l_ref.shape, F32)
        acc_ref[...] = jnp.zeros(acc_ref.shape, F32)

    qb = (q_ref[0] * (DIFF_DH ** -0.5)).astype(BF16)
    slope = slope_ref[...][:, :1]
    log_new, log_h = n_new.bit_length() - 1, DIFF_H.bit_length() - 1

    def coords(ncol):
        row = lax.broadcasted_iota(jnp.int32, (rows, ncol), 0)
        col = lax.broadcasted_iota(jnp.int32, (rows, ncol), 1)
        row_h = lax.shift_right_logical(row, log_new) & (DIFF_H - 1)
        return row & (n_new - 1), (col & (DIFF_H - 1)) == row_h, lax.shift_right_logical(col, log_h)

    def update(ss, vbs, carry):
        m_prev, l_prev, acc_prev = carry
        m_new = m_prev
        for s in ss:
            m_new = jnp.maximum(m_new, jnp.max(s, axis=-1, keepdims=True))
        alpha = jnp.exp(m_prev - m_new)
        l_new = alpha * l_prev
        acc_new = alpha * acc_prev
        for s, vb in zip(ss, vbs):
            p = jnp.exp(s - m_new)
            l_new = l_new + jnp.sum(p, axis=-1, keepdims=True)
            acc_new = acc_new + _dot(p.astype(BF16), vb)
        return m_new, l_new, acc_new

    _, same_head, tok = coords(page * DIFF_H)
    ss, vbs = [], []
    for i in range(n_pages):
        kb = k_refs[i][0].astype(BF16)
        vbs.append(v_refs[i][0].astype(BF16))
        k_pos = ((s_id * n_pages + i) * page - past_len + tok).astype(F32)
        ss.append(jnp.where(same_head, _dot(qb, kb, NT) + slope * k_pos, NEG))
    m_ref[...], l_ref[...], acc_ref[...] = update(ss, vbs, (m_ref[...], l_ref[...], acc_ref[...]))

    @pl.when(s_id == pl.num_programs(1) - 1)
    def _():
        kb = kn_ref[0].astype(BF16)
        vb = vn_ref[0].astype(BF16)
        row_tn, same_head_n, tok_n = coords(n_new * DIFF_H)
        k_pos = tok_n.astype(F32)
        s = jnp.where(same_head_n & (tok_n <= row_tn), _dot(qb, kb, NT) + slope * k_pos, NEG)
        _, l, acc = update([s], [vb], (m_ref[...], l_ref[...], acc_ref[...]))
        o_ref[0] = _diff_finish(acc, l, lq_ref[...], subln_ref[...], rows // 2, lam_init)


def _paged_diff_attn(qm, cache_k, cache_v, k_new, v_new, page_table, slope_rows, lambda_qk, subln, *,
                     n_pages, lam_init):
    db, rows, _ = qm.shape
    n_pool, page = cache_k.shape[0], cache_k.shape[1] // DIFF_H
    n_tab = page_table.shape[1]
    n_new = rows // (2 * DIFF_H)
    assert n_tab % n_pages == 0 and n_new & (n_new - 1) == 0 and DIFF_H & (DIFF_H - 1) == 0

    def page_spec(i):
        return pl.BlockSpec((1, page * DIFF_H, DIFF_DV), lambda b, s, pt: (pt[b, s * n_pages + i], 0, 0))

    grid_spec = pltpu.PrefetchScalarGridSpec(
        num_scalar_prefetch=1,
        grid=(db, n_tab // n_pages),
        in_specs=([pl.BlockSpec((1, rows, DIFF_DV), lambda b, s, pt: (b, 0, 0))]
                  + [page_spec(i) for i in range(n_pages)] * 2
                  + [pl.BlockSpec((1, n_new * DIFF_H, DIFF_DV), lambda b, s, pt: (b, 0, 0)),
                     pl.BlockSpec((1, n_new * DIFF_H, DIFF_DV), lambda b, s, pt: (b, 0, 0)),
                     pl.BlockSpec((rows, LANES), lambda b, s, pt: (0, 0)),
                     pl.BlockSpec((4, DIFF_DH), lambda b, s, pt: (0, 0)),
                     pl.BlockSpec((1, DIFF_DV), lambda b, s, pt: (0, 0))]),
        out_specs=pl.BlockSpec((1, rows // 2, DIFF_DV), lambda b, s, pt: (b, 0, 0)),
        scratch_shapes=[pltpu.VMEM((rows, 1), F32), pltpu.VMEM((rows, 1), F32), pltpu.VMEM((rows, DIFF_DV), F32)],
    )
    return pl.pallas_call(
        functools.partial(_paged_diff_attn_kernel, n_pages=n_pages, page=page, n_new=n_new,
                          past_len=n_tab * page, lam_init=lam_init),
        grid_spec=grid_spec,
        out_shape=jax.ShapeDtypeStruct((db, rows // 2, DIFF_DV), F32),
        compiler_params=_params(("arbitrary", "arbitrary"), 32),
        name="paged_diff_attn",
    )(page_table, qm, *([cache_k] * n_pages), *([cache_v] * n_pages), k_new, v_new, slope_rows, lambda_qk, subln)


def _cross_attn_kernel(q_ref, k_ref, v_ref, o_ref, *, scale):
    q = q_ref[0].astype(BF16)
    k = k_ref[0].astype(BF16)
    v = v_ref[0].astype(BF16)
    s = _dot(q, k, NT) * scale
    p = jnp.exp(s - jnp.max(s, axis=-1, keepdims=True))
    l = jnp.sum(p, axis=-1, keepdims=True)
    o_ref[0] = _dot(p.astype(BF16), v) / l


def _cross_attn(q_arr, q_off, k_arr, k_off, v_arr, v_off, *, tq):
    b, t, _ = q_arr.shape
    n_mem = k_arr.shape[1]
    qb, kb, vb = q_off // CROSS_DH, k_off // CROSS_DH, v_off // CROSS_DH
    return pl.pallas_call(
        functools.partial(_cross_attn_kernel, scale=CROSS_DH ** -0.5),
        grid=(b, CROSS_H, t // tq),
        in_specs=[pl.BlockSpec((1, tq, CROSS_DH), lambda bb, h, i: (bb, i, qb + h)),
                  pl.BlockSpec((1, n_mem, CROSS_DH), lambda bb, h, i: (bb, 0, kb + h)),
                  pl.BlockSpec((1, n_mem, CROSS_DH), lambda bb, h, i: (bb, 0, vb + h))],
        out_specs=pl.BlockSpec((1, tq, CROSS_DH), lambda bb, h, i: (bb, i, h)),
        out_shape=jax.ShapeDtypeStruct((b, t, CROSS_H * CROSS_DH), F32),
        compiler_params=_params(("arbitrary", "arbitrary", "arbitrary"), 32),
        name="cross_attn",
    )(q_arr, k_arr, v_arr)


def _rwkv_prep_kernel(r_ref, k_ref, v_ref, lo_ref, fr_ref, fk_ref, fv_ref, fl_ref,
                      mur_ref, muk_ref, muv_ref, mul_ref, w0_ref, w2_ref, a0_ref, a2_ref,
                      kk_ref, ka_ref, rk_ref, e1_ref, e2_ref,
                      ro_ref, lwo_ref, ko_ref, vo_ref, ao_ref, bo_ref, bonus_ref,
                      cr_ref, ck_ref, cv_ref, cl_ref):
    @pl.when(pl.program_id(1) == 0)
    def _():
        cr_ref[...] = fr_ref[0]
        ck_ref[...] = fk_ref[0]
        cv_ref[...] = fv_ref[0]
        cl_ref[...] = fl_ref[0]

    def shift_mix(x_ref, carry_ref, mu_ref):
        x = x_ref[0]
        rows = x.shape[0]
        prev = pltpu.roll(x, 1, axis=0)
        rid = lax.broadcasted_iota(jnp.int32, x.shape, 0)
        prev = jnp.where(rid == 0, carry_ref[...], prev)
        carry_ref[...] = x[rows - 1:rows, :]
        return x + mu_ref[...] * (prev - x)

    r = shift_mix(r_ref, cr_ref, mur_ref)
    k = shift_mix(k_ref, ck_ref, muk_ref)
    v = shift_mix(v_ref, cv_ref, muv_ref)
    lo = shift_mix(lo_ref, cl_ref, mul_ref)

    xw = -(w0_ref[...] + _dot3(jnp.tanh(lo), w2_ref[...]))
    softplus = jnp.maximum(xw, 0.0) + jnp.log(1.0 + jnp.exp(-jnp.abs(xw)))
    w = -softplus - 0.5
    lwo_ref[0] = -jnp.exp(w)
    a = jax.nn.sigmoid(a0_ref[...] + _dot3(lo, a2_ref[...]))

    def head_sum(x):
        return _dot_sel(_dot_sel(x, e1_ref[...]), e2_ref[...])

    kkr = k * kk_ref[...]
    kk = kkr / jnp.maximum(jnp.sqrt(head_sum(kkr * kkr)), 1e-12)
    k2 = k * (1.0 + (a - 1.0) * ka_ref[...])
    ro_ref[0] = r
    ko_ref[0] = k2
    vo_ref[0] = v
    ao_ref[0] = -kk
    bo_ref[0] = kk * a
    bonus_ref[0] = head_sum(r * k2 * rk_ref[...]) * v


def _rwkv_prep(p3, firsts, mus, w0, w2p, a0, a2p, k_k, k_a, r_k, e1, e2, *, tm):
    b, t, _ = p3.shape
    nt = t // tm
    wide = lambda blk: pl.BlockSpec((1, tm, RWKV_W), lambda bb, i: (bb, i, blk))
    vec = lambda n: pl.BlockSpec((1, n), lambda bb, i: (0, 0))
    first = lambda n: pl.BlockSpec((1, 1, n), lambda bb, i: (bb, 0, 0))
    out_wide = pl.BlockSpec((1, tm, RWKV_W), lambda bb, i: (bb, i, 0))
    in_specs = [wide(OFF_R // RWKV_W), wide(OFF_K // RWKV_W), wide(OFF_V // RWKV_W),
                pl.BlockSpec((1, tm, LANES), lambda bb, i: (bb, i, OFF_LORA // LANES)),
                first(RWKV_W), first(RWKV_W), first(RWKV_W), first(LANES),
                vec(RWKV_W), vec(RWKV_W), vec(RWKV_W), vec(LANES),
                vec(RWKV_W), pl.BlockSpec((LANES, RWKV_W), lambda bb, i: (0, 0)),
                vec(RWKV_W), pl.BlockSpec((LANES, RWKV_W), lambda bb, i: (0, 0)),
                vec(RWKV_W), vec(RWKV_W), vec(RWKV_W),
                pl.BlockSpec((RWKV_W, LANES), lambda bb, i: (0, 0)),
                pl.BlockSpec((LANES, RWKV_W), lambda bb, i: (0, 0))]
    shape = jax.ShapeDtypeStruct((b, t, RWKV_W), F32)
    return pl.pallas_call(
        _rwkv_prep_kernel,
        grid=(b, nt),
        in_specs=in_specs,
        out_specs=[out_wide] * 7,
        out_shape=[shape] * 7,
        scratch_shapes=[pltpu.VMEM((1, RWKV_W), F32), pltpu.VMEM((1, RWKV_W), F32),
                        pltpu.VMEM((1, RWKV_W), F32), pltpu.VMEM((1, LANES), F32)],
        compiler_params=_params(("arbitrary", "arbitrary"), 48),
        name="rwkv_prep",
    )(p3, p3, p3, p3, *firsts, *mus, w0, w2p, a0, a2p, k_k, k_a, r_k, e1, e2)


def _rwkv_scan_kernel(r_ref, lw_ref, k_ref, v_ref, a_ref, b_ref, bonus_ref, s0_ref, g_ref, bb_ref,
                      y_ref, sout_ref, s_ref, *, chunk, n_pairs, nb):
    c = pl.program_id(1)
    c2 = 2 * chunk

    @pl.when(c == 0)
    def _():
        s_ref[...] = s0_ref[...]

    rid = lax.broadcasted_iota(jnp.int32, (c2, c2), 0)
    cid = lax.broadcasted_iota(jnp.int32, (c2, c2), 1)
    strict = rid > cid
    incl2 = jnp.concatenate([rid >= cid, rid >= cid], axis=1)
    tr = lax.broadcasted_iota(jnp.int32, (chunk, chunk), 0)
    tc = lax.broadcasted_iota(jnp.int32, (chunk, chunk), 1)
    tri = (tr >= tc).astype(BF16)
    srow = lax.broadcasted_iota(jnp.int32, (c2, LANES), 0)
    slane = lax.broadcasted_iota(jnp.int32, (c2, LANES), 1)
    own = (srow < chunk) == (slane < RWKV_N)

    def stack(x):
        return jnp.where(own, jnp.concatenate([x, x], axis=0), 0.0)

    def cat2(x, y, axis):
        return tuple(jnp.concatenate([p, q], axis=axis) for p, q in zip(x, y))

    units = [(bi, pi) for bi in range(nb) for pi in range(n_pairs)]
    pairs = range(len(units))
    sls = [(bi, slice(None), pl.ds(pi * LANES, LANES)) for bi, pi in units]
    lhs, rhs, vm, g_end, s_prev = [], [], [], [], []
    for sl in sls:
        lw = lw_ref[sl]
        h1 = lw.astype(BF16)
        r1 = lw - h1.astype(F32)
        h2 = r1.astype(BF16)
        h3 = (r1 - h2.astype(F32)).astype(BF16)
        cum = _dot(tri, h1) + (_dot(tri, h2) + _dot(tri, h3))
        g = jnp.exp(cum)
        gi = jnp.exp(-cum)
        gp = jnp.exp(cum - lw)
        am = stack(a_ref[sl] * gp)
        rm = stack(r_ref[sl] * g)
        bm = stack(b_ref[sl] * gi)
        km = stack(k_ref[sl] * gi)
        lhs.append(_split2(jnp.concatenate([am, rm], axis=0)))
        rhs.append(_split2(jnp.concatenate([bm, km], axis=0)))
        vm.append(_split2(stack(v_ref[sl])))
        g_end.append(g[chunk - 1:chunk, :])
    for bi, pi in units:
        s_prev.append(s_ref[bi, pi])

    ps = SCAN_PASSES
    p1 = [_dot3s(lhs[p], rhs[p], NT, ps['intra']) for p in pairs]
    p2 = [_dot3s(lhs[p], _split2(s_prev[p]), NT, ps['state_read']) for p in pairs]
    nm = [_split2(jnp.where(strict, p1[p][:c2, :c2], 0.0)) for p in pairs]
    aak = [_split2(jnp.where(strict, p1[p][:c2, c2:], 0.0)) for p in pairs]
    m2 = [_split2(jnp.where(incl2, p1[p][c2:, :], 0.0)) for p in pairs]
    u = [p2[p][:c2] + _dot3s(aak[p], vm[p], NN, ps['av']) for p in pairs]
    n = 1
    while 2 * n < chunk:
        us = [_split2(u[p]) for p in pairs]
        y = [_dot3s(nm[p], cat2(nm[p], us[p], 1), NN, ps['solve']) for p in pairs]
        u = [u[p] + y[p][:, c2:] for p in pairs]
        nm = [_split2(y[p][:, :c2]) for p in pairs]
        n *= 2
    us = [_split2(u[p]) for p in pairs]
    u = [u[p] + _dot3s(nm[p], us[p], NN, ps['solve']) for p in pairs]
    uv = [cat2(_split2(u[p]), vm[p], 0) for p in pairs]
    o = [p2[p][c2:] + _dot3s(m2[p], uv[p], NN, ps['out']) for p in pairs]
    for p, (bi, pi) in zip(pairs, units):
        s_ref[bi, pi] = (s_prev[p] + _dot3s(uv[p], rhs[p], TN, ps['update'])) * g_end[p]

    for p, sl in zip(pairs, sls):
        mu = jnp.sum(o[p], axis=-1, keepdims=True) * (1.0 / RWKV_N)
        d = jnp.where(own, o[p] - mu, 0.0)
        var = jnp.sum(d * d, axis=-1, keepdims=True) * (1.0 / RWKV_N)
        yn = d * lax.rsqrt(var + RWKV_GN_EPS)
        y = yn[:chunk] + yn[chunk:]
        y_ref[sl] = y * g_ref[:, sl[2]] + bb_ref[:, sl[2]] + bonus_ref[sl]

    @pl.when(c == pl.num_programs(1) - 1)
    def _():
        sout_ref[...] = s_ref[...]


def _rwkv_scan(r, lw, k, v, a, b, bonus, s0_pairs, ln_g, ln_b, *, nb):
    bsz, t, _ = r.shape
    n_pairs = RWKV_H // 2
    assert bsz % nb == 0
    blk = pl.BlockSpec((nb, CHUNK, RWKV_W), lambda bb, c: (bb, c, 0))
    st = pl.BlockSpec((nb, n_pairs, LANES, LANES), lambda bb, c: (bb, 0, 0, 0))
    vec = pl.BlockSpec((1, RWKV_W), lambda bb, c: (0, 0))
    return pl.pallas_call(
        functools.partial(_rwkv_scan_kernel, chunk=CHUNK, n_pairs=n_pairs, nb=nb),
        grid=(bsz // nb, t // CHUNK),
        in_specs=[blk] * 7 + [st, vec, vec],
        out_specs=[blk, st],
        out_shape=[jax.ShapeDtypeStruct((bsz, t, RWKV_W), F32),
                   jax.ShapeDtypeStruct((bsz, n_pairs, LANES, LANES), F32)],
        scratch_shapes=[pltpu.VMEM((nb, n_pairs, LANES, LANES), F32)],
        compiler_params=_params(("arbitrary", "arbitrary"), 40),
        name="rwkv_scan",
    )(r, lw, k, v, a, b, bonus, s0_pairs, ln_g, ln_b)


def _pair_states(s):
    bsz = s.shape[0]
    s = s.reshape(bsz, RWKV_H // 2, 2, RWKV_N, RWKV_N)
    z = jnp.zeros_like(s[:, :, 0])
    top = jnp.concatenate([s[:, :, 0], z], axis=-1)
    bot = jnp.concatenate([z, s[:, :, 1]], axis=-1)
    return jnp.concatenate([top, bot], axis=-2)


def _unpair_states(sp):
    bsz = sp.shape[0]
    even = sp[:, :, :RWKV_N, :RWKV_N]
    odd = sp[:, :, RWKV_N:, RWKV_N:]
    return jnp.stack([even, odd], axis=2).reshape(bsz, RWKV_H, RWKV_N, RWKV_N)


def _merge_out_kernel(x_ref, d_ref, r_ref, c_ref, gate_ref, w_ref, g_ref, o_ref, *, eps):
    gate = gate_ref[...]
    sg = gate * jax.nn.sigmoid(gate)
    nd, nr = d_ref.shape[1], r_ref.shape[1]
    acc = _dot((d_ref[...] * sg[:, :nd]).astype(BF16), w_ref[0:nd, :])
    acc += _dot((r_ref[...] * sg[:, nd:nd + nr]).astype(BF16), w_ref[nd:nd + nr, :])
    acc += _dot((c_ref[...] * sg[:, nd + nr:]).astype(BF16), w_ref[nd + nr:, :])
    x = x_ref[...] + acc
    o_ref[...] = x * lax.rsqrt(jnp.mean(x * x, axis=-1, keepdims=True) + eps) * g_ref[...]


def _merge_out(x, d_o, r_o, c_o, p2, w_out, norm_out, *, tm):
    m, d = x.shape
    row = lambda n: pl.BlockSpec((tm, n), lambda i: (i, 0))
    return pl.pallas_call(
        functools.partial(_merge_out_kernel, eps=1e-6),
        grid=(m // tm,),
        in_specs=[row(d), row(d_o.shape[1]), row(r_o.shape[1]), row(c_o.shape[1]),
                  pl.BlockSpec((tm, d), lambda i: (i, OFF_GATE // d)),
                  pl.BlockSpec((d, d), lambda i: (0, 0)),
                  pl.BlockSpec((1, d), lambda i: (0, 0))],
        out_specs=row(d),
        out_shape=jax.ShapeDtypeStruct((m, d), F32),
        compiler_params=_params(("arbitrary",), 56),
        name="merge_out",
    )(x, d_o, r_o, c_o, p2, w_out, norm_out)


def _reorder_in_cols(w):
    dq, z, cq, gate = 0, 1536, 1536 + 3200, 1536 + 3200 + 512
    parts = [w[..., gate:], w[..., z:z + 3 * RWKV_W], w[..., dq:z], w[..., cq:gate], w[..., z + 3 * RWKV_W:cq]]
    return jnp.concatenate([p.astype(BF16) for p in parts], axis=-1)


def kernel(x_prompt, x_sample, cache_k, cache_v, cache_mem_k, cache_mem_v, state_rwkv, state_shift, page_table,
           mem_prompt, norm_in, w_in, norm_mem, w_mem_kv, lambda_qk, diff_subln, rwkv_mu, rwkv_w0, rwkv_w2, rwkv_a0,
           rwkv_a2, rwkv_k_k, rwkv_k_a, rwkv_r_k, rwkv_ln_g, rwkv_ln_b, w_out, norm_out):
    bsz, seq, d = x_prompt.shape
    db, ds, _ = x_sample.shape
    depth = w_in.shape[0]
    assert depth == 1
    l = 0
    lam_init = 0.8 - 0.6 * math.exp(-0.3 * l)
    n_mem = mem_prompt.shape[1]

    w_in_b = _reorder_in_cols(w_in[l])
    w_shift_b = jnp.concatenate([w_in_b[:, OFF_R:OFF_DQ], w_in_b[:, OFF_LORA:]], axis=1)
    w_mem_b = w_mem_kv[l].astype(BF16)
    w_out_b = w_out[l].astype(BF16)
    g_in = norm_in[l][None]
    g_mem = norm_mem[l][None]
    g_out = norm_out[None]
    mu = rwkv_mu[l]
    mus = (mu[None, :RWKV_W], mu[None, RWKV_W:2 * RWKV_W], mu[None, 2 * RWKV_W:3 * RWKV_W], mu[None, 3 * RWKV_W:])
    zl = jnp.zeros((LORA, RWKV_W), F32)
    w2p = jnp.concatenate([rwkv_w2[l], zl], axis=0)
    a2p = jnp.concatenate([zl, rwkv_a2[l]], axis=0)
    vec = lambda x: x.reshape(1, RWKV_W)
    w0, a0, k_k, k_a, r_k = vec(rwkv_w0[l]), vec(rwkv_a0[l]), vec(rwkv_k_k[l]), vec(rwkv_k_a[l]), vec(rwkv_r_k[l])
    ln_g, ln_b = vec(rwkv_ln_g[l]), vec(rwkv_ln_b[l])
    head_of = np.arange(RWKV_W) // RWKV_N
    e1 = jnp.asarray((head_of[:, None] == np.arange(LANES)[None, :]).astype(np.float32), BF16)
    e2 = jnp.asarray((np.arange(LANES)[:, None] == head_of[None, :]).astype(np.float32), BF16)
    slopes = 2.0 ** (-8.0 * np.arange(1, DIFF_H + 1, dtype=np.float64) / DIFF_H)
    subln = diff_subln[l][None]
    lq = lambda_qk[l]

    xp2 = x_prompt.reshape(bsz * seq, d)
    p2 = _norm_matmul(xp2, g_in, w_in_b, normalize=True, tm=512, tn=IN_COLS // 3, vmem_mib=56)
    p3 = p2.reshape(bsz, seq, IN_COLS)
    hs_p = _rmsnorm_rows(x_prompt[:, -1], g_in)
    d_o, k_p, v_p = _diff_attn(p3, slopes, lq, subln.reshape(DIFF_DV, 1), tq=1024, qw=256, lam_init=lam_init)
    zeros_first = (jnp.zeros((bsz, 1, RWKV_W), F32),) * 3 + (jnp.zeros((bsz, 1, LANES), F32),)
    r, lw, k2, v, a, b, bonus = _rwkv_prep(p3, zeros_first, mus, w0, w2p, a0, a2p, k_k, k_a, r_k, e1, e2, tm=256)
    s0_p = jnp.zeros((bsz, RWKV_H // 2, LANES, LANES), F32)
    r_o, sp_p = _rwkv_scan(r, lw, k2, v, a, b, bonus, s0_p, ln_g, ln_b, nb=2)
    mem_kv = _norm_matmul(mem_prompt.reshape(bsz * n_mem, d), g_mem, w_mem_b, normalize=True, tm=256, tn=512,
                          vmem_mib=32).reshape(bsz, n_mem, 2 * CROSS_H * CROSS_DH)
    c_o = _cross_attn(p3, OFF_CQ, mem_kv, 0, mem_kv, CROSS_H * CROSS_DH, tq=1024)
    y_p = _merge_out(xp2, d_o.reshape(bsz * seq, -1), r_o.reshape(bsz * seq, -1), c_o.reshape(bsz * seq, -1), p2,
                     w_out_b, g_out, tm=512).reshape(bsz, seq, d)
    nk_p = k_p.reshape(1, bsz, seq, DIFF_H, DIFF_DV)
    nv_p = v_p.reshape(1, bsz, seq, DIFF_H, DIFF_DV)
    nmk_p = mem_kv[:, :, :CROSS_H * CROSS_DH].reshape(1, bsz, n_mem, CROSS_H, CROSS_DH)
    nmv_p = mem_kv[:, :, CROSS_H * CROSS_DH:].reshape(1, bsz, n_mem, CROSS_H, CROSS_DH)
    ns_p = _unpair_states(sp_p)[None]

    xs2 = x_sample.reshape(db * ds, d)
    ps2 = _norm_matmul(xs2, g_in, w_in_b, normalize=True, tm=db * ds, tn=IN_COLS // 3, vmem_mib=56)
    ps3 = ps2.reshape(db, ds, IN_COLS)
    hs_s = _rmsnorm_rows(x_sample[:, -1], g_in)
    shift_rows = _pad_rows(state_shift[l], 16)
    z_first = _norm_matmul(shift_rows, g_in, w_shift_b, normalize=False, tm=shift_rows.shape[0],
                           tn=w_shift_b.shape[1], vmem_mib=56)[:db]
    firsts = (z_first[:, None, :RWKV_W], z_first[:, None, RWKV_W:2 * RWKV_W],
              z_first[:, None, 2 * RWKV_W:3 * RWKV_W], z_first[:, None, 3 * RWKV_W:])

    q5 = ps3[:, :, OFF_DQ:OFF_DK].reshape(db, ds, DIFF_H, 2, DIFF_DH)
    zq = jnp.zeros_like(q5[..., 0, :])
    qm = jnp.stack([jnp.concatenate([q5[..., 0, :], zq], -1), jnp.concatenate([zq, q5[..., 1, :]], -1)], axis=1)
    qm = qm.transpose(0, 1, 3, 2, 4).reshape(db, 2 * DIFF_H * ds, DIFF_DV)
    k_new = ps3[:, :, OFF_DK:OFF_DV].reshape(db, ds * DIFF_H, DIFF_DV)
    v_new = ps3[:, :, OFF_DV:OFF_CQ].reshape(db, ds * DIFF_H, DIFF_DV)
    row_head = (np.arange(2 * DIFF_H * ds) // ds) % DIFF_H
    slope_rows = jnp.asarray(np.broadcast_to(slopes[row_head][:, None], (2 * DIFF_H * ds, LANES)).astype(np.float32))
    n_pool, page = cache_k.shape[1], cache_k.shape[2]
    ck = cache_k[l].reshape(n_pool, page * DIFF_H, DIFF_DV)
    cv = cache_v[l].reshape(n_pool, page * DIFF_H, DIFF_DV)
    ds_o = _paged_diff_attn(qm, ck, cv, k_new, v_new, page_table, slope_rows, lq, subln, n_pages=16,
                            lam_init=lam_init)
    ds_o = ds_o.reshape(db, DIFF_H, ds, DIFF_DV).transpose(0, 2, 1, 3).reshape(db * ds, DIFF_H * DIFF_DV)

    pad_t = 8
    ps3_pad = jnp.pad(ps3, ((0, 0), (0, pad_t - ds), (0, 0)))
    outs = _rwkv_prep(ps3_pad, firsts, mus, w0, w2p, a0, a2p, k_k, k_a, r_k, e1, e2, tm=pad_t)
    outs = [jnp.pad(o[:, :ds], ((0, 0), (0, CHUNK - ds), (0, 0))) for o in outs]
    rs_o, sp_s = _rwkv_scan(*outs, _pair_states(state_rwkv[l]), ln_g, ln_b, nb=2)
    rs_o = rs_o[:, :ds].reshape(db * ds, RWKV_W)

    cq_pad = jnp.pad(ps3[:, :, OFF_CQ:OFF_LORA], ((0, 0), (0, 16 - ds), (0, 0)))
    mk_s = cache_mem_k[l].reshape(db, n_mem, CROSS_H * CROSS_DH)
    mv_s = cache_mem_v[l].reshape(db, n_mem, CROSS_H * CROSS_DH)
    cs_o = _cross_attn(cq_pad, 0, mk_s, 0, mv_s, 0, tq=16)[:, :ds].reshape(db * ds, CROSS_H * CROSS_DH)
    y_s = _merge_out(xs2, ds_o, rs_o, cs_o, ps2, w_out_b, g_out, tm=db * ds).reshape(db, ds, d)
    nk_s = ps3[:, :, OFF_DK:OFF_DV].reshape(1, db, ds, DIFF_H, DIFF_DV)
    nv_s = ps3[:, :, OFF_DV:OFF_CQ].reshape(1, db, ds, DIFF_H, DIFF_DV)
    ns_s = _unpair_states(sp_s)[None]

    return (y_p, y_s, nk_p, nv_p, nmk_p, nmv_p, ns_p, hs_p[None], nk_s, nv_s, ns_s, hs_s[None])
```

```python
import functools
import math

import jax
import jax.numpy as jnp
import numpy as np
from jax import lax
from jax.experimental import pallas as pl
from jax.experimental.pallas import tpu as pltpu

F32 = jnp.float32
BF16 = jnp.bfloat16

LANES = 128
MIB = 1024 * 1024

DIFF_H = 4
DIFF_DV = 128
DIFF_DH = 64
RWKV_N = 64
RWKV_H = 16
RWKV_W = RWKV_H * RWKV_N
CROSS_H = 4
CROSS_DH = 128
LORA = 64
RWKV_GN_EPS = 64e-5
CHUNK = 64
NEG = -1e30

OFF_GATE = 0
OFF_R = 2048
OFF_K = 3072
OFF_V = 4096
OFF_DQ = 5120
OFF_DK = 5632
OFF_DV = 6144
OFF_CQ = 6656
OFF_LORA = 7168
IN_COLS = 7296


def _params(sem, vmem_mib):
    return pltpu.CompilerParams(dimension_semantics=sem, vmem_limit_bytes=vmem_mib * MIB)


def _split2(x):
    hi = x.astype(BF16)
    lo = (x - hi.astype(F32)).astype(BF16)
    return hi, lo


def _dot(a, b, dims=(((1,), (0,)), ((), ()))):
    return lax.dot_general(a, b, dims, preferred_element_type=F32)


NN = (((1,), (0,)), ((), ()))
NT = (((1,), (1,)), ((), ()))
TN = (((0,), (0,)), ((), ()))


def _dot3(a, b, dims=NN):
    ah, al = _split2(a)
    bh, bl = _split2(b)
    return _dot(ah, bh, dims) + (_dot(ah, bl, dims) + _dot(al, bh, dims))


def _dot3s(a, b, dims=NN, passes=3):
    ah, al = a
    bh, bl = b
    out = _dot(ah, bh, dims)
    if passes >= 2:
        out = out + _dot(al, bh, dims)
    if passes >= 3:
        out = out + _dot(ah, bl, dims)
    return out


SCAN_PASSES = dict(intra=1, state_read=1, av=1, solve=1, out=1, update=1)


def _dot_sel(x, sel):
    h, l = _split2(x)
    return _dot(h, sel) + _dot(l, sel)


def _norm_matmul_kernel(x_ref, g_ref, w_ref, o_ref, h_ref, *, n_norm, eps):
    @pl.when(pl.program_id(1) == 0)
    def _():
        x = x_ref[:n_norm, :]
        h_ref[:n_norm, :] = (x * lax.rsqrt(jnp.mean(x * x, axis=-1, keepdims=True) + eps) * g_ref[...]).astype(BF16)
        if n_norm < x_ref.shape[0]:
            h_ref[n_norm:, :] = x_ref[n_norm:, :].astype(BF16)

    o_ref[...] = jnp.dot(h_ref[...], w_ref[...], preferred_element_type=F32)


def _norm_matmul(x, g, w, *, tm, tn, vmem_mib, n_norm=None):
    m, d = x.shape
    n = w.shape[1]
    assert m % tm == 0 and n % tn == 0 and (n_norm is None or m == tm)
    return pl.pallas_call(
        functools.partial(_norm_matmul_kernel, n_norm=tm if n_norm is None else n_norm, eps=1e-6),
        grid=(m // tm, n // tn),
        in_specs=[pl.BlockSpec((tm, d), lambda i, j: (i, 0)),
                  pl.BlockSpec((1, d), lambda i, j: (0, 0)),
                  pl.BlockSpec((d, tn), lambda i, j: (0, j))],
        out_specs=pl.BlockSpec((tm, tn), lambda i, j: (i, j)),
        out_shape=jax.ShapeDtypeStruct((m, n), F32),
        scratch_shapes=[pltpu.VMEM((tm, d), BF16)],
        compiler_params=_params(("arbitrary", "arbitrary"), vmem_mib),
        name="norm_matmul",
    )(x, g, w)


def _rmsnorm_kernel(x_ref, g_ref, o_ref, *, eps):
    x = x_ref[...]
    o_ref[...] = x * lax.rsqrt(jnp.mean(x * x, axis=-1, keepdims=True) + eps) * g_ref[...]


def _pad_rows(x, mult):
    pad = -x.shape[0] % mult
    return jnp.pad(x, ((0, pad), (0, 0))) if pad else x


def _rmsnorm_rows(x, g):
    n = x.shape[0]
    xp = _pad_rows(x, 8)
    return pl.pallas_call(
        functools.partial(_rmsnorm_kernel, eps=1e-6),
        out_shape=jax.ShapeDtypeStruct(xp.shape, F32),
        name="rmsnorm_rows",
    )(xp, g)[:n]


def _diff_lambda(lq, lam_init):
    t1 = jnp.sum(lq[0:1] * lq[1:2], axis=-1, keepdims=True)
    t2 = jnp.sum(lq[2:3] * lq[3:4], axis=-1, keepdims=True)
    return jnp.exp(t1) - jnp.exp(t2) + lam_init


def _diff_finish(acc, l, lq, subln, rows, lam_init):
    o12 = acc / l
    o = o12[:rows] - _diff_lambda(lq, lam_init) * o12[rows:]
    o = o * lax.rsqrt(jnp.mean(o * o, axis=-1, keepdims=True) + 1e-5) * subln
    return o * (1.0 - lam_init)


LOG2E = math.log2(math.e)
POS_RADIX = 16
N_SLOPE_PIECES = 3


def _alibi_lanes(slopes, tq):
    assert tq <= POS_RADIX * 256
    c = np.asarray(slopes, np.float64) * LOG2E
    pieces, rest = [], c
    for _ in range(N_SLOPE_PIECES):
        piece = rest.astype(np.float32).astype(BF16).astype(np.float64)
        pieces.append(piece)
        rest = rest - piece
    q_lanes = np.zeros((len(c), 1, LANES), np.float32)
    k_lanes = np.zeros((tq, LANES), np.float32)
    pos = np.arange(tq)
    for n, piece in enumerate(pieces):
        q_lanes[:, 0, 2 * n] = POS_RADIX * piece
        q_lanes[:, 0, 2 * n + 1] = piece
        k_lanes[:, 2 * n] = pos // POS_RADIX
        k_lanes[:, 2 * n + 1] = pos % POS_RADIX
    c_sum = np.broadcast_to(sum(pieces)[:, None, None], (len(c), 1, LANES)).astype(np.float32)
    return jnp.asarray(q_lanes), jnp.asarray(k_lanes.astype(BF16)), jnp.asarray(c_sum)


def _stack_maps(q, q_lanes):
    q = q * (DIFF_DH ** -0.5 * LOG2E)
    lane = lax.broadcasted_iota(jnp.int32, q.shape, 1)
    q1 = jnp.where(lane < DIFF_DH, q, 0.0)
    q2 = jnp.where(lane >= DIFF_DH, q, 0.0)
    qq = jnp.concatenate([q1, q2], axis=0)
    return jnp.concatenate([qq, jnp.broadcast_to(q_lanes, qq.shape)], axis=1).astype(BF16)


def _diff_attn_kernel(qi_ref, kj_ref, q_ref, k_ref, v_ref, qlane_ref, klane_ref, slope_ref, lq_ref, subln_ref,
                      o_ref, ko_ref, vo_ref, qq_ref, m_ref, l_ref, acc_ref, *, tq, qw, lam_init):
    t = pl.program_id(2)
    i = qi_ref[t]
    j = kj_ref[t]

    @pl.when(j == 0)
    def _():
        qq_ref[...] = _stack_maps(q_ref[0], qlane_ref[0])
        m_ref[...] = jnp.full(m_ref.shape, NEG, F32)
        l_ref[...] = jnp.zeros(l_ref.shape, F32)
        acc_ref[...] = jnp.zeros(acc_ref.shape, F32)

    kb = jnp.concatenate([k_ref[0].astype(BF16), klane_ref[...]], axis=1)
    vb = v_ref[0].astype(BF16)
    block_off = slope_ref[0][:, :1] * jnp.full((1, 1), (j - i) * tq, jnp.int32).astype(F32)

    def accumulate(diagonal):
        starts = list(range(0, 2 * tq, qw))
        tiles = [pl.ds(c0, qw) for c0 in starts]
        n_keys = [c0 % tq + qw if diagonal else tq for c0 in starts]
        scores = []
        for c0, cols, nk in zip(starts, tiles, n_keys):
            s = _dot(kb[:nk], qq_ref[cols, :], NT)
            if diagonal:
                qry = c0 % tq + lax.broadcasted_iota(jnp.int32, s.shape, 1)
                s = jnp.where(lax.broadcasted_iota(jnp.int32, s.shape, 0) <= qry, s, NEG)
            scores.append(s)
        for s, cols, nk in zip(scores, tiles, n_keys):
            m_prev = m_ref[:, cols] - block_off
            m_new = jnp.maximum(m_prev, jnp.max(s, axis=0, keepdims=True))
            alpha = jnp.exp2(m_prev - m_new)
            p = jnp.exp2(s - m_new)
            l_ref[:, cols] = alpha * l_ref[:, cols] + jnp.sum(p, axis=0, keepdims=True)
            acc_ref[:, cols] = alpha * acc_ref[:, cols] + _dot(vb[:nk], p.astype(BF16), TN)
            m_ref[:, cols] = m_new + block_off

    @pl.when(j < i)
    def _():
        accumulate(False)

    @pl.when(j == i)
    def _():
        accumulate(True)
        o12 = acc_ref[...] / l_ref[...]
        o = o12[:, :tq] - _diff_lambda(lq_ref[...], lam_init) * o12[:, tq:]
        o = o * lax.rsqrt(jnp.mean(o * o, axis=0, keepdims=True) + 1e-5) * subln_ref[...]
        o_ref[0] = (o * (1.0 - lam_init)).T
        ko_ref[0] = k_ref[0]
        vo_ref[0] = v_ref[0]


def _diff_attn(p3, slopes, lambda_qk, subln_col, *, tq, qw, lam_init):
    b, t, _ = p3.shape
    nq = t // tq
    q_lanes, k_lanes, c_sum = _alibi_lanes(slopes, tq)
    pairs = [(i, j) for i in range(nq) for j in range(i + 1)]
    qi = jnp.asarray(np.array([p[0] for p in pairs], np.int32))
    kj = jnp.asarray(np.array([p[1] for p in pairs], np.int32))
    qb, kb, vb = OFF_DQ // DIFF_DV, OFF_DK // DIFF_DV, OFF_DV // DIFF_DV
    grid_spec = pltpu.PrefetchScalarGridSpec(
        num_scalar_prefetch=2,
        grid=(b, DIFF_H, len(pairs)),
        in_specs=[pl.BlockSpec((1, tq, DIFF_DV), lambda bb, h, s, qi, kj: (bb, qi[s], qb + h)),
                  pl.BlockSpec((1, tq, DIFF_DV), lambda bb, h, s, qi, kj: (bb, kj[s], kb + h)),
                  pl.BlockSpec((1, tq, DIFF_DV), lambda bb, h, s, qi, kj: (bb, kj[s], vb + h)),
                  pl.BlockSpec((1, 1, LANES), lambda bb, h, s, qi, kj: (h, 0, 0)),
                  pl.BlockSpec((tq, LANES), lambda bb, h, s, qi, kj: (0, 0)),
                  pl.BlockSpec((1, 1, LANES), lambda bb, h, s, qi, kj: (h, 0, 0)),
                  pl.BlockSpec((4, DIFF_DH), lambda bb, h, s, qi, kj: (0, 0)),
                  pl.BlockSpec((DIFF_DV, 1), lambda bb, h, s, qi, kj: (0, 0))],
        out_specs=[pl.BlockSpec((1, tq, DIFF_DV), lambda bb, h, s, qi, kj: (bb, qi[s], h))] * 3,
        scratch_shapes=[pltpu.VMEM((2 * tq, 2 * DIFF_DV), BF16), pltpu.VMEM((1, 2 * tq), F32),
                        pltpu.VMEM((1, 2 * tq), F32), pltpu.VMEM((DIFF_DV, 2 * tq), F32)],
    )
    return pl.pallas_call(
        functools.partial(_diff_attn_kernel, tq=tq, qw=qw, lam_init=lam_init),
        grid_spec=grid_spec,
        out_shape=[jax.ShapeDtypeStruct((b, t, DIFF_H * DIFF_DV), F32)] * 3,
        compiler_params=_params(("arbitrary", "arbitrary", "arbitrary"), 40),
        name="diff_attn",
    )(qi, kj, p3, p3, p3, q_lanes, k_lanes, c_sum, lambda_qk, subln_col)


def _paged_diff_attn_kernel(pt_ref, q_ref, *refs, n_pages, page, n_new, past_len, lam_init):
    k_refs = refs[:n_pages]
    v_refs = refs[n_pages:2 * n_pages]
    kn_ref, vn_ref, slope_ref, lq_ref, subln_ref, o_ref, m_ref, l_ref, acc_ref = refs[2 * n_pages:]
    s_id = pl.program_id(1)
    rows = 2 * DIFF_H * n_new

    @pl.when(s_id == 0)
    def _():
        m_ref[...] = jnp.full(m_ref.shape, NEG, F32)
        l_ref[...] = jnp.zeros(l_ref.shape, F32)
        acc_ref[...] = jnp.zeros(acc_ref.shape, F32)

    qb = (q_ref[0] * (DIFF_DH ** -0.5)).astype(BF16)
    slope = slope_ref[...][:, :1]
    log_new, log_h = n_new.bit_length() - 1, DIFF_H.bit_length() - 1

    def coords(ncol):
        row = lax.broadcasted_iota(jnp.int32, (rows, ncol), 0)
        col = lax.broadcasted_iota(jnp.int32, (rows, ncol), 1)
        row_h = lax.shift_right_logical(row, log_new) & (DIFF_H - 1)
        return row & (n_new - 1), (col & (DIFF_H - 1)) == row_h, lax.shift_right_logical(col, log_h)

    def update(ss, vbs, carry):
        m_prev, l_prev, acc_prev = carry
        m_new = m_prev
        for s in ss:
            m_new = jnp.maximum(m_new, jnp.max(s, axis=-1, keepdims=True))
        alpha = jnp.exp(m_prev - m_new)
        l_new = alpha * l_prev
        acc_new = alpha * acc_prev
        for s, vb in zip(ss, vbs):
            p = jnp.exp(s - m_new)
            l_new = l_new + jnp.sum(p, axis=-1, keepdims=True)
            acc_new = acc_new + _dot(p.astype(BF16), vb)
        return m_new, l_new, acc_new

    _, same_head, tok = coords(page * DIFF_H)
    ss, vbs = [], []
    for i in range(n_pages):
        kb = k_refs[i][0].astype(BF16)
        vbs.append(v_refs[i][0].astype(BF16))
        k_pos = ((s_id * n_pages + i) * page - past_len + tok).astype(F32)
        ss.append(jnp.where(same_head, _dot(qb, kb, NT) + slope * k_pos, NEG))
    m_ref[...], l_ref[...], acc_ref[...] = update(ss, vbs, (m_ref[...], l_ref[...], acc_ref[...]))

    @pl.when(s_id == pl.num_programs(1) - 1)
    def _():
        kb = kn_ref[0].astype(BF16)
        vb = vn_ref[0].astype(BF16)
        row_tn, same_head_n, tok_n = coords(n_new * DIFF_H)
        k_pos = tok_n.astype(F32)
        s = jnp.where(same_head_n & (tok_n <= row_tn), _dot(qb, kb, NT) + slope * k_pos, NEG)
        _, l, acc = update([s], [vb], (m_ref[...], l_ref[...], acc_ref[...]))
        o_ref[0] = _diff_finish(acc, l, lq_ref[...], subln_ref[...], rows // 2, lam_init)


def _paged_diff_attn(qm, cache_k, cache_v, k_new, v_new, page_table, slope_rows, lambda_qk, subln, *,
                     n_pages, lam_init):
    db, rows, _ = qm.shape
    n_pool, page = cache_k.shape[0], cache_k.shape[1] // DIFF_H
    n_tab = page_table.shape[1]
    n_new = rows // (2 * DIFF_H)
    assert n_tab % n_pages == 0 and n_new & (n_new - 1) == 0 and DIFF_H & (DIFF_H - 1) == 0

    def page_spec(i):
        return pl.BlockSpec((1, page * DIFF_H, DIFF_DV), lambda b, s, pt: (pt[b, s * n_pages + i], 0, 0))

    grid_spec = pltpu.PrefetchScalarGridSpec(
        num_scalar_prefetch=1,
        grid=(db, n_tab // n_pages),
        in_specs=([pl.BlockSpec((1, rows, DIFF_DV), lambda b, s, pt: (b, 0, 0))]
                  + [page_spec(i) for i in range(n_pages)] * 2
                  + [pl.BlockSpec((1, n_new * DIFF_H, DIFF_DV), lambda b, s, pt: (b, 0, 0)),
                     pl.BlockSpec((1, n_new * DIFF_H, DIFF_DV), lambda b, s, pt: (b, 0, 0)),
                     pl.BlockSpec((rows, LANES), lambda b, s, pt: (0, 0)),
                     pl.BlockSpec((4, DIFF_DH), lambda b, s, pt: (0, 0)),
                     pl.BlockSpec((1, DIFF_DV), lambda b, s, pt: (0, 0))]),
        out_specs=pl.BlockSpec((1, rows // 2, DIFF_DV), lambda b, s, pt: (b, 0, 0)),
        scratch_shapes=[pltpu.VMEM((rows, 1), F32), pltpu.VMEM((rows, 1), F32), pltpu.VMEM((rows, DIFF_DV), F32)],
    )
    return pl.pallas_call(
        functools.partial(_paged_diff_attn_kernel, n_pages=n_pages, page=page, n_new=n_new,
                          past_len=n_tab * page, lam_init=lam_init),
        grid_spec=grid_spec,
        out_shape=jax.ShapeDtypeStruct((db, rows // 2, DIFF_DV), F32),
        compiler_params=_params(("arbitrary", "arbitrary"), 32),
        name="paged_diff_attn",
    )(page_table, qm, *([cache_k] * n_pages), *([cache_v] * n_pages), k_new, v_new, slope_rows, lambda_qk, subln)


def _cross_attn_kernel(q_ref, k_ref, v_ref, o_ref, *, scale):
    q = q_ref[0].astype(BF16)
    k = k_ref[0].astype(BF16)
    v = v_ref[0].astype(BF16)
    s = _dot(q, k, NT) * scale
    p = jnp.exp(s - jnp.max(s, axis=-1, keepdims=True))
    l = jnp.sum(p, axis=-1, keepdims=True)
    o_ref[0] = _dot(p.astype(BF16), v) / l


def _cross_attn(q_arr, q_off, k_arr, k_off, v_arr, v_off, *, tq):
    b, t, _ = q_arr.shape
    n_mem = k_arr.shape[1]
    qb, kb, vb = q_off // CROSS_DH, k_off // CROSS_DH, v_off // CROSS_DH
    return pl.pallas_call(
        functools.partial(_cross_attn_kernel, scale=CROSS_DH ** -0.5),
        grid=(b, CROSS_H, t // tq),
        in_specs=[pl.BlockSpec((1, tq, CROSS_DH), lambda bb, h, i: (bb, i, qb + h)),
                  pl.BlockSpec((1, n_mem, CROSS_DH), lambda bb, h, i: (bb, 0, kb + h)),
                  pl.BlockSpec((1, n_mem, CROSS_DH), lambda bb, h, i: (bb, 0, vb + h))],
        out_specs=pl.BlockSpec((1, tq, CROSS_DH), lambda bb, h, i: (bb, i, h)),
        out_shape=jax.ShapeDtypeStruct((b, t, CROSS_H * CROSS_DH), F32),
        compiler_params=_params(("arbitrary", "arbitrary", "arbitrary"), 32),
        name="cross_attn",
    )(q_arr, k_arr, v_arr)


def _rwkv_step(r_ref, k_ref, v_ref, lo_ref, fr_ref, fk_ref, fv_ref, fl_ref,
               mur_ref, muk_ref, muv_ref, mul_ref, w0_ref, w2_ref, a0_ref, a2_ref, kk_ref, ka_ref, rk_ref,
               s0_ref, g_ref, bb_ref, y_ref, sout_ref,
               lhs_s, rhs_s, vm_s, gend_s, bonus_s, s_ref, cr_ref, ck_ref, cv_ref, cl_ref,
               *, chunk, n_pairs, nb, n_chunks, valid, slot_w, slot_r):
    c = pl.program_id(1)
    c2 = 2 * chunk

    rid = lax.broadcasted_iota(jnp.int32, (c2, c2), 0)
    cid = lax.broadcasted_iota(jnp.int32, (c2, c2), 1)
    strict = rid > cid
    incl2 = jnp.concatenate([rid >= cid, rid >= cid], axis=1)
    tr = lax.broadcasted_iota(jnp.int32, (chunk, chunk), 0)
    tc = lax.broadcasted_iota(jnp.int32, (chunk, chunk), 1)
    tri = (tr >= tc).astype(BF16)
    srow = lax.broadcasted_iota(jnp.int32, (c2, LANES), 0)
    slane = lax.broadcasted_iota(jnp.int32, (c2, LANES), 1)
    own = (srow < chunk) == (slane < RWKV_N)
    low_half = lax.broadcasted_iota(jnp.int32, (chunk, LANES), 1) < RWKV_N

    def stack(x):
        return jnp.where(own, jnp.concatenate([x, x], axis=0), 0.0).astype(BF16)

    def head_sum(x):
        s_lo = jnp.sum(jnp.where(low_half, x, 0.0), axis=-1, keepdims=True)
        s_hi = jnp.sum(jnp.where(low_half, 0.0, x), axis=-1, keepdims=True)
        return jnp.where(low_half, s_lo, s_hi)

    row_id = lax.broadcasted_iota(jnp.int32, (chunk, 1), 0)

    def prepare():
        for bi in range(nb):
            def shift_mix(x_ref, carry_ref, mu_ref):
                x = x_ref[bi]
                prev = pltpu.roll(x, 1, axis=0)
                prev = jnp.where(lax.broadcasted_iota(jnp.int32, x.shape, 0) == 0, carry_ref[bi], prev)
                carry_ref[bi] = x[chunk - 1:chunk, :]
                return x + mu_ref[...] * (prev - x)

            r = shift_mix(r_ref, cr_ref, mur_ref)
            k = shift_mix(k_ref, ck_ref, muk_ref)
            v = shift_mix(v_ref, cv_ref, muv_ref)
            lo = shift_mix(lo_ref, cl_ref, mul_ref)
            yield
            xw = -(w0_ref[...] + _dot(jnp.tanh(lo).astype(BF16), w2_ref[...]))
            softplus = jnp.maximum(xw, 0.0) + jnp.log(1.0 + jnp.exp(-jnp.abs(xw)))
            lw = -jnp.exp(-softplus - 0.5)
            a = jax.nn.sigmoid(a0_ref[...] + _dot(lo.astype(BF16), a2_ref[...]))
            kkr = k * kk_ref[...]
            k2 = k * (1.0 + (a - 1.0) * ka_ref[...])
            rb = r * k2 * rk_ref[...]
            if valid is not None:
                ok = jnp.minimum(c, n_chunks - 1) * chunk + row_id < valid
                lw, r, k2, v, kkr, rb = [jnp.where(ok, t, 0.0) for t in (lw, r, k2, v, kkr, rb)]
            yield
            for pi in range(n_pairs):
                cols = slice(pi * LANES, (pi + 1) * LANES)
                lw_p, kkr_p, v_p = lw[:, cols], kkr[:, cols], v[:, cols]
                kk = kkr_p / jnp.maximum(jnp.sqrt(head_sum(kkr_p * kkr_p)), 1e-12)
                h1 = lw_p.astype(BF16)
                r1 = lw_p - h1.astype(F32)
                h2 = r1.astype(BF16)
                h3 = (r1 - h2.astype(F32)).astype(BF16)
                cum = _dot(tri, h1) + (_dot(tri, h2) + _dot(tri, h3))
                g = jnp.exp(cum)
                gi = jnp.exp(-cum)
                gp = jnp.exp(cum - lw_p)
                lhs_s[slot_w, bi, pi] = jnp.concatenate([stack(-kk * gp), stack(r[:, cols] * g)], axis=0)
                rhs_s[slot_w, bi, pi] = jnp.concatenate([stack(kk * a[:, cols] * gi), stack(k2[:, cols] * gi)], axis=0)
                vm_s[slot_w, bi, pi] = stack(v_p)
                gend_s[slot_w, bi, :, cols] = g[chunk - 1:chunk, :]
                bonus_s[slot_w, bi, :, cols] = head_sum(rb[:, cols]) * v_p
                yield

    pieces = prepare()
    n_pieces = nb * (n_pairs + 2)
    n_gaps = chunk.bit_length() + 3

    def interleave(gap):
        for _ in range((gap + 1) * n_pieces // n_gaps - gap * n_pieces // n_gaps):
            next(pieces, None)

    units = [(bi, pi) for bi in range(nb) for pi in range(n_pairs)]
    lhs = [lhs_s[slot_r, bi, pi] for bi, pi in units]
    rhs = [rhs_s[slot_r, bi, pi] for bi, pi in units]
    vm = [vm_s[slot_r, bi, pi] for bi, pi in units]
    s_prev = [s_ref[bi, pi] for bi, pi in units]
    n_units = range(len(units))
    p1 = [_dot(lhs[u], rhs[u], NT) for u in n_units]
    interleave(0)
    p2 = [_dot(lhs[u], s_prev[u].astype(BF16), NT) for u in n_units]
    interleave(1)
    nm = [jnp.where(strict, p1[u][:c2, :c2], 0.0).astype(BF16) for u in n_units]
    aak = [jnp.where(strict, p1[u][:c2, c2:], 0.0).astype(BF16) for u in n_units]
    m2 = [jnp.where(incl2, p1[u][c2:, :], 0.0).astype(BF16) for u in n_units]
    uu = [p2[u][:c2] + _dot(aak[u], vm[u]) for u in n_units]
    interleave(2)
    n, gap = 1, 3
    while 2 * n < chunk:
        y = [_dot(nm[u], jnp.concatenate([nm[u], uu[u].astype(BF16)], axis=1)) for u in n_units]
        uu = [uu[u] + y[u][:, c2:] for u in n_units]
        nm = [y[u][:, :c2].astype(BF16) for u in n_units]
        interleave(gap)
        n, gap = 2 * n, gap + 1
    uu = [uu[u] + _dot(nm[u], uu[u].astype(BF16)) for u in n_units]
    interleave(gap)
    uv = [jnp.concatenate([uu[u].astype(BF16), vm[u]], axis=0) for u in n_units]
    o = [p2[u][c2:] + _dot(m2[u], uv[u]) for u in n_units]
    interleave(gap + 1)
    for u, (bi, pi) in zip(n_units, units):
        cols = pl.ds(pi * LANES, LANES)
        s_ref[bi, pi] = (s_prev[u] + _dot(uv[u], rhs[u], TN)) * gend_s[slot_r, bi, :, cols]
    for _ in pieces:
        pass
    for u, (bi, pi) in zip(n_units, units):
        cols = pl.ds(pi * LANES, LANES)
        mu = jnp.sum(o[u], axis=-1, keepdims=True) * (1.0 / RWKV_N)
        d = jnp.where(own, o[u] - mu, 0.0)
        var = jnp.sum(d * d, axis=-1, keepdims=True) * (1.0 / RWKV_N)
        yn = d * lax.rsqrt(var + RWKV_GN_EPS)
        y_ref[bi, :, cols] = ((yn[:chunk] + yn[chunk:]) * g_ref[:, cols] + bb_ref[:, cols]
                              + bonus_s[slot_r, bi, :, cols])


def _rwkv_kernel(*refs, n_chunks, **static):
    (fr_ref, fk_ref, fv_ref, fl_ref), s0_ref, sout_ref = refs[4:8], refs[19], refs[23]
    lhs_s, rhs_s, vm_s, gend_s, bonus_s, s_ref, cr_ref, ck_ref, cv_ref, cl_ref = refs[24:]
    c = pl.program_id(1)

    @pl.when(c == 0)
    def _():
        cr_ref[...] = fr_ref[...]
        ck_ref[...] = fk_ref[...]
        cv_ref[...] = fv_ref[...]
        cl_ref[...] = fl_ref[...]
        s_ref[...] = s0_ref[...]
        lhs_s[1] = jnp.zeros(lhs_s.shape[1:], BF16)
        rhs_s[1] = jnp.zeros(rhs_s.shape[1:], BF16)
        vm_s[1] = jnp.zeros(vm_s.shape[1:], BF16)
        gend_s[1] = jnp.ones(gend_s.shape[1:], F32)
        bonus_s[1] = jnp.zeros(bonus_s.shape[1:], F32)

    @pl.when(lax.rem(c, 2) == 0)
    def _():
        _rwkv_step(*refs, n_chunks=n_chunks, slot_w=0, slot_r=1, **static)

    @pl.when(lax.rem(c, 2) == 1)
    def _():
        _rwkv_step(*refs, n_chunks=n_chunks, slot_w=1, slot_r=0, **static)

    @pl.when(c == n_chunks)
    def _():
        sout_ref[...] = s_ref[...]


def _rwkv(p3, firsts, mus, w0, w2b, a0, a2b, k_k, k_a, r_k, s0_pairs, ln_g, ln_b, *, nb, valid=None):
    bsz, t, _ = p3.shape
    n_pairs = RWKV_H // 2
    n_chunks = t // CHUNK
    assert bsz % nb == 0 and t % CHUNK == 0
    cur = lambda c: jnp.minimum(c, n_chunks - 1)
    wide = lambda blk: pl.BlockSpec((nb, CHUNK, RWKV_W), lambda bb, c: (bb, cur(c), blk))
    vec = lambda n: pl.BlockSpec((1, n), lambda bb, c: (0, 0))
    first = lambda n: pl.BlockSpec((nb, 1, n), lambda bb, c: (bb, 0, 0))
    st = pl.BlockSpec((nb, n_pairs, LANES, LANES), lambda bb, c: (bb, 0, 0, 0))
    in_specs = [wide(OFF_R // RWKV_W), wide(OFF_K // RWKV_W), wide(OFF_V // RWKV_W),
                pl.BlockSpec((nb, CHUNK, LANES), lambda bb, c: (bb, cur(c), OFF_LORA // LANES)),
                first(RWKV_W), first(RWKV_W), first(RWKV_W), first(LANES),
                vec(RWKV_W), vec(RWKV_W), vec(RWKV_W), vec(LANES),
                vec(RWKV_W), pl.BlockSpec((LANES, RWKV_W), lambda bb, c: (0, 0)),
                vec(RWKV_W), pl.BlockSpec((LANES, RWKV_W), lambda bb, c: (0, 0)),
                vec(RWKV_W), vec(RWKV_W), vec(RWKV_W),
                st, vec(RWKV_W), vec(RWKV_W)]
    out_specs = [pl.BlockSpec((nb, CHUNK, RWKV_W), lambda bb, c: (bb, jnp.maximum(c - 1, 0), 0)), st]
    scratch = [pltpu.VMEM((2, nb, n_pairs, 4 * CHUNK, LANES), BF16), pltpu.VMEM((2, nb, n_pairs, 4 * CHUNK, LANES), BF16),
               pltpu.VMEM((2, nb, n_pairs, 2 * CHUNK, LANES), BF16), pltpu.VMEM((2, nb, 1, RWKV_W), F32),
               pltpu.VMEM((2, nb, CHUNK, RWKV_W), F32), pltpu.VMEM((nb, n_pairs, LANES, LANES), F32),
               pltpu.VMEM((nb, 1, RWKV_W), F32), pltpu.VMEM((nb, 1, RWKV_W), F32),
               pltpu.VMEM((nb, 1, RWKV_W), F32), pltpu.VMEM((nb, 1, LANES), F32)]
    return pl.pallas_call(
        functools.partial(_rwkv_kernel, chunk=CHUNK, n_pairs=n_pairs, nb=nb, n_chunks=n_chunks, valid=valid),
        grid=(bsz // nb, n_chunks + 1),
        in_specs=in_specs,
        out_specs=out_specs,
        out_shape=[jax.ShapeDtypeStruct((bsz, t, RWKV_W), F32),
                   jax.ShapeDtypeStruct((bsz, n_pairs, LANES, LANES), F32)],
        scratch_shapes=scratch,
        compiler_params=_params(("arbitrary", "arbitrary"), 48),
        name="rwkv",
    )(p3, p3, p3, p3, *firsts, *mus, w0, w2b, a0, a2b, k_k, k_a, r_k, s0_pairs, ln_g, ln_b)


def _rwkv_prep_kernel(r_ref, k_ref, v_ref, lo_ref, fr_ref, fk_ref, fv_ref, fl_ref,
                      mur_ref, muk_ref, muv_ref, mul_ref, w0_ref, w2_ref, a0_ref, a2_ref,
                      kk_ref, ka_ref, rk_ref, e1_ref, e2_ref,
                      ro_ref, lwo_ref, ko_ref, vo_ref, ao_ref, bo_ref, bonus_ref,
                      cr_ref, ck_ref, cv_ref, cl_ref):
    @pl.when(pl.program_id(1) == 0)
    def _():
        cr_ref[...] = fr_ref[0]
        ck_ref[...] = fk_ref[0]
        cv_ref[...] = fv_ref[0]
        cl_ref[...] = fl_ref[0]

    def shift_mix(x_ref, carry_ref, mu_ref):
        x = x_ref[0]
        rows = x.shape[0]
        prev = pltpu.roll(x, 1, axis=0)
        rid = lax.broadcasted_iota(jnp.int32, x.shape, 0)
        prev = jnp.where(rid == 0, carry_ref[...], prev)
        carry_ref[...] = x[rows - 1:rows, :]
        return x + mu_ref[...] * (prev - x)

    r = shift_mix(r_ref, cr_ref, mur_ref)
    k = shift_mix(k_ref, ck_ref, muk_ref)
    v = shift_mix(v_ref, cv_ref, muv_ref)
    lo = shift_mix(lo_ref, cl_ref, mul_ref)

    xw = -(w0_ref[...] + _dot3(jnp.tanh(lo), w2_ref[...]))
    softplus = jnp.maximum(xw, 0.0) + jnp.log(1.0 + jnp.exp(-jnp.abs(xw)))
    w = -softplus - 0.5
    lwo_ref[0] = -jnp.exp(w)
    a = jax.nn.sigmoid(a0_ref[...] + _dot3(lo, a2_ref[...]))

    def head_sum(x):
        return _dot_sel(_dot_sel(x, e1_ref[...]), e2_ref[...])

    kkr = k * kk_ref[...]
    kk = kkr / jnp.maximum(jnp.sqrt(head_sum(kkr * kkr)), 1e-12)
    k2 = k * (1.0 + (a - 1.0) * ka_ref[...])
    ro_ref[0] = r
    ko_ref[0] = k2
    vo_ref[0] = v
    ao_ref[0] = -kk
    bo_ref[0] = kk * a
    bonus_ref[0] = head_sum(r * k2 * rk_ref[...]) * v


def _rwkv_prep(p3, firsts, mus, w0, w2p, a0, a2p, k_k, k_a, r_k, e1, e2, *, tm):
    b, t, _ = p3.shape
    nt = t // tm
    wide = lambda blk: pl.BlockSpec((1, tm, RWKV_W), lambda bb, i: (bb, i, blk))
    vec = lambda n: pl.BlockSpec((1, n), lambda bb, i: (0, 0))
    first = lambda n: pl.BlockSpec((1, 1, n), lambda bb, i: (bb, 0, 0))
    out_wide = pl.BlockSpec((1, tm, RWKV_W), lambda bb, i: (bb, i, 0))
    in_specs = [wide(OFF_R // RWKV_W), wide(OFF_K // RWKV_W), wide(OFF_V // RWKV_W),
                pl.BlockSpec((1, tm, LANES), lambda bb, i: (bb, i, OFF_LORA // LANES)),
                first(RWKV_W), first(RWKV_W), first(RWKV_W), first(LANES),
                vec(RWKV_W), vec(RWKV_W), vec(RWKV_W), vec(LANES),
                vec(RWKV_W), pl.BlockSpec((LANES, RWKV_W), lambda bb, i: (0, 0)),
                vec(RWKV_W), pl.BlockSpec((LANES, RWKV_W), lambda bb, i: (0, 0)),
                vec(RWKV_W), vec(RWKV_W), vec(RWKV_W),
                pl.BlockSpec((RWKV_W, LANES), lambda bb, i: (0, 0)),
                pl.BlockSpec((LANES, RWKV_W), lambda bb, i: (0, 0))]
    shape = jax.ShapeDtypeStruct((b, t, RWKV_W), F32)
    return pl.pallas_call(
        _rwkv_prep_kernel,
        grid=(b, nt),
        in_specs=in_specs,
        out_specs=[out_wide] * 7,
        out_shape=[shape] * 7,
        scratch_shapes=[pltpu.VMEM((1, RWKV_W), F32), pltpu.VMEM((1, RWKV_W), F32),
                        pltpu.VMEM((1, RWKV_W), F32), pltpu.VMEM((1, LANES), F32)],
        compiler_params=_params(("arbitrary", "arbitrary"), 48),
        name="rwkv_prep",
    )(p3, p3, p3, p3, *firsts, *mus, w0, w2p, a0, a2p, k_k, k_a, r_k, e1, e2)


def _rwkv_scan_kernel(r_ref, lw_ref, k_ref, v_ref, a_ref, b_ref, bonus_ref, s0_ref, g_ref, bb_ref,
                      y_ref, sout_ref, s_ref, *, chunk, n_pairs, nb):
    c = pl.program_id(1)
    c2 = 2 * chunk

    @pl.when(c == 0)
    def _():
        s_ref[...] = s0_ref[...]

    rid = lax.broadcasted_iota(jnp.int32, (c2, c2), 0)
    cid = lax.broadcasted_iota(jnp.int32, (c2, c2), 1)
    strict = rid > cid
    incl2 = jnp.concatenate([rid >= cid, rid >= cid], axis=1)
    tr = lax.broadcasted_iota(jnp.int32, (chunk, chunk), 0)
    tc = lax.broadcasted_iota(jnp.int32, (chunk, chunk), 1)
    tri = (tr >= tc).astype(BF16)
    srow = lax.broadcasted_iota(jnp.int32, (c2, LANES), 0)
    slane = lax.broadcasted_iota(jnp.int32, (c2, LANES), 1)
    own = (srow < chunk) == (slane < RWKV_N)

    def stack(x):
        return jnp.where(own, jnp.concatenate([x, x], axis=0), 0.0)

    def cat2(x, y, axis):
        return tuple(jnp.concatenate([p, q], axis=axis) for p, q in zip(x, y))

    units = [(bi, pi) for bi in range(nb) for pi in range(n_pairs)]
    pairs = range(len(units))
    sls = [(bi, slice(None), pl.ds(pi * LANES, LANES)) for bi, pi in units]
    lhs, rhs, vm, g_end, s_prev = [], [], [], [], []
    for sl in sls:
        lw = lw_ref[sl]
        h1 = lw.astype(BF16)
        r1 = lw - h1.astype(F32)
        h2 = r1.astype(BF16)
        h3 = (r1 - h2.astype(F32)).astype(BF16)
        cum = _dot(tri, h1) + (_dot(tri, h2) + _dot(tri, h3))
        g = jnp.exp(cum)
        gi = jnp.exp(-cum)
        gp = jnp.exp(cum - lw)
        am = stack(a_ref[sl] * gp)
        rm = stack(r_ref[sl] * g)
        bm = stack(b_ref[sl] * gi)
        km = stack(k_ref[sl] * gi)
        lhs.append(_split2(jnp.concatenate([am, rm], axis=0)))
        rhs.append(_split2(jnp.concatenate([bm, km], axis=0)))
        vm.append(_split2(stack(v_ref[sl])))
        g_end.append(g[chunk - 1:chunk, :])
    for bi, pi in units:
        s_prev.append(s_ref[bi, pi])

    ps = SCAN_PASSES
    p1 = [_dot3s(lhs[p], rhs[p], NT, ps['intra']) for p in pairs]
    p2 = [_dot3s(lhs[p], _split2(s_prev[p]), NT, ps['state_read']) for p in pairs]
    nm = [_split2(jnp.where(strict, p1[p][:c2, :c2], 0.0)) for p in pairs]
    aak = [_split2(jnp.where(strict, p1[p][:c2, c2:], 0.0)) for p in pairs]
    m2 = [_split2(jnp.where(incl2, p1[p][c2:, :], 0.0)) for p in pairs]
    u = [p2[p][:c2] + _dot3s(aak[p], vm[p], NN, ps['av']) for p in pairs]
    n = 1
    while 2 * n < chunk:
        us = [_split2(u[p]) for p in pairs]
        y = [_dot3s(nm[p], cat2(nm[p], us[p], 1), NN, ps['solve']) for p in pairs]
        u = [u[p] + y[p][:, c2:] for p in pairs]
        nm = [_split2(y[p][:, :c2]) for p in pairs]
        n *= 2
    us = [_split2(u[p]) for p in pairs]
    u = [u[p] + _dot3s(nm[p], us[p], NN, ps['solve']) for p in pairs]
    uv = [cat2(_split2(u[p]), vm[p], 0) for p in pairs]
    o = [p2[p][c2:] + _dot3s(m2[p], uv[p], NN, ps['out']) for p in pairs]
    for p, (bi, pi) in zip(pairs, units):
        s_ref[bi, pi] = (s_prev[p] + _dot3s(uv[p], rhs[p], TN, ps['update'])) * g_end[p]

    for p, sl in zip(pairs, sls):
        mu = jnp.sum(o[p], axis=-1, keepdims=True) * (1.0 / RWKV_N)
        d = jnp.where(own, o[p] - mu, 0.0)
        var = jnp.sum(d * d, axis=-1, keepdims=True) * (1.0 / RWKV_N)
        yn = d * lax.rsqrt(var + RWKV_GN_EPS)
        y = yn[:chunk] + yn[chunk:]
        y_ref[sl] = y * g_ref[:, sl[2]] + bb_ref[:, sl[2]] + bonus_ref[sl]

    @pl.when(c == pl.num_programs(1) - 1)
    def _():
        sout_ref[...] = s_ref[...]


def _rwkv_scan(r, lw, k, v, a, b, bonus, s0_pairs, ln_g, ln_b, *, nb):
    bsz, t, _ = r.shape
    n_pairs = RWKV_H // 2
    assert bsz % nb == 0
    blk = pl.BlockSpec((nb, CHUNK, RWKV_W), lambda bb, c: (bb, c, 0))
    st = pl.BlockSpec((nb, n_pairs, LANES, LANES), lambda bb, c: (bb, 0, 0, 0))
    vec = pl.BlockSpec((1, RWKV_W), lambda bb, c: (0, 0))
    return pl.pallas_call(
        functools.partial(_rwkv_scan_kernel, chunk=CHUNK, n_pairs=n_pairs, nb=nb),
        grid=(bsz // nb, t // CHUNK),
        in_specs=[blk] * 7 + [st, vec, vec],
        out_specs=[blk, st],
        out_shape=[jax.ShapeDtypeStruct((bsz, t, RWKV_W), F32),
                   jax.ShapeDtypeStruct((bsz, n_pairs, LANES, LANES), F32)],
        scratch_shapes=[pltpu.VMEM((nb, n_pairs, LANES, LANES), F32)],
        compiler_params=_params(("arbitrary", "arbitrary"), 40),
        name="rwkv_scan",
    )(r, lw, k, v, a, b, bonus, s0_pairs, ln_g, ln_b)


def _pair_states(s):
    bsz = s.shape[0]
    s = s.reshape(bsz, RWKV_H // 2, 2, RWKV_N, RWKV_N)
    z = jnp.zeros_like(s[:, :, 0])
    top = jnp.concatenate([s[:, :, 0], z], axis=-1)
    bot = jnp.concatenate([z, s[:, :, 1]], axis=-1)
    return jnp.concatenate([top, bot], axis=-2)


def _unpair_states(sp):
    bsz = sp.shape[0]
    even = sp[:, :, :RWKV_N, :RWKV_N]
    odd = sp[:, :, RWKV_N:, RWKV_N:]
    return jnp.stack([even, odd], axis=2).reshape(bsz, RWKV_H, RWKV_N, RWKV_N)


def _merge_out_kernel(x_ref, d_ref, r_ref, c_ref, gate_ref, w_ref, g_ref, o_ref, *, eps):
    gate = gate_ref[...]
    sg = gate * jax.nn.sigmoid(gate)
    nd, nr = d_ref.shape[1], r_ref.shape[1]
    acc = _dot((d_ref[...] * sg[:, :nd]).astype(BF16), w_ref[0:nd, :])
    acc += _dot((r_ref[...] * sg[:, nd:nd + nr]).astype(BF16), w_ref[nd:nd + nr, :])
    acc += _dot((c_ref[...] * sg[:, nd + nr:]).astype(BF16), w_ref[nd + nr:, :])
    x = x_ref[...] + acc
    o_ref[...] = x * lax.rsqrt(jnp.mean(x * x, axis=-1, keepdims=True) + eps) * g_ref[...]


def _merge_out(x, d_o, r_o, c_o, p2, w_out, norm_out, *, tm):
    m, d = x.shape
    row = lambda n: pl.BlockSpec((tm, n), lambda i: (i, 0))
    return pl.pallas_call(
        functools.partial(_merge_out_kernel, eps=1e-6),
        grid=(m // tm,),
        in_specs=[row(d), row(d_o.shape[1]), row(r_o.shape[1]), row(c_o.shape[1]),
                  pl.BlockSpec((tm, d), lambda i: (i, OFF_GATE // d)),
                  pl.BlockSpec((d, d), lambda i: (0, 0)),
                  pl.BlockSpec((1, d), lambda i: (0, 0))],
        out_specs=row(d),
        out_shape=jax.ShapeDtypeStruct((m, d), F32),
        compiler_params=_params(("arbitrary",), 56),
        name="merge_out",
    )(x, d_o, r_o, c_o, p2, w_out, norm_out)


REORDER_COLS = 512


def _cast_kernel(offs_ref, src_ref, dst_ref):
    dst_ref[...] = src_ref[...].astype(BF16)


def _reorder_in_cols(w):
    d = w.shape[0]
    dq, z, cq, gate = 0, 1536, 1536 + 3200, 1536 + 3200 + 512
    runs = [(gate, IN_COLS - gate), (z, 3 * RWKV_W), (dq, z), (cq, gate - cq), (z + 3 * RWKV_W, 2 * LORA)]
    offs = []
    for start, width in runs:
        offs += [start + c for c in range(0, width, REORDER_COLS)]
    assert all(wd % REORDER_COLS == 0 for _, wd in runs[:-1]) and offs[-1] + REORDER_COLS <= IN_COLS
    grid_spec = pltpu.PrefetchScalarGridSpec(
        num_scalar_prefetch=1,
        grid=(len(offs),),
        in_specs=[pl.BlockSpec((pl.Element(d), pl.Element(REORDER_COLS)), lambda i, offs: (0, offs[i] * LANES))],
        out_specs=pl.BlockSpec((d, REORDER_COLS), lambda i, offs: (0, i)),
    )
    return pl.pallas_call(
        _cast_kernel,
        grid_spec=grid_spec,
        out_shape=jax.ShapeDtypeStruct((d, IN_COLS), BF16),
        compiler_params=_params(("arbitrary",), 24),
        name="reorder_cast",
    )(jnp.asarray(np.array(offs, np.int32) // LANES), w)


def kernel(x_prompt, x_sample, cache_k, cache_v, cache_mem_k, cache_mem_v, state_rwkv, state_shift, page_table,
           mem_prompt, norm_in, w_in, norm_mem, w_mem_kv, lambda_qk, diff_subln, rwkv_mu, rwkv_w0, rwkv_w2, rwkv_a0,
           rwkv_a2, rwkv_k_k, rwkv_k_a, rwkv_r_k, rwkv_ln_g, rwkv_ln_b, w_out, norm_out):
    bsz, seq, d = x_prompt.shape
    db, ds, _ = x_sample.shape
    depth = w_in.shape[0]
    assert depth == 1
    l = 0
    lam_init = 0.8 - 0.6 * math.exp(-0.3 * l)
    n_mem = mem_prompt.shape[1]

    w_in_b = _reorder_in_cols(w_in[l])
    w_mem_b = w_mem_kv[l].astype(BF16)
    w_out_b = w_out[l].astype(BF16)
    g_in = norm_in[l][None]
    g_mem = norm_mem[l][None]
    g_out = norm_out[None]
    mu = rwkv_mu[l]
    mus = (mu[None, :RWKV_W], mu[None, RWKV_W:2 * RWKV_W], mu[None, 2 * RWKV_W:3 * RWKV_W], mu[None, 3 * RWKV_W:])
    zl = jnp.zeros((LORA, RWKV_W), F32)
    w2b = jnp.concatenate([rwkv_w2[l], zl], axis=0).astype(BF16)
    a2b = jnp.concatenate([zl, rwkv_a2[l]], axis=0).astype(BF16)
    vec = lambda x: x.reshape(1, RWKV_W)
    w0, a0, k_k, k_a, r_k = vec(rwkv_w0[l]), vec(rwkv_a0[l]), vec(rwkv_k_k[l]), vec(rwkv_k_a[l]), vec(rwkv_r_k[l])
    ln_g, ln_b = vec(rwkv_ln_g[l]), vec(rwkv_ln_b[l])
    slopes = 2.0 ** (-8.0 * np.arange(1, DIFF_H + 1, dtype=np.float64) / DIFF_H)
    subln = diff_subln[l][None]
    lq = lambda_qk[l]

    xp2 = x_prompt.reshape(bsz * seq, d)
    p2 = _norm_matmul(xp2, g_in, w_in_b, tm=512, tn=IN_COLS // 3, vmem_mib=56)
    p3 = p2.reshape(bsz, seq, IN_COLS)
    hs_p = _rmsnorm_rows(x_prompt[:, -1], g_in)
    d_o, k_p, v_p = _diff_attn(p3, slopes, lq, subln.reshape(DIFF_DV, 1), tq=1024, qw=256, lam_init=lam_init)
    zeros_first = (jnp.zeros((bsz, 1, RWKV_W), F32),) * 3 + (jnp.zeros((bsz, 1, LANES), F32),)
    s0_p = jnp.zeros((bsz, RWKV_H // 2, LANES, LANES), F32)
    r_o, sp_p = _rwkv(p3, zeros_first, mus, w0, w2b, a0, a2b, k_k, k_a, r_k, s0_p, ln_g, ln_b, nb=2)
    mem_kv = _norm_matmul(mem_prompt.reshape(bsz * n_mem, d), g_mem, w_mem_b, tm=256, tn=512,
                          vmem_mib=32).reshape(bsz, n_mem, 2 * CROSS_H * CROSS_DH)
    c_o = _cross_attn(p3, OFF_CQ, mem_kv, 0, mem_kv, CROSS_H * CROSS_DH, tq=1024)
    y_p = _merge_out(xp2, d_o.reshape(bsz * seq, -1), r_o.reshape(bsz * seq, -1), c_o.reshape(bsz * seq, -1), p2,
                     w_out_b, g_out, tm=512).reshape(bsz, seq, d)
    nk_p = k_p.reshape(1, bsz, seq, DIFF_H, DIFF_DV)
    nv_p = v_p.reshape(1, bsz, seq, DIFF_H, DIFF_DV)
    nmk_p = mem_kv[:, :, :CROSS_H * CROSS_DH].reshape(1, bsz, n_mem, CROSS_H, CROSS_DH)
    nmv_p = mem_kv[:, :, CROSS_H * CROSS_DH:].reshape(1, bsz, n_mem, CROSS_H, CROSS_DH)
    ns_p = _unpair_states(sp_p)[None]

    xs2 = x_sample.reshape(db * ds, d)
    rows = _pad_rows(jnp.concatenate([xs2, state_shift[l]], axis=0), 16)
    ps_all = _norm_matmul(rows, g_in, w_in_b, tm=rows.shape[0], tn=IN_COLS // 3, vmem_mib=56, n_norm=db * ds)
    ps2 = ps_all[:db * ds]
    ps3 = ps2.reshape(db, ds, IN_COLS)
    hs_s = _rmsnorm_rows(x_sample[:, -1], g_in)
    z_first = ps_all[db * ds:db * ds + db, None, :]
    firsts = (z_first[..., OFF_R:OFF_K], z_first[..., OFF_K:OFF_V], z_first[..., OFF_V:OFF_DQ], z_first[..., OFF_LORA:])

    q5 = ps3[:, :, OFF_DQ:OFF_DK].reshape(db, ds, DIFF_H, 2, DIFF_DH)
    zq = jnp.zeros_like(q5[..., 0, :])
    qm = jnp.stack([jnp.concatenate([q5[..., 0, :], zq], -1), jnp.concatenate([zq, q5[..., 1, :]], -1)], axis=1)
    qm = qm.transpose(0, 1, 3, 2, 4).reshape(db, 2 * DIFF_H * ds, DIFF_DV)
    k_new = ps3[:, :, OFF_DK:OFF_DV].reshape(db, ds * DIFF_H, DIFF_DV)
    v_new = ps3[:, :, OFF_DV:OFF_CQ].reshape(db, ds * DIFF_H, DIFF_DV)
    row_head = (np.arange(2 * DIFF_H * ds) // ds) % DIFF_H
    slope_rows = jnp.asarray(np.broadcast_to(slopes[row_head][:, None], (2 * DIFF_H * ds, LANES)).astype(np.float32))
    n_pool, page = cache_k.shape[1], cache_k.shape[2]
    ck = cache_k[l].reshape(n_pool, page * DIFF_H, DIFF_DV)
    cv = cache_v[l].reshape(n_pool, page * DIFF_H, DIFF_DV)
    ds_o = _paged_diff_attn(qm, ck, cv, k_new, v_new, page_table, slope_rows, lq, subln, n_pages=16,
                            lam_init=lam_init)
    ds_o = ds_o.reshape(db, DIFF_H, ds, DIFF_DV).transpose(0, 2, 1, 3).reshape(db * ds, DIFF_H * DIFF_DV)

    ps3_pad = jnp.pad(ps3, ((0, 0), (0, CHUNK - ds), (0, 0)))
    rs_o, sp_s = _rwkv(ps3_pad, firsts, mus, w0, w2b, a0, a2b, k_k, k_a, r_k, _pair_states(state_rwkv[l]), ln_g, ln_b,
                       nb=2, valid=ds)
    rs_o = rs_o[:, :ds].reshape(db * ds, RWKV_W)

    cq_pad = jnp.pad(ps3[:, :, OFF_CQ:OFF_LORA], ((0, 0), (0, 16 - ds), (0, 0)))
    mk_s = cache_mem_k[l].reshape(db, n_mem, CROSS_H * CROSS_DH)
    mv_s = cache_mem_v[l].reshape(db, n_mem, CROSS_H * CROSS_DH)
    cs_o = _cross_attn(cq_pad, 0, mk_s, 0, mv_s, 0, tq=16)[:, :ds].reshape(db * ds, CROSS_H * CROSS_DH)
    y_s = _merge_out(xs2, ds_o, rs_o, cs_o, ps2, w_out_b, g_out, tm=db * ds).reshape(db, ds, d)
    nk_s = ps3[:, :, OFF_DK:OFF_DV].reshape(1, db, ds, DIFF_H, DIFF_DV)
    nv_s = ps3[:, :, OFF_DV:OFF_CQ].reshape(1, db, ds, DIFF_H, DIFF_DV)
    ns_s = _unpair_states(sp_s)[None]

    return (y_p, y_s, nk_p, nv_p, nmk_p, nmv_p, ns_p, hs_p[None], nk_s, nv_s, ns_s, hs_s[None])
```

```python
import functools
import math

import jax
import jax.numpy as jnp
import numpy as np
from jax import lax
from jax.experimental import pallas as pl
from jax.experimental.pallas import tpu as pltpu

F32 = jnp.float32
BF16 = jnp.bfloat16

LANES = 128
MIB = 1024 * 1024

DIFF_H = 4
DIFF_DV = 128
DIFF_DH = 64
RWKV_N = 64
RWKV_H = 16
RWKV_W = RWKV_H * RWKV_N
CROSS_H = 4
CROSS_DH = 128
LORA = 64
RWKV_GN_EPS = 64e-5
CHUNK = 64
NEG = -1e30

OFF_GATE = 0
OFF_R = 2048
OFF_K = 3072
OFF_V = 4096
OFF_DQ = 5120
OFF_DK = 5632
OFF_DV = 6144
OFF_CQ = 6656
OFF_LORA = 7168
IN_COLS = 7296


def _params(sem, vmem_mib):
    return pltpu.CompilerParams(dimension_semantics=sem, vmem_limit_bytes=vmem_mib * MIB)


def _split2(x):
    hi = x.astype(BF16)
    lo = (x - hi.astype(F32)).astype(BF16)
    return hi, lo


def _dot(a, b, dims=(((1,), (0,)), ((), ()))):
    return lax.dot_general(a, b, dims, preferred_element_type=F32)


NN = (((1,), (0,)), ((), ()))
NT = (((1,), (1,)), ((), ()))
TN = (((0,), (0,)), ((), ()))


def _dot3(a, b, dims=NN):
    ah, al = _split2(a)
    bh, bl = _split2(b)
    return _dot(ah, bh, dims) + (_dot(ah, bl, dims) + _dot(al, bh, dims))


def _dot3s(a, b, dims=NN, passes=3):
    ah, al = a
    bh, bl = b
    out = _dot(ah, bh, dims)
    if passes >= 2:
        out = out + _dot(al, bh, dims)
    if passes >= 3:
        out = out + _dot(ah, bl, dims)
    return out


SCAN_PASSES = dict(intra=1, state_read=1, av=1, solve=1, out=1, update=1)


def _dot_sel(x, sel):
    h, l = _split2(x)
    return _dot(h, sel) + _dot(l, sel)


def _norm_matmul_kernel(x_ref, g_ref, w_ref, o_ref, h_ref, *, n_norm, eps):
    @pl.when(pl.program_id(1) == 0)
    def _():
        x = x_ref[:n_norm, :]
        h_ref[:n_norm, :] = (x * lax.rsqrt(jnp.mean(x * x, axis=-1, keepdims=True) + eps) * g_ref[...]).astype(BF16)
        if n_norm < x_ref.shape[0]:
            h_ref[n_norm:, :] = x_ref[n_norm:, :].astype(BF16)

    o_ref[...] = jnp.dot(h_ref[...], w_ref[...], preferred_element_type=F32)


def _norm_matmul(x, g, w, *, tm, tn, vmem_mib, n_norm=None):
    m, d = x.shape
    n = w.shape[1]
    assert m % tm == 0 and n % tn == 0 and (n_norm is None or m == tm)
    return pl.pallas_call(
        functools.partial(_norm_matmul_kernel, n_norm=tm if n_norm is None else n_norm, eps=1e-6),
        grid=(m // tm, n // tn),
        in_specs=[pl.BlockSpec((tm, d), lambda i, j: (i, 0)),
                  pl.BlockSpec((1, d), lambda i, j: (0, 0)),
                  pl.BlockSpec((d, tn), lambda i, j: (0, j))],
        out_specs=pl.BlockSpec((tm, tn), lambda i, j: (i, j)),
        out_shape=jax.ShapeDtypeStruct((m, n), F32),
        scratch_shapes=[pltpu.VMEM((tm, d), BF16)],
        compiler_params=_params(("arbitrary", "arbitrary"), vmem_mib),
        name="norm_matmul",
    )(x, g, w)


def _rmsnorm_kernel(x_ref, g_ref, o_ref, *, eps):
    x = x_ref[...]
    o_ref[...] = x * lax.rsqrt(jnp.mean(x * x, axis=-1, keepdims=True) + eps) * g_ref[...]


def _pad_rows(x, mult):
    pad = -x.shape[0] % mult
    return jnp.pad(x, ((0, pad), (0, 0))) if pad else x


def _rmsnorm_rows(x, g):
    n = x.shape[0]
    xp = _pad_rows(x, 8)
    return pl.pallas_call(
        functools.partial(_rmsnorm_kernel, eps=1e-6),
        out_shape=jax.ShapeDtypeStruct(xp.shape, F32),
        name="rmsnorm_rows",
    )(xp, g)[:n]


def _diff_lambda(lq, lam_init):
    t1 = jnp.sum(lq[0:1] * lq[1:2], axis=-1, keepdims=True)
    t2 = jnp.sum(lq[2:3] * lq[3:4], axis=-1, keepdims=True)
    return jnp.exp(t1) - jnp.exp(t2) + lam_init


def _diff_finish(acc, l, lq, subln, rows, lam_init):
    o12 = acc / l
    o = o12[:rows] - _diff_lambda(lq, lam_init) * o12[rows:]
    o = o * lax.rsqrt(jnp.mean(o * o, axis=-1, keepdims=True) + 1e-5) * subln
    return o * (1.0 - lam_init)


LOG2E = math.log2(math.e)
POS_RADIX = 16
N_SLOPE_PIECES = 3
SCORE_LOOKAHEAD = 8


def _alibi_lanes(slopes, tq):
    assert tq <= POS_RADIX * 256
    c = np.asarray(slopes, np.float64) * LOG2E
    pieces, rest = [], c
    for _ in range(N_SLOPE_PIECES):
        piece = rest.astype(np.float32).astype(BF16).astype(np.float64)
        pieces.append(piece)
        rest = rest - piece
    q_lanes = np.zeros((len(c), 1, LANES), np.float32)
    k_lanes = np.zeros((tq, LANES), np.float32)
    pos = np.arange(tq)
    for n, piece in enumerate(pieces):
        q_lanes[:, 0, 2 * n] = POS_RADIX * piece
        q_lanes[:, 0, 2 * n + 1] = piece
        k_lanes[:, 2 * n] = pos // POS_RADIX
        k_lanes[:, 2 * n + 1] = pos % POS_RADIX
    c_sum = np.broadcast_to(sum(pieces)[:, None, None], (len(c), 1, LANES)).astype(np.float32)
    return jnp.asarray(q_lanes), jnp.asarray(k_lanes.astype(BF16)), jnp.asarray(c_sum)


def _stack_maps(q, q_lanes):
    q = q * (DIFF_DH ** -0.5 * LOG2E)
    lane = lax.broadcasted_iota(jnp.int32, q.shape, 1)
    q1 = jnp.where(lane < DIFF_DH, q, 0.0)
    q2 = jnp.where(lane >= DIFF_DH, q, 0.0)
    qq = jnp.concatenate([q1, q2], axis=0)
    return jnp.concatenate([qq, jnp.broadcast_to(q_lanes, qq.shape)], axis=1).astype(BF16)


def _diff_attn_kernel(qi_ref, kj_ref, q_ref, k_ref, v_ref, qlane_ref, klane_ref, slope_ref, lq_ref, subln_ref,
                      o_ref, ko_ref, vo_ref, qq_ref, m_ref, l_ref, acc_ref, *, tq, qw, lam_init):
    t = pl.program_id(2)
    i = qi_ref[t]
    j = kj_ref[t]

    @pl.when(j == 0)
    def _():
        qq_ref[...] = _stack_maps(q_ref[0], qlane_ref[0])
        m_ref[...] = jnp.full(m_ref.shape, NEG, F32)
        l_ref[...] = jnp.zeros(l_ref.shape, F32)
        acc_ref[...] = jnp.zeros(acc_ref.shape, F32)

    kb = jnp.concatenate([k_ref[0].astype(BF16), klane_ref[...]], axis=1)
    vb = v_ref[0].astype(BF16)
    block_off = slope_ref[0][:, :1] * jnp.full((1, 1), (j - i) * tq, jnp.int32).astype(F32)

    def accumulate(diagonal):
        starts = list(range(0, 2 * tq, qw))
        tiles = [pl.ds(c0, qw) for c0 in starts]
        n_keys = [c0 % tq + qw if diagonal else tq for c0 in starts]
        def score_tile(t):
            s = _dot(kb[:n_keys[t]], qq_ref[tiles[t], :], NT)
            if diagonal:
                qry = starts[t] % tq + lax.broadcasted_iota(jnp.int32, s.shape, 1)
                s = jnp.where(lax.broadcasted_iota(jnp.int32, s.shape, 0) <= qry, s, NEG)
            return s

        def softmax_tile(t, s):
            cols = tiles[t]
            m_prev = m_ref[:, cols] - block_off
            m_new = jnp.maximum(m_prev, jnp.max(s, axis=0, keepdims=True))
            alpha = jnp.exp2(m_prev - m_new)
            p = jnp.exp2(s - m_new)
            l_ref[:, cols] = alpha * l_ref[:, cols] + jnp.sum(p, axis=0, keepdims=True)
            acc_ref[:, cols] = alpha * acc_ref[:, cols] + _dot(vb[:n_keys[t]], p.astype(BF16), TN)
            m_ref[:, cols] = m_new + block_off

        scores = {}
        for t in range(len(tiles) + SCORE_LOOKAHEAD):
            if t < len(tiles):
                scores[t] = score_tile(t)
            if t >= SCORE_LOOKAHEAD:
                softmax_tile(t - SCORE_LOOKAHEAD, scores.pop(t - SCORE_LOOKAHEAD))

    @pl.when(j < i)
    def _():
        accumulate(False)

    @pl.when(j == i)
    def _():
        accumulate(True)
        o12 = acc_ref[...] / l_ref[...]
        o = o12[:, :tq] - _diff_lambda(lq_ref[...], lam_init) * o12[:, tq:]
        o = o * lax.rsqrt(jnp.mean(o * o, axis=0, keepdims=True) + 1e-5) * subln_ref[...]
        o_ref[0] = (o * (1.0 - lam_init)).T
        ko_ref[0] = k_ref[0]
        vo_ref[0] = v_ref[0]


def _diff_attn(p3, slopes, lambda_qk, subln_col, *, tq, qw, lam_init):
    b, t, _ = p3.shape
    nq = t // tq
    q_lanes, k_lanes, c_sum = _alibi_lanes(slopes, tq)
    pairs = [(i, j) for i in range(nq) for j in range(i + 1)]
    qi = jnp.asarray(np.array([p[0] for p in pairs], np.int32))
    kj = jnp.asarray(np.array([p[1] for p in pairs], np.int32))
    qb, kb, vb = OFF_DQ // DIFF_DV, OFF_DK // DIFF_DV, OFF_DV // DIFF_DV
    grid_spec = pltpu.PrefetchScalarGridSpec(
        num_scalar_prefetch=2,
        grid=(b, DIFF_H, len(pairs)),
        in_specs=[pl.BlockSpec((1, tq, DIFF_DV), lambda bb, h, s, qi, kj: (bb, qi[s], qb + h)),
                  pl.BlockSpec((1, tq, DIFF_DV), lambda bb, h, s, qi, kj: (bb, kj[s], kb + h)),
                  pl.BlockSpec((1, tq, DIFF_DV), lambda bb, h, s, qi, kj: (bb, kj[s], vb + h)),
                  pl.BlockSpec((1, 1, LANES), lambda bb, h, s, qi, kj: (h, 0, 0)),
                  pl.BlockSpec((tq, LANES), lambda bb, h, s, qi, kj: (0, 0)),
                  pl.BlockSpec((1, 1, LANES), lambda bb, h, s, qi, kj: (h, 0, 0)),
                  pl.BlockSpec((4, DIFF_DH), lambda bb, h, s, qi, kj: (0, 0)),
                  pl.BlockSpec((DIFF_DV, 1), lambda bb, h, s, qi, kj: (0, 0))],
        out_specs=[pl.BlockSpec((1, tq, DIFF_DV), lambda bb, h, s, qi, kj: (bb, qi[s], h))] * 3,
        scratch_shapes=[pltpu.VMEM((2 * tq, 2 * DIFF_DV), BF16), pltpu.VMEM((1, 2 * tq), F32),
                        pltpu.VMEM((1, 2 * tq), F32), pltpu.VMEM((DIFF_DV, 2 * tq), F32)],
    )
    return pl.pallas_call(
        functools.partial(_diff_attn_kernel, tq=tq, qw=qw, lam_init=lam_init),
        grid_spec=grid_spec,
        out_shape=[jax.ShapeDtypeStruct((b, t, DIFF_H * DIFF_DV), F32)] * 3,
        compiler_params=_params(("arbitrary", "arbitrary", "arbitrary"), 40),
        name="diff_attn",
    )(qi, kj, p3, p3, p3, q_lanes, k_lanes, c_sum, lambda_qk, subln_col)


def _paged_diff_attn_kernel(pt_ref, q_ref, *refs, n_pages, page, n_new, past_len, lam_init):
    k_refs = refs[:n_pages]
    v_refs = refs[n_pages:2 * n_pages]
    kn_ref, vn_ref, slope_ref, lq_ref, subln_ref, o_ref, m_ref, l_ref, acc_ref = refs[2 * n_pages:]
    s_id = pl.program_id(1)
    rows = 2 * DIFF_H * n_new

    @pl.when(s_id == 0)
    def _():
        m_ref[...] = jnp.full(m_ref.shape, NEG, F32)
        l_ref[...] = jnp.zeros(l_ref.shape, F32)
        acc_ref[...] = jnp.zeros(acc_ref.shape, F32)

    qb = (q_ref[0] * (DIFF_DH ** -0.5)).astype(BF16)
    slope = slope_ref[...][:, :1]
    log_new, log_h = n_new.bit_length() - 1, DIFF_H.bit_length() - 1

    def coords(ncol):
        row = lax.broadcasted_iota(jnp.int32, (rows, ncol), 0)
        col = lax.broadcasted_iota(jnp.int32, (rows, ncol), 1)
        row_h = lax.shift_right_logical(row, log_new) & (DIFF_H - 1)
        return row & (n_new - 1), (col & (DIFF_H - 1)) == row_h, lax.shift_right_logical(col, log_h)

    def update(ss, vbs, carry):
        m_prev, l_prev, acc_prev = carry
        m_new = m_prev
        for s in ss:
            m_new = jnp.maximum(m_new, jnp.max(s, axis=-1, keepdims=True))
        alpha = jnp.exp(m_prev - m_new)
        l_new = alpha * l_prev
        acc_new = alpha * acc_prev
        for s, vb in zip(ss, vbs):
            p = jnp.exp(s - m_new)
            l_new = l_new + jnp.sum(p, axis=-1, keepdims=True)
            acc_new = acc_new + _dot(p.astype(BF16), vb)
        return m_new, l_new, acc_new

    _, same_head, tok = coords(page * DIFF_H)
    ss, vbs = [], []
    for i in range(n_pages):
        kb = k_refs[i][0].astype(BF16)
        vbs.append(v_refs[i][0].astype(BF16))
        k_pos = ((s_id * n_pages + i) * page - past_len + tok).astype(F32)
        ss.append(jnp.where(same_head, _dot(qb, kb, NT) + slope * k_pos, NEG))
    m_ref[...], l_ref[...], acc_ref[...] = update(ss, vbs, (m_ref[...], l_ref[...], acc_ref[...]))

    @pl.when(s_id == pl.num_programs(1) - 1)
    def _():
        kb = kn_ref[0].astype(BF16)
        vb = vn_ref[0].astype(BF16)
        row_tn, same_head_n, tok_n = coords(n_new * DIFF_H)
        k_pos = tok_n.astype(F32)
        s = jnp.where(same_head_n & (tok_n <= row_tn), _dot(qb, kb, NT) + slope * k_pos, NEG)
        _, l, acc = update([s], [vb], (m_ref[...], l_ref[...], acc_ref[...]))
        o_ref[0] = _diff_finish(acc, l, lq_ref[...], subln_ref[...], rows // 2, lam_init)


def _paged_diff_attn(qm, cache_k, cache_v, k_new, v_new, page_table, slope_rows, lambda_qk, subln, *,
                     n_pages, lam_init):
    db, rows, _ = qm.shape
    n_pool, page = cache_k.shape[0], cache_k.shape[1] // DIFF_H
    n_tab = page_table.shape[1]
    n_new = rows // (2 * DIFF_H)
    assert n_tab % n_pages == 0 and n_new & (n_new - 1) == 0 and DIFF_H & (DIFF_H - 1) == 0

    def page_spec(i):
        return pl.BlockSpec((1, page * DIFF_H, DIFF_DV), lambda b, s, pt: (pt[b, s * n_pages + i], 0, 0))

    grid_spec = pltpu.PrefetchScalarGridSpec(
        num_scalar_prefetch=1,
        grid=(db, n_tab // n_pages),
        in_specs=([pl.BlockSpec((1, rows, DIFF_DV), lambda b, s, pt: (b, 0, 0))]
                  + [page_spec(i) for i in range(n_pages)] * 2
                  + [pl.BlockSpec((1, n_new * DIFF_H, DIFF_DV), lambda b, s, pt: (b, 0, 0)),
                     pl.BlockSpec((1, n_new * DIFF_H, DIFF_DV), lambda b, s, pt: (b, 0, 0)),
                     pl.BlockSpec((rows, LANES), lambda b, s, pt: (0, 0)),
                     pl.BlockSpec((4, DIFF_DH), lambda b, s, pt: (0, 0)),
                     pl.BlockSpec((1, DIFF_DV), lambda b, s, pt: (0, 0))]),
        out_specs=pl.BlockSpec((1, rows // 2, DIFF_DV), lambda b, s, pt: (b, 0, 0)),
        scratch_shapes=[pltpu.VMEM((rows, 1), F32), pltpu.VMEM((rows, 1), F32), pltpu.VMEM((rows, DIFF_DV), F32)],
    )
    return pl.pallas_call(
        functools.partial(_paged_diff_attn_kernel, n_pages=n_pages, page=page, n_new=n_new,
                          past_len=n_tab * page, lam_init=lam_init),
        grid_spec=grid_spec,
        out_shape=jax.ShapeDtypeStruct((db, rows // 2, DIFF_DV), F32),
        compiler_params=_params(("arbitrary", "arbitrary"), 32),
        name="paged_diff_attn",
    )(page_table, qm, *([cache_k] * n_pages), *([cache_v] * n_pages), k_new, v_new, slope_rows, lambda_qk, subln)


def _cross_attn_kernel(q_ref, k_ref, v_ref, o_ref, *, scale):
    for h in range(CROSS_H):
        cols = pl.ds(h * CROSS_DH, CROSS_DH)
        q = q_ref[0, :, cols].astype(BF16)
        k = k_ref[0, :, cols].astype(BF16)
        v = v_ref[0, :, cols].astype(BF16)
        s = _dot(q, k, NT) * scale
        p = jnp.exp(s - jnp.max(s, axis=-1, keepdims=True))
        l = jnp.sum(p, axis=-1, keepdims=True)
        o_ref[0, :, cols] = _dot(p.astype(BF16), v) / l


def _cross_attn(q_arr, q_off, k_arr, k_off, v_arr, v_off, *, tq):
    b, t, _ = q_arr.shape
    n_mem = k_arr.shape[1]
    width = CROSS_H * CROSS_DH
    qb, kb, vb = q_off // width, k_off // width, v_off // width
    return pl.pallas_call(
        functools.partial(_cross_attn_kernel, scale=CROSS_DH ** -0.5),
        grid=(b, t // tq),
        in_specs=[pl.BlockSpec((1, tq, width), lambda bb, i: (bb, i, qb)),
                  pl.BlockSpec((1, n_mem, width), lambda bb, i: (bb, 0, kb)),
                  pl.BlockSpec((1, n_mem, width), lambda bb, i: (bb, 0, vb))],
        out_specs=pl.BlockSpec((1, tq, width), lambda bb, i: (bb, i, 0)),
        out_shape=jax.ShapeDtypeStruct((b, t, width), F32),
        compiler_params=_params(("arbitrary", "arbitrary"), 32),
        name="cross_attn",
    )(q_arr, k_arr, v_arr)


def _rwkv_step(r_ref, k_ref, v_ref, lo_ref, fr_ref, fk_ref, fv_ref, fl_ref,
               mur_ref, muk_ref, muv_ref, mul_ref, w0_ref, w2_ref, a0_ref, a2_ref, kk_ref, ka_ref, rk_ref,
               s0_ref, g_ref, bb_ref, y_ref, sout_ref,
               lhs_s, rhs_s, vm_s, gend_s, bonus_s, s_ref, cr_ref, ck_ref, cv_ref, cl_ref,
               *, chunk, n_pairs, nb, n_chunks, valid, slot_w, slot_r, ahead):
    c = pl.program_id(1)
    c2 = 2 * chunk

    rid = lax.broadcasted_iota(jnp.int32, (c2, c2), 0)
    cid = lax.broadcasted_iota(jnp.int32, (c2, c2), 1)
    strict = rid > cid
    incl2 = jnp.concatenate([rid >= cid, rid >= cid], axis=1)
    tr = lax.broadcasted_iota(jnp.int32, (chunk, chunk), 0)
    tc = lax.broadcasted_iota(jnp.int32, (chunk, chunk), 1)
    tri = (tr >= tc).astype(BF16)
    srow = lax.broadcasted_iota(jnp.int32, (c2, LANES), 0)
    slane = lax.broadcasted_iota(jnp.int32, (c2, LANES), 1)
    own = (srow < chunk) == (slane < RWKV_N)
    low_half = lax.broadcasted_iota(jnp.int32, (chunk, LANES), 1) < RWKV_N

    def stack(x):
        return jnp.where(own, jnp.concatenate([x, x], axis=0), 0.0).astype(BF16)

    def head_sum(x):
        s_lo = jnp.sum(jnp.where(low_half, x, 0.0), axis=-1, keepdims=True)
        s_hi = jnp.sum(jnp.where(low_half, 0.0, x), axis=-1, keepdims=True)
        return jnp.where(low_half, s_lo, s_hi)

    row_id = lax.broadcasted_iota(jnp.int32, (chunk, 1), 0)

    def prepare():
        for bi in range(nb):
            def shift_mix(x_ref, carry_ref, mu_ref):
                x = x_ref[bi]
                prev = pltpu.roll(x, 1, axis=0)
                prev = jnp.where(lax.broadcasted_iota(jnp.int32, x.shape, 0) == 0, carry_ref[bi], prev)
                carry_ref[bi] = x[chunk - 1:chunk, :]
                return x + mu_ref[...] * (prev - x)

            r = shift_mix(r_ref, cr_ref, mur_ref)
            k = shift_mix(k_ref, ck_ref, muk_ref)
            v = shift_mix(v_ref, cv_ref, muv_ref)
            lo = shift_mix(lo_ref, cl_ref, mul_ref)
            yield
            xw = -(w0_ref[...] + _dot(jnp.tanh(lo).astype(BF16), w2_ref[...]))
            softplus = jnp.maximum(xw, 0.0) + jnp.log(1.0 + jnp.exp(-jnp.abs(xw)))
            lw = -jnp.exp(-softplus - 0.5)
            a = jax.nn.sigmoid(a0_ref[...] + _dot(lo.astype(BF16), a2_ref[...]))
            kkr = k * kk_ref[...]
            k2 = k * (1.0 + (a - 1.0) * ka_ref[...])
            rb = r * k2 * rk_ref[...]
            if valid is not None:
                ok = jnp.minimum(c, n_chunks - 1) * chunk + row_id < valid
                lw, r, k2, v, kkr, rb = [jnp.where(ok, t, 0.0) for t in (lw, r, k2, v, kkr, rb)]
            yield
            for pi in range(n_pairs):
                cols = slice(pi * LANES, (pi + 1) * LANES)
                lw_p, kkr_p, v_p = lw[:, cols], kkr[:, cols], v[:, cols]
                kk = kkr_p / jnp.maximum(jnp.sqrt(head_sum(kkr_p * kkr_p)), 1e-12)
                h1 = lw_p.astype(BF16)
                r1 = lw_p - h1.astype(F32)
                h2 = r1.astype(BF16)
                h3 = (r1 - h2.astype(F32)).astype(BF16)
                cum = _dot(tri, h1) + (_dot(tri, h2) + _dot(tri, h3))
                g = jnp.exp(cum)
                gi = jnp.exp(-cum)
                gp = jnp.exp(cum - lw_p)
                lhs_s[slot_w, bi, pi] = jnp.concatenate([stack(-kk * gp), stack(r[:, cols] * g)], axis=0)
                rhs_s[slot_w, bi, pi] = jnp.concatenate([stack(kk * a[:, cols] * gi), stack(k2[:, cols] * gi)], axis=0)
                vm_s[slot_w, bi, pi] = stack(v_p)
                gend_s[slot_w, bi, :, cols] = g[chunk - 1:chunk, :]
                bonus_s[slot_w, bi, :, cols] = head_sum(rb[:, cols]) * v_p
                yield

    pieces = prepare()
    n_pieces = nb * (n_pairs + 2)
    n_ticks = (chunk.bit_length() + 4) * nb * n_pairs
    progress = [0, 0]

    def stage(fn):
        out = []
        for u in n_units:
            out.append(fn(u))
            progress[0] += 1
            while progress[1] < progress[0] * n_pieces // n_ticks:
                next(pieces, None)
                progress[1] += 1
        return out

    if not ahead:
        for _ in pieces:
            pass

    units = [(bi, pi) for bi in range(nb) for pi in range(n_pairs)]
    lhs = [lhs_s[slot_r, bi, pi] for bi, pi in units]
    rhs = [rhs_s[slot_r, bi, pi] for bi, pi in units]
    vm = [vm_s[slot_r, bi, pi] for bi, pi in units]
    s_prev = [s_ref[bi, pi] for bi, pi in units]
    n_units = range(len(units))
    p1 = stage(lambda u: _dot(lhs[u], rhs[u], NT))
    p2 = stage(lambda u: _dot(lhs[u], s_prev[u].astype(BF16), NT))
    nm = [jnp.where(strict, p1[u][:c2, :c2], 0.0).astype(BF16) for u in n_units]
    aak = [jnp.where(strict, p1[u][:c2, c2:], 0.0).astype(BF16) for u in n_units]
    m2 = [jnp.where(incl2, p1[u][c2:, :], 0.0).astype(BF16) for u in n_units]
    uu = stage(lambda u: p2[u][:c2] + _dot(aak[u], vm[u]))
    n = 1
    while 2 * n < chunk:
        y = stage(lambda u: _dot(nm[u], jnp.concatenate([nm[u], uu[u].astype(BF16)], axis=1)))
        uu = [uu[u] + y[u][:, c2:] for u in n_units]
        nm = [y[u][:, :c2].astype(BF16) for u in n_units]
        n *= 2
    uu = stage(lambda u: uu[u] + _dot(nm[u], uu[u].astype(BF16)))
    uv = [jnp.concatenate([uu[u].astype(BF16), vm[u]], axis=0) for u in n_units]
    o = stage(lambda u: p2[u][c2:] + _dot(m2[u], uv[u]))
    s_new = stage(lambda u: s_prev[u] + _dot(uv[u], rhs[u], TN))
    for _ in pieces:
        pass
    for u, (bi, pi) in zip(n_units, units):
        s_ref[bi, pi] = s_new[u] * gend_s[slot_r, bi, :, pl.ds(pi * LANES, LANES)]
    for u, (bi, pi) in zip(n_units, units):
        cols = pl.ds(pi * LANES, LANES)
        mu = jnp.sum(o[u], axis=-1, keepdims=True) * (1.0 / RWKV_N)
        d = jnp.where(own, o[u] - mu, 0.0)
        var = jnp.sum(d * d, axis=-1, keepdims=True) * (1.0 / RWKV_N)
        yn = d * lax.rsqrt(var + RWKV_GN_EPS)
        y_ref[bi, :, cols] = ((yn[:chunk] + yn[chunk:]) * g_ref[:, cols] + bb_ref[:, cols]
                              + bonus_s[slot_r, bi, :, cols])


def _rwkv_kernel(*refs, n_chunks, **static):
    (fr_ref, fk_ref, fv_ref, fl_ref), s0_ref, sout_ref = refs[4:8], refs[19], refs[23]
    lhs_s, rhs_s, vm_s, gend_s, bonus_s, s_ref, cr_ref, ck_ref, cv_ref, cl_ref = refs[24:]
    c = pl.program_id(1)
    ahead = n_chunks > 1

    @pl.when(c == 0)
    def _():
        cr_ref[...] = fr_ref[...]
        ck_ref[...] = fk_ref[...]
        cv_ref[...] = fv_ref[...]
        cl_ref[...] = fl_ref[...]
        s_ref[...] = s0_ref[...]
        if ahead:
            lhs_s[1] = jnp.zeros(lhs_s.shape[1:], BF16)
            rhs_s[1] = jnp.zeros(rhs_s.shape[1:], BF16)
            vm_s[1] = jnp.zeros(vm_s.shape[1:], BF16)
            gend_s[1] = jnp.ones(gend_s.shape[1:], F32)
            bonus_s[1] = jnp.zeros(bonus_s.shape[1:], F32)

    if ahead:
        @pl.when(lax.rem(c, 2) == 0)
        def _():
            _rwkv_step(*refs, n_chunks=n_chunks, slot_w=0, slot_r=1, ahead=True, **static)

        @pl.when(lax.rem(c, 2) == 1)
        def _():
            _rwkv_step(*refs, n_chunks=n_chunks, slot_w=1, slot_r=0, ahead=True, **static)
    else:
        _rwkv_step(*refs, n_chunks=n_chunks, slot_w=0, slot_r=0, ahead=False, **static)

    @pl.when(c == (n_chunks if ahead else 0))
    def _():
        sout_ref[...] = s_ref[...]


def _rwkv(p3, firsts, mus, w0, w2b, a0, a2b, k_k, k_a, r_k, s0_pairs, ln_g, ln_b, *, nb, valid=None):
    bsz, t, _ = p3.shape
    n_pairs = RWKV_H // 2
    n_chunks = t // CHUNK
    assert bsz % nb == 0 and t % CHUNK == 0
    cur = lambda c: jnp.minimum(c, n_chunks - 1)
    wide = lambda blk: pl.BlockSpec((nb, CHUNK, RWKV_W), lambda bb, c: (bb, cur(c), blk))
    vec = lambda n: pl.BlockSpec((1, n), lambda bb, c: (0, 0))
    first = lambda n: pl.BlockSpec((nb, 1, n), lambda bb, c: (bb, 0, 0))
    st = pl.BlockSpec((nb, n_pairs, LANES, LANES), lambda bb, c: (bb, 0, 0, 0))
    in_specs = [wide(OFF_R // RWKV_W), wide(OFF_K // RWKV_W), wide(OFF_V // RWKV_W),
                pl.BlockSpec((nb, CHUNK, LANES), lambda bb, c: (bb, cur(c), OFF_LORA // LANES)),
                first(RWKV_W), first(RWKV_W), first(RWKV_W), first(LANES),
                vec(RWKV_W), vec(RWKV_W), vec(RWKV_W), vec(LANES),
                vec(RWKV_W), pl.BlockSpec((LANES, RWKV_W), lambda bb, c: (0, 0)),
                vec(RWKV_W), pl.BlockSpec((LANES, RWKV_W), lambda bb, c: (0, 0)),
                vec(RWKV_W), vec(RWKV_W), vec(RWKV_W),
                st, vec(RWKV_W), vec(RWKV_W)]
    lag = 1 if n_chunks > 1 else 0
    out_specs = [pl.BlockSpec((nb, CHUNK, RWKV_W), lambda bb, c: (bb, jnp.maximum(c - lag, 0), 0)), st]
    scratch = [pltpu.VMEM((2, nb, n_pairs, 4 * CHUNK, LANES), BF16), pltpu.VMEM((2, nb, n_pairs, 4 * CHUNK, LANES), BF16),
               pltpu.VMEM((2, nb, n_pairs, 2 * CHUNK, LANES), BF16), pltpu.VMEM((2, nb, 1, RWKV_W), F32),
               pltpu.VMEM((2, nb, CHUNK, RWKV_W), F32), pltpu.VMEM((nb, n_pairs, LANES, LANES), F32),
               pltpu.VMEM((nb, 1, RWKV_W), F32), pltpu.VMEM((nb, 1, RWKV_W), F32),
               pltpu.VMEM((nb, 1, RWKV_W), F32), pltpu.VMEM((nb, 1, LANES), F32)]
    return pl.pallas_call(
        functools.partial(_rwkv_kernel, chunk=CHUNK, n_pairs=n_pairs, nb=nb, n_chunks=n_chunks, valid=valid),
        grid=(bsz // nb, n_chunks + lag),
        in_specs=in_specs,
        out_specs=out_specs,
        out_shape=[jax.ShapeDtypeStruct((bsz, t, RWKV_W), F32),
                   jax.ShapeDtypeStruct((bsz, n_pairs, LANES, LANES), F32)],
        scratch_shapes=scratch,
        compiler_params=_params(("arbitrary", "arbitrary"), 48),
        name="rwkv",
    )(p3, p3, p3, p3, *firsts, *mus, w0, w2b, a0, a2b, k_k, k_a, r_k, s0_pairs, ln_g, ln_b)


def _rwkv_prep_kernel(r_ref, k_ref, v_ref, lo_ref, fr_ref, fk_ref, fv_ref, fl_ref,
                      mur_ref, muk_ref, muv_ref, mul_ref, w0_ref, w2_ref, a0_ref, a2_ref,
                      kk_ref, ka_ref, rk_ref, e1_ref, e2_ref,
                      ro_ref, lwo_ref, ko_ref, vo_ref, ao_ref, bo_ref, bonus_ref,
                      cr_ref, ck_ref, cv_ref, cl_ref):
    @pl.when(pl.program_id(1) == 0)
    def _():
        cr_ref[...] = fr_ref[0]
        ck_ref[...] = fk_ref[0]
        cv_ref[...] = fv_ref[0]
        cl_ref[...] = fl_ref[0]

    def shift_mix(x_ref, carry_ref, mu_ref):
        x = x_ref[0]
        rows = x.shape[0]
        prev = pltpu.roll(x, 1, axis=0)
        rid = lax.broadcasted_iota(jnp.int32, x.shape, 0)
        prev = jnp.where(rid == 0, carry_ref[...], prev)
        carry_ref[...] = x[rows - 1:rows, :]
        return x + mu_ref[...] * (prev - x)

    r = shift_mix(r_ref, cr_ref, mur_ref)
    k = shift_mix(k_ref, ck_ref, muk_ref)
    v = shift_mix(v_ref, cv_ref, muv_ref)
    lo = shift_mix(lo_ref, cl_ref, mul_ref)

    xw = -(w0_ref[...] + _dot3(jnp.tanh(lo), w2_ref[...]))
    softplus = jnp.maximum(xw, 0.0) + jnp.log(1.0 + jnp.exp(-jnp.abs(xw)))
    w = -softplus - 0.5
    lwo_ref[0] = -jnp.exp(w)
    a = jax.nn.sigmoid(a0_ref[...] + _dot3(lo, a2_ref[...]))

    def head_sum(x):
        return _dot_sel(_dot_sel(x, e1_ref[...]), e2_ref[...])

    kkr = k * kk_ref[...]
    kk = kkr / jnp.maximum(jnp.sqrt(head_sum(kkr * kkr)), 1e-12)
    k2 = k * (1.0 + (a - 1.0) * ka_ref[...])
    ro_ref[0] = r
    ko_ref[0] = k2
    vo_ref[0] = v
    ao_ref[0] = -kk
    bo_ref[0] = kk * a
    bonus_ref[0] = head_sum(r * k2 * rk_ref[...]) * v


def _rwkv_prep(p3, firsts, mus, w0, w2p, a0, a2p, k_k, k_a, r_k, e1, e2, *, tm):
    b, t, _ = p3.shape
    nt = t // tm
    wide = lambda blk: pl.BlockSpec((1, tm, RWKV_W), lambda bb, i: (bb, i, blk))
    vec = lambda n: pl.BlockSpec((1, n), lambda bb, i: (0, 0))
    first = lambda n: pl.BlockSpec((1, 1, n), lambda bb, i: (bb, 0, 0))
    out_wide = pl.BlockSpec((1, tm, RWKV_W), lambda bb, i: (bb, i, 0))
    in_specs = [wide(OFF_R // RWKV_W), wide(OFF_K // RWKV_W), wide(OFF_V // RWKV_W),
                pl.BlockSpec((1, tm, LANES), lambda bb, i: (bb, i, OFF_LORA // LANES)),
                first(RWKV_W), first(RWKV_W), first(RWKV_W), first(LANES),
                vec(RWKV_W), vec(RWKV_W), vec(RWKV_W), vec(LANES),
                vec(RWKV_W), pl.BlockSpec((LANES, RWKV_W), lambda bb, i: (0, 0)),
                vec(RWKV_W), pl.BlockSpec((LANES, RWKV_W), lambda bb, i: (0, 0)),
                vec(RWKV_W), vec(RWKV_W), vec(RWKV_W),
                pl.BlockSpec((RWKV_W, LANES), lambda bb, i: (0, 0)),
                pl.BlockSpec((LANES, RWKV_W), lambda bb, i: (0, 0))]
    shape = jax.ShapeDtypeStruct((b, t, RWKV_W), F32)
    return pl.pallas_call(
        _rwkv_prep_kernel,
        grid=(b, nt),
        in_specs=in_specs,
        out_specs=[out_wide] * 7,
        out_shape=[shape] * 7,
        scratch_shapes=[pltpu.VMEM((1, RWKV_W), F32), pltpu.VMEM((1, RWKV_W), F32),
                        pltpu.VMEM((1, RWKV_W), F32), pltpu.VMEM((1, LANES), F32)],
        compiler_params=_params(("arbitrary", "arbitrary"), 48),
        name="rwkv_prep",
    )(p3, p3, p3, p3, *firsts, *mus, w0, w2p, a0, a2p, k_k, k_a, r_k, e1, e2)


def _rwkv_scan_kernel(r_ref, lw_ref, k_ref, v_ref, a_ref, b_ref, bonus_ref, s0_ref, g_ref, bb_ref,
                      y_ref, sout_ref, s_ref, *, chunk, n_pairs, nb):
    c = pl.program_id(1)
    c2 = 2 * chunk

    @pl.when(c == 0)
    def _():
        s_ref[...] = s0_ref[...]

    rid = lax.broadcasted_iota(jnp.int32, (c2, c2), 0)
    cid = lax.broadcasted_iota(jnp.int32, (c2, c2), 1)
    strict = rid > cid
    incl2 = jnp.concatenate([rid >= cid, rid >= cid], axis=1)
    tr = lax.broadcasted_iota(jnp.int32, (chunk, chunk), 0)
    tc = lax.broadcasted_iota(jnp.int32, (chunk, chunk), 1)
    tri = (tr >= tc).astype(BF16)
    srow = lax.broadcasted_iota(jnp.int32, (c2, LANES), 0)
    slane = lax.broadcasted_iota(jnp.int32, (c2, LANES), 1)
    own = (srow < chunk) == (slane < RWKV_N)

    def stack(x):
        return jnp.where(own, jnp.concatenate([x, x], axis=0), 0.0)

    def cat2(x, y, axis):
        return tuple(jnp.concatenate([p, q], axis=axis) for p, q in zip(x, y))

    units = [(bi, pi) for bi in range(nb) for pi in range(n_pairs)]
    pairs = range(len(units))
    sls = [(bi, slice(None), pl.ds(pi * LANES, LANES)) for bi, pi in units]
    lhs, rhs, vm, g_end, s_prev = [], [], [], [], []
    for sl in sls:
        lw = lw_ref[sl]
        h1 = lw.astype(BF16)
        r1 = lw - h1.astype(F32)
        h2 = r1.astype(BF16)
        h3 = (r1 - h2.astype(F32)).astype(BF16)
        cum = _dot(tri, h1) + (_dot(tri, h2) + _dot(tri, h3))
        g = jnp.exp(cum)
        gi = jnp.exp(-cum)
        gp = jnp.exp(cum - lw)
        am = stack(a_ref[sl] * gp)
        rm = stack(r_ref[sl] * g)
        bm = stack(b_ref[sl] * gi)
        km = stack(k_ref[sl] * gi)
        lhs.append(_split2(jnp.concatenate([am, rm], axis=0)))
        rhs.append(_split2(jnp.concatenate([bm, km], axis=0)))
        vm.append(_split2(stack(v_ref[sl])))
        g_end.append(g[chunk - 1:chunk, :])
    for bi, pi in units:
        s_prev.append(s_ref[bi, pi])

    ps = SCAN_PASSES
    p1 = [_dot3s(lhs[p], rhs[p], NT, ps['intra']) for p in pairs]
    p2 = [_dot3s(lhs[p], _split2(s_prev[p]), NT, ps['state_read']) for p in pairs]
    nm = [_split2(jnp.where(strict, p1[p][:c2, :c2], 0.0)) for p in pairs]
    aak = [_split2(jnp.where(strict, p1[p][:c2, c2:], 0.0)) for p in pairs]
    m2 = [_split2(jnp.where(incl2, p1[p][c2:, :], 0.0)) for p in pairs]
    u = [p2[p][:c2] + _dot3s(aak[p], vm[p], NN, ps['av']) for p in pairs]
    n = 1
    while 2 * n < chunk:
        us = [_split2(u[p]) for p in pairs]
        y = [_dot3s(nm[p], cat2(nm[p], us[p], 1), NN, ps['solve']) for p in pairs]
        u = [u[p] + y[p][:, c2:] for p in pairs]
        nm = [_split2(y[p][:, :c2]) for p in pairs]
        n *= 2
    us = [_split2(u[p]) for p in pairs]
    u = [u[p] + _dot3s(nm[p], us[p], NN, ps['solve']) for p in pairs]
    uv = [cat2(_split2(u[p]), vm[p], 0) for p in pairs]
    o = [p2[p][c2:] + _dot3s(m2[p], uv[p], NN, ps['out']) for p in pairs]
    for p, (bi, pi) in zip(pairs, units):
        s_ref[bi, pi] = (s_prev[p] + _dot3s(uv[p], rhs[p], TN, ps['update'])) * g_end[p]

    for p, sl in zip(pairs, sls):
        mu = jnp.sum(o[p], axis=-1, keepdims=True) * (1.0 / RWKV_N)
        d = jnp.where(own, o[p] - mu, 0.0)
        var = jnp.sum(d * d, axis=-1, keepdims=True) * (1.0 / RWKV_N)
        yn = d * lax.rsqrt(var + RWKV_GN_EPS)
        y = yn[:chunk] + yn[chunk:]
        y_ref[sl] = y * g_ref[:, sl[2]] + bb_ref[:, sl[2]] + bonus_ref[sl]

    @pl.when(c == pl.num_programs(1) - 1)
    def _():
        sout_ref[...] = s_ref[...]


def _rwkv_scan(r, lw, k, v, a, b, bonus, s0_pairs, ln_g, ln_b, *, nb):
    bsz, t, _ = r.shape
    n_pairs = RWKV_H // 2
    assert bsz % nb == 0
    blk = pl.BlockSpec((nb, CHUNK, RWKV_W), lambda bb, c: (bb, c, 0))
    st = pl.BlockSpec((nb, n_pairs, LANES, LANES), lambda bb, c: (bb, 0, 0, 0))
    vec = pl.BlockSpec((1, RWKV_W), lambda bb, c: (0, 0))
    return pl.pallas_call(
        functools.partial(_rwkv_scan_kernel, chunk=CHUNK, n_pairs=n_pairs, nb=nb),
        grid=(bsz // nb, t // CHUNK),
        in_specs=[blk] * 7 + [st, vec, vec],
        out_specs=[blk, st],
        out_shape=[jax.ShapeDtypeStruct((bsz, t, RWKV_W), F32),
                   jax.ShapeDtypeStruct((bsz, n_pairs, LANES, LANES), F32)],
        scratch_shapes=[pltpu.VMEM((nb, n_pairs, LANES, LANES), F32)],
        compiler_params=_params(("arbitrary", "arbitrary"), 40),
        name="rwkv_scan",
    )(r, lw, k, v, a, b, bonus, s0_pairs, ln_g, ln_b)


def _pair_states(s):
    bsz = s.shape[0]
    s = s.reshape(bsz, RWKV_H // 2, 2, RWKV_N, RWKV_N)
    z = jnp.zeros_like(s[:, :, 0])
    top = jnp.concatenate([s[:, :, 0], z], axis=-1)
    bot = jnp.concatenate([z, s[:, :, 1]], axis=-1)
    return jnp.concatenate([top, bot], axis=-2)


def _unpair_states(sp):
    bsz = sp.shape[0]
    even = sp[:, :, :RWKV_N, :RWKV_N]
    odd = sp[:, :, RWKV_N:, RWKV_N:]
    return jnp.stack([even, odd], axis=2).reshape(bsz, RWKV_H, RWKV_N, RWKV_N)


def _merge_out_kernel(x_ref, d_ref, r_ref, c_ref, gate_ref, w_ref, g_ref, o_ref, *, eps):
    gate = gate_ref[...]
    sg = gate * jax.nn.sigmoid(gate)
    nd, nr = d_ref.shape[1], r_ref.shape[1]
    acc = _dot((d_ref[...] * sg[:, :nd]).astype(BF16), w_ref[0:nd, :])
    acc += _dot((r_ref[...] * sg[:, nd:nd + nr]).astype(BF16), w_ref[nd:nd + nr, :])
    acc += _dot((c_ref[...] * sg[:, nd + nr:]).astype(BF16), w_ref[nd + nr:, :])
    x = x_ref[...] + acc
    o_ref[...] = x * lax.rsqrt(jnp.mean(x * x, axis=-1, keepdims=True) + eps) * g_ref[...]


def _merge_out(x, d_o, r_o, c_o, p2, w_out, norm_out, *, tm):
    m, d = x.shape
    row = lambda n: pl.BlockSpec((tm, n), lambda i: (i, 0))
    return pl.pallas_call(
        functools.partial(_merge_out_kernel, eps=1e-6),
        grid=(m // tm,),
        in_specs=[row(d), row(d_o.shape[1]), row(r_o.shape[1]), row(c_o.shape[1]),
                  pl.BlockSpec((tm, d), lambda i: (i, OFF_GATE // d)),
                  pl.BlockSpec((d, d), lambda i: (0, 0)),
                  pl.BlockSpec((1, d), lambda i: (0, 0))],
        out_specs=row(d),
        out_shape=jax.ShapeDtypeStruct((m, d), F32),
        compiler_params=_params(("arbitrary",), 56),
        name="merge_out",
    )(x, d_o, r_o, c_o, p2, w_out, norm_out)


REORDER_COLS = 512


def _cast_kernel(offs_ref, src_ref, dst_ref):
    dst_ref[...] = src_ref[...].astype(BF16)


def _reorder_in_cols(w):
    d = w.shape[0]
    dq, z, cq, gate = 0, 1536, 1536 + 3200, 1536 + 3200 + 512
    runs = [(gate, IN_COLS - gate), (z, 3 * RWKV_W), (dq, z), (cq, gate - cq), (z + 3 * RWKV_W, 2 * LORA)]
    offs = []
    for start, width in runs:
        offs += [start + c for c in range(0, width, REORDER_COLS)]
    assert all(wd % REORDER_COLS == 0 for _, wd in runs[:-1]) and offs[-1] + REORDER_COLS <= IN_COLS
    grid_spec = pltpu.PrefetchScalarGridSpec(
        num_scalar_prefetch=1,
        grid=(len(offs),),
        in_specs=[pl.BlockSpec((pl.Element(d), pl.Element(REORDER_COLS)), lambda i, offs: (0, offs[i] * LANES))],
        out_specs=pl.BlockSpec((d, REORDER_COLS), lambda i, offs: (0, i)),
    )
    return pl.pallas_call(
        _cast_kernel,
        grid_spec=grid_spec,
        out_shape=jax.ShapeDtypeStruct((d, IN_COLS), BF16),
        compiler_params=_params(("arbitrary",), 24),
        name="reorder_cast",
    )(jnp.asarray(np.array(offs, np.int32) // LANES), w)


def kernel(x_prompt, x_sample, cache_k, cache_v, cache_mem_k, cache_mem_v, state_rwkv, state_shift, page_table,
           mem_prompt, norm_in, w_in, norm_mem, w_mem_kv, lambda_qk, diff_subln, rwkv_mu, rwkv_w0, rwkv_w2, rwkv_a0,
           rwkv_a2, rwkv_k_k, rwkv_k_a, rwkv_r_k, rwkv_ln_g, rwkv_ln_b, w_out, norm_out):
    bsz, seq, d = x_prompt.shape
    db, ds, _ = x_sample.shape
    depth = w_in.shape[0]
    assert depth == 1
    l = 0
    lam_init = 0.8 - 0.6 * math.exp(-0.3 * l)
    n_mem = mem_prompt.shape[1]

    w_in_b = _reorder_in_cols(w_in[l])
    w_mem_b = w_mem_kv[l].astype(BF16)
    w_out_b = w_out[l].astype(BF16)
    g_in = norm_in[l][None]
    g_mem = norm_mem[l][None]
    g_out = norm_out[None]
    mu = rwkv_mu[l]
    mus = (mu[None, :RWKV_W], mu[None, RWKV_W:2 * RWKV_W], mu[None, 2 * RWKV_W:3 * RWKV_W], mu[None, 3 * RWKV_W:])
    zl = jnp.zeros((LORA, RWKV_W), F32)
    w2b = jnp.concatenate([rwkv_w2[l], zl], axis=0).astype(BF16)
    a2b = jnp.concatenate([zl, rwkv_a2[l]], axis=0).astype(BF16)
    vec = lambda x: x.reshape(1, RWKV_W)
    w0, a0, k_k, k_a, r_k = vec(rwkv_w0[l]), vec(rwkv_a0[l]), vec(rwkv_k_k[l]), vec(rwkv_k_a[l]), vec(rwkv_r_k[l])
    ln_g, ln_b = vec(rwkv_ln_g[l]), vec(rwkv_ln_b[l])
    slopes = 2.0 ** (-8.0 * np.arange(1, DIFF_H + 1, dtype=np.float64) / DIFF_H)
    subln = diff_subln[l][None]
    lq = lambda_qk[l]

    xp2 = x_prompt.reshape(bsz * seq, d)
    p2 = _norm_matmul(xp2, g_in, w_in_b, tm=512, tn=IN_COLS // 3, vmem_mib=56)
    p3 = p2.reshape(bsz, seq, IN_COLS)
    hs_p = _rmsnorm_rows(x_prompt[:, -1], g_in)
    d_o, k_p, v_p = _diff_attn(p3, slopes, lq, subln.reshape(DIFF_DV, 1), tq=1024, qw=256, lam_init=lam_init)
    zeros_first = (jnp.zeros((bsz, 1, RWKV_W), F32),) * 3 + (jnp.zeros((bsz, 1, LANES), F32),)
    s0_p = jnp.zeros((bsz, RWKV_H // 2, LANES, LANES), F32)
    r_o, sp_p = _rwkv(p3, zeros_first, mus, w0, w2b, a0, a2b, k_k, k_a, r_k, s0_p, ln_g, ln_b, nb=2)
    mem_kv = _norm_matmul(mem_prompt.reshape(bsz * n_mem, d), g_mem, w_mem_b, tm=256, tn=512,
                          vmem_mib=32).reshape(bsz, n_mem, 2 * CROSS_H * CROSS_DH)
    c_o = _cross_attn(p3, OFF_CQ, mem_kv, 0, mem_kv, CROSS_H * CROSS_DH, tq=1024)
    y_p = _merge_out(xp2, d_o.reshape(bsz * seq, -1), r_o.reshape(bsz * seq, -1), c_o.reshape(bsz * seq, -1), p2,
                     w_out_b, g_out, tm=512).reshape(bsz, seq, d)
    nk_p = k_p.reshape(1, bsz, seq, DIFF_H, DIFF_DV)
    nv_p = v_p.reshape(1, bsz, seq, DIFF_H, DIFF_DV)
    nmk_p = mem_kv[:, :, :CROSS_H * CROSS_DH].reshape(1, bsz, n_mem, CROSS_H, CROSS_DH)
    nmv_p = mem_kv[:, :, CROSS_H * CROSS_DH:].reshape(1, bsz, n_mem, CROSS_H, CROSS_DH)
    ns_p = _unpair_states(sp_p)[None]

    xs2 = x_sample.reshape(db * ds, d)
    rows = _pad_rows(jnp.concatenate([xs2, state_shift[l]], axis=0), 16)
    ps_all = _norm_matmul(rows, g_in, w_in_b, tm=rows.shape[0], tn=IN_COLS // 3, vmem_mib=56, n_norm=db * ds)
    ps2 = ps_all[:db * ds]
    ps3 = ps2.reshape(db, ds, IN_COLS)
    hs_s = _rmsnorm_rows(x_sample[:, -1], g_in)
    z_first = ps_all[db * ds:db * ds + db, None, :]
    firsts = (z_first[..., OFF_R:OFF_K], z_first[..., OFF_K:OFF_V], z_first[..., OFF_V:OFF_DQ], z_first[..., OFF_LORA:])

    q5 = ps3[:, :, OFF_DQ:OFF_DK].reshape(db, ds, DIFF_H, 2, DIFF_DH)
    zq = jnp.zeros_like(q5[..., 0, :])
    qm = jnp.stack([jnp.concatenate([q5[..., 0, :], zq], -1), jnp.concatenate([zq, q5[..., 1, :]], -1)], axis=1)
    qm = qm.transpose(0, 1, 3, 2, 4).reshape(db, 2 * DIFF_H * ds, DIFF_DV)
    k_new = ps3[:, :, OFF_DK:OFF_DV].reshape(db, ds * DIFF_H, DIFF_DV)
    v_new = ps3[:, :, OFF_DV:OFF_CQ].reshape(db, ds * DIFF_H, DIFF_DV)
    row_head = (np.arange(2 * DIFF_H * ds) // ds) % DIFF_H
    slope_rows = jnp.asarray(np.broadcast_to(slopes[row_head][:, None], (2 * DIFF_H * ds, LANES)).astype(np.float32))
    n_pool, page = cache_k.shape[1], cache_k.shape[2]
    ck = cache_k[l].reshape(n_pool, page * DIFF_H, DIFF_DV)
    cv = cache_v[l].reshape(n_pool, page * DIFF_H, DIFF_DV)
    ds_o = _paged_diff_attn(qm, ck, cv, k_new, v_new, page_table, slope_rows, lq, subln, n_pages=16,
                            lam_init=lam_init)
    ds_o = ds_o.reshape(db, DIFF_H, ds, DIFF_DV).transpose(0, 2, 1, 3).reshape(db * ds, DIFF_H * DIFF_DV)

    ps3_pad = jnp.pad(ps3, ((0, 0), (0, CHUNK - ds), (0, 0)))
    rs_o, sp_s = _rwkv(ps3_pad, firsts, mus, w0, w2b, a0, a2b, k_k, k_a, r_k, _pair_states(state_rwkv[l]), ln_g, ln_b,
                       nb=2, valid=ds)
    rs_o = rs_o[:, :ds].reshape(db * ds, RWKV_W)

    cq_pad = jnp.pad(ps3[:, :, OFF_CQ:OFF_LORA], ((0, 0), (0, 16 - ds), (0, 0)))
    mk_s = cache_mem_k[l].reshape(db, n_mem, CROSS_H * CROSS_DH)
    mv_s = cache_mem_v[l].reshape(db, n_mem, CROSS_H * CROSS_DH)
    cs_o = _cross_attn(cq_pad, 0, mk_s, 0, mv_s, 0, tq=16)[:, :ds].reshape(db * ds, CROSS_H * CROSS_DH)
    y_s = _merge_out(xs2, ds_o, rs_o, cs_o, ps2, w_out_b, g_out, tm=db * ds).reshape(db, ds, d)
    nk_s = ps3[:, :, OFF_DK:OFF_DV].reshape(1, db, ds, DIFF_H, DIFF_DV)
    nv_s = ps3[:, :, OFF_DV:OFF_CQ].reshape(1, db, ds, DIFF_H, DIFF_DV)
    ns_s = _unpair_states(sp_s)[None]

    return (y_p, y_s, nk_p, nv_p, nmk_p, nmv_p, ns_p, hs_p[None], nk_s, nv_s, ns_s, hs_s[None])
```

```python
import functools
import math

import jax
import jax.numpy as jnp
import numpy as np
from jax import lax
from jax.experimental import pallas as pl
from jax.experimental.pallas import tpu as pltpu

F32 = jnp.float32
BF16 = jnp.bfloat16

LANES = 128
MIB = 1024 * 1024

DIFF_H = 4
DIFF_DV = 128
DIFF_DH = 64
RWKV_N = 64
RWKV_H = 16
RWKV_W = RWKV_H * RWKV_N
CROSS_H = 4
CROSS_DH = 128
LORA = 64
RWKV_GN_EPS = 64e-5
CHUNK = 64
NEG = -1e30

OFF_GATE = 0
OFF_R = 2048
OFF_K = 3072
OFF_V = 4096
OFF_DQ = 5120
OFF_DK = 5632
OFF_DV = 6144
OFF_CQ = 6656
OFF_LORA = 7168
IN_COLS = 7296


def _params(sem, vmem_mib):
    return pltpu.CompilerParams(dimension_semantics=sem, vmem_limit_bytes=vmem_mib * MIB)


def _dot(a, b, dims=(((1,), (0,)), ((), ()))):
    return lax.dot_general(a, b, dims, preferred_element_type=F32)


NN = (((1,), (0,)), ((), ()))
NT = (((1,), (1,)), ((), ()))
TN = (((0,), (0,)), ((), ()))


def _norm_matmul_kernel(x_ref, g_ref, w_ref, o_ref, *rest, n_norm, eps, head_cols, tn):
    head_refs, h_ref = rest[:-1], rest[-1]

    @pl.when(pl.program_id(1) == 0)
    def _():
        x = x_ref[:n_norm, :]
        h_ref[:n_norm, :] = (x * lax.rsqrt(jnp.mean(x * x, axis=-1, keepdims=True) + eps) * g_ref[...]).astype(BF16)
        if n_norm < x_ref.shape[0]:
            h_ref[n_norm:, :] = x_ref[n_norm:, :].astype(BF16)

    out = jnp.dot(h_ref[...], w_ref[...], preferred_element_type=F32)
    o_ref[...] = out
    for ref, col in zip(head_refs, head_cols):
        @pl.when(pl.program_id(1) == col // tn)
        def _(ref=ref, col=col):
            c0 = col % tn
            ref[...] = out[:, c0:c0 + ref.shape[1] * ref.shape[2]].reshape(ref.shape)


def _norm_matmul(x, g, w, *, tm, tn, vmem_mib, n_norm=None, head_cols=()):
    m, d = x.shape
    n = w.shape[1]
    width = DIFF_H * DIFF_DV
    assert m % tm == 0 and n % tn == 0 and (n_norm is None or m == tm)
    assert all(c // tn == (c + width - 1) // tn and c % LANES == 0 for c in head_cols)
    head_spec = pl.BlockSpec((tm, DIFF_H, DIFF_DV), lambda i, j: (i, 0, 0))
    outs = pl.pallas_call(
        functools.partial(_norm_matmul_kernel, n_norm=tm if n_norm is None else n_norm, eps=1e-6,
                          head_cols=tuple(head_cols), tn=tn),
        grid=(m // tm, n // tn),
        in_specs=[pl.BlockSpec((tm, d), lambda i, j: (i, 0)),
                  pl.BlockSpec((1, d), lambda i, j: (0, 0)),
                  pl.BlockSpec((d, tn), lambda i, j: (0, j))],
        out_specs=[pl.BlockSpec((tm, tn), lambda i, j: (i, j))] + [head_spec] * len(head_cols),
        out_shape=[jax.ShapeDtypeStruct((m, n), F32)]
        + [jax.ShapeDtypeStruct((m, DIFF_H, DIFF_DV), F32)] * len(head_cols),
        scratch_shapes=[pltpu.VMEM((tm, d), BF16)],
        compiler_params=_params(("arbitrary", "arbitrary"), vmem_mib),
        name="norm_matmul",
    )(x, g, w)
    return outs if head_cols else outs[0]


def _rmsnorm_kernel(x_ref, g_ref, o_ref, *, eps):
    x = x_ref[...]
    o_ref[...] = x * lax.rsqrt(jnp.mean(x * x, axis=-1, keepdims=True) + eps) * g_ref[...]


def _pad_rows(x, mult):
    pad = -x.shape[0] % mult
    return jnp.pad(x, ((0, pad), (0, 0))) if pad else x


def _rmsnorm_rows(x, g):
    n = x.shape[0]
    xp = _pad_rows(x, 8)
    return pl.pallas_call(
        functools.partial(_rmsnorm_kernel, eps=1e-6),
        out_shape=jax.ShapeDtypeStruct(xp.shape, F32),
        name="rmsnorm_rows",
    )(xp, g)[:n]


def _diff_lambda(lq, lam_init):
    t1 = jnp.sum(lq[0:1] * lq[1:2], axis=-1, keepdims=True)
    t2 = jnp.sum(lq[2:3] * lq[3:4], axis=-1, keepdims=True)
    return jnp.exp(t1) - jnp.exp(t2) + lam_init


def _diff_finish(acc, l, lq, subln, rows, lam_init):
    o12 = acc / l
    o = o12[:rows] - _diff_lambda(lq, lam_init) * o12[rows:]
    o = o * lax.rsqrt(jnp.mean(o * o, axis=-1, keepdims=True) + 1e-5) * subln
    return o * (1.0 - lam_init)


LOG2E = math.log2(math.e)
POS_RADIX = 16
N_SLOPE_PIECES = 3
SCORE_LOOKAHEAD = 8


def _alibi_lanes(slopes, tq):
    assert tq <= POS_RADIX * 256
    c = np.asarray(slopes, np.float64) * LOG2E
    pieces, rest = [], c
    for _ in range(N_SLOPE_PIECES):
        piece = rest.astype(np.float32).astype(BF16).astype(np.float64)
        pieces.append(piece)
        rest = rest - piece
    q_lanes = np.zeros((len(c), 1, LANES), np.float32)
    k_lanes = np.zeros((tq, LANES), np.float32)
    pos = np.arange(tq)
    for n, piece in enumerate(pieces):
        q_lanes[:, 0, 2 * n] = POS_RADIX * piece
        q_lanes[:, 0, 2 * n + 1] = piece
        k_lanes[:, 2 * n] = pos // POS_RADIX
        k_lanes[:, 2 * n + 1] = pos % POS_RADIX
    c_sum = np.broadcast_to(sum(pieces)[:, None, None], (len(c), 1, LANES)).astype(np.float32)
    return jnp.asarray(q_lanes), jnp.asarray(k_lanes.astype(BF16)), jnp.asarray(c_sum)


def _stack_maps(q, q_lanes):
    q = q * (DIFF_DH ** -0.5 * LOG2E)
    lane = lax.broadcasted_iota(jnp.int32, q.shape, 1)
    q1 = jnp.where(lane < DIFF_DH, q, 0.0)
    q2 = jnp.where(lane >= DIFF_DH, q, 0.0)
    qq = jnp.concatenate([q1, q2], axis=0)
    return jnp.concatenate([qq, jnp.broadcast_to(q_lanes, qq.shape)], axis=1).astype(BF16)


def _diff_attn_kernel(qi_ref, kj_ref, q_ref, k_ref, v_ref, qlane_ref, klane_ref, slope_ref, lq_ref, subln_ref,
                      o_ref, qq_ref, m_ref, l_ref, acc_ref, *, tq, qw, lam_init):
    t = pl.program_id(2)
    i = qi_ref[t]
    j = kj_ref[t]

    @pl.when(j == 0)
    def _():
        qq_ref[...] = _stack_maps(q_ref[0], qlane_ref[0])
        m_ref[...] = jnp.full(m_ref.shape, NEG, F32)
        l_ref[...] = jnp.zeros(l_ref.shape, F32)
        acc_ref[...] = jnp.zeros(acc_ref.shape, F32)

    kb = jnp.concatenate([k_ref[0].astype(BF16), klane_ref[...]], axis=1)
    vb = v_ref[0].astype(BF16)
    block_off = slope_ref[0][:, :1] * jnp.full((1, 1), (j - i) * tq, jnp.int32).astype(F32)

    def accumulate(diagonal):
        starts = list(range(0, 2 * tq, qw))
        tiles = [pl.ds(c0, qw) for c0 in starts]
        n_keys = [c0 % tq + qw if diagonal else tq for c0 in starts]
        def score_tile(t):
            s = _dot(kb[:n_keys[t]], qq_ref[tiles[t], :], NT)
            if diagonal:
                qry = starts[t] % tq + lax.broadcasted_iota(jnp.int32, s.shape, 1)
                s = jnp.where(lax.broadcasted_iota(jnp.int32, s.shape, 0) <= qry, s, NEG)
            return s

        def softmax_tile(t, s):
            cols = tiles[t]
            m_prev = m_ref[:, cols] - block_off
            m_new = jnp.maximum(m_prev, jnp.max(s, axis=0, keepdims=True))
            alpha = jnp.exp2(m_prev - m_new)
            p = jnp.exp2(s - m_new)
            l_ref[:, cols] = alpha * l_ref[:, cols] + jnp.sum(p, axis=0, keepdims=True)
            acc_ref[:, cols] = alpha * acc_ref[:, cols] + _dot(vb[:n_keys[t]], p.astype(BF16), TN)
            m_ref[:, cols] = m_new + block_off

        scores = {}
        for t in range(len(tiles) + SCORE_LOOKAHEAD):
            if t < len(tiles):
                scores[t] = score_tile(t)
            if t >= SCORE_LOOKAHEAD:
                softmax_tile(t - SCORE_LOOKAHEAD, scores.pop(t - SCORE_LOOKAHEAD))

    @pl.when(j < i)
    def _():
        accumulate(False)

    @pl.when(j == i)
    def _():
        accumulate(True)
        o12 = acc_ref[...] / l_ref[...]
        o = o12[:, :tq] - _diff_lambda(lq_ref[...], lam_init) * o12[:, tq:]
        o = o * lax.rsqrt(jnp.mean(o * o, axis=0, keepdims=True) + 1e-5) * subln_ref[...]
        o_ref[0] = (o * (1.0 - lam_init)).T


def _diff_attn(p3, slopes, lambda_qk, subln_col, *, tq, qw, lam_init):
    b, t, _ = p3.shape
    nq = t // tq
    q_lanes, k_lanes, c_sum = _alibi_lanes(slopes, tq)
    pairs = [(i, j) for i in range(nq) for j in range(i + 1)]
    qi = jnp.asarray(np.array([p[0] for p in pairs], np.int32))
    kj = jnp.asarray(np.array([p[1] for p in pairs], np.int32))
    qb, kb, vb = OFF_DQ // DIFF_DV, OFF_DK // DIFF_DV, OFF_DV // DIFF_DV
    grid_spec = pltpu.PrefetchScalarGridSpec(
        num_scalar_prefetch=2,
        grid=(b, DIFF_H, len(pairs)),
        in_specs=[pl.BlockSpec((1, tq, DIFF_DV), lambda bb, h, s, qi, kj: (bb, qi[s], qb + h)),
                  pl.BlockSpec((1, tq, DIFF_DV), lambda bb, h, s, qi, kj: (bb, kj[s], kb + h)),
                  pl.BlockSpec((1, tq, DIFF_DV), lambda bb, h, s, qi, kj: (bb, kj[s], vb + h)),
                  pl.BlockSpec((1, 1, LANES), lambda bb, h, s, qi, kj: (h, 0, 0)),
                  pl.BlockSpec((tq, LANES), lambda bb, h, s, qi, kj: (0, 0)),
                  pl.BlockSpec((1, 1, LANES), lambda bb, h, s, qi, kj: (h, 0, 0)),
                  pl.BlockSpec((4, DIFF_DH), lambda bb, h, s, qi, kj: (0, 0)),
                  pl.BlockSpec((DIFF_DV, 1), lambda bb, h, s, qi, kj: (0, 0))],
        out_specs=pl.BlockSpec((1, tq, DIFF_DV), lambda bb, h, s, qi, kj: (bb, qi[s], h)),
        scratch_shapes=[pltpu.VMEM((2 * tq, 2 * DIFF_DV), BF16), pltpu.VMEM((1, 2 * tq), F32),
                        pltpu.VMEM((1, 2 * tq), F32), pltpu.VMEM((DIFF_DV, 2 * tq), F32)],
    )
    return pl.pallas_call(
        functools.partial(_diff_attn_kernel, tq=tq, qw=qw, lam_init=lam_init),
        grid_spec=grid_spec,
        out_shape=jax.ShapeDtypeStruct((b, t, DIFF_H * DIFF_DV), F32),
        compiler_params=_params(("arbitrary", "arbitrary", "arbitrary"), 40),
        name="diff_attn",
    )(qi, kj, p3, p3, p3, q_lanes, k_lanes, c_sum, lambda_qk, subln_col)


def _paged_diff_attn_kernel(pt_ref, q_ref, *refs, n_pages, page, n_new, past_len, lam_init):
    k_refs = refs[:n_pages]
    v_refs = refs[n_pages:2 * n_pages]
    kn_ref, vn_ref, slope_ref, lq_ref, subln_ref, o_ref, m_ref, l_ref, acc_ref = refs[2 * n_pages:]
    s_id = pl.program_id(1)
    rows = 2 * DIFF_H * n_new

    @pl.when(s_id == 0)
    def _():
        m_ref[...] = jnp.full(m_ref.shape, NEG, F32)
        l_ref[...] = jnp.zeros(l_ref.shape, F32)
        acc_ref[...] = jnp.zeros(acc_ref.shape, F32)

    qb = (q_ref[0] * (DIFF_DH ** -0.5)).astype(BF16)
    slope = slope_ref[...][:, :1]
    log_new, log_h = n_new.bit_length() - 1, DIFF_H.bit_length() - 1

    def coords(ncol):
        row = lax.broadcasted_iota(jnp.int32, (rows, ncol), 0)
        col = lax.broadcasted_iota(jnp.int32, (rows, ncol), 1)
        row_h = lax.shift_right_logical(row, log_new) & (DIFF_H - 1)
        return row & (n_new - 1), (col & (DIFF_H - 1)) == row_h, lax.shift_right_logical(col, log_h)

    def update(ss, vbs, carry):
        m_prev, l_prev, acc_prev = carry
        m_new = m_prev
        for s in ss:
            m_new = jnp.maximum(m_new, jnp.max(s, axis=-1, keepdims=True))
        alpha = jnp.exp(m_prev - m_new)
        l_new = alpha * l_prev
        acc_new = alpha * acc_prev
        for s, vb in zip(ss, vbs):
            p = jnp.exp(s - m_new)
            l_new = l_new + jnp.sum(p, axis=-1, keepdims=True)
            acc_new = acc_new + _dot(p.astype(BF16), vb)
        return m_new, l_new, acc_new

    _, same_head, tok = coords(page * DIFF_H)
    ss, vbs = [], []
    for i in range(n_pages):
        kb = k_refs[i][0].astype(BF16)
        vbs.append(v_refs[i][0].astype(BF16))
        k_pos = ((s_id * n_pages + i) * page - past_len + tok).astype(F32)
        ss.append(jnp.where(same_head, _dot(qb, kb, NT) + slope * k_pos, NEG))
    m_ref[...], l_ref[...], acc_ref[...] = update(ss, vbs, (m_ref[...], l_ref[...], acc_ref[...]))

    @pl.when(s_id == pl.num_programs(1) - 1)
    def _():
        kb = kn_ref[0].astype(BF16)
        vb = vn_ref[0].astype(BF16)
        row_tn, same_head_n, tok_n = coords(n_new * DIFF_H)
        k_pos = tok_n.astype(F32)
        s = jnp.where(same_head_n & (tok_n <= row_tn), _dot(qb, kb, NT) + slope * k_pos, NEG)
        _, l, acc = update([s], [vb], (m_ref[...], l_ref[...], acc_ref[...]))
        o_ref[0] = _diff_finish(acc, l, lq_ref[...], subln_ref[...], rows // 2, lam_init)


def _paged_diff_attn(qm, cache_k, cache_v, k_new, v_new, page_table, slope_rows, lambda_qk, subln, *,
                     n_pages, lam_init):
    db, rows, _ = qm.shape
    n_pool, page = cache_k.shape[0], cache_k.shape[1] // DIFF_H
    n_tab = page_table.shape[1]
    n_new = rows // (2 * DIFF_H)
    assert n_tab % n_pages == 0 and n_new & (n_new - 1) == 0 and DIFF_H & (DIFF_H - 1) == 0

    def page_spec(i):
        return pl.BlockSpec((1, page * DIFF_H, DIFF_DV), lambda b, s, pt: (pt[b, s * n_pages + i], 0, 0))

    grid_spec = pltpu.PrefetchScalarGridSpec(
        num_scalar_prefetch=1,
        grid=(db, n_tab // n_pages),
        in_specs=([pl.BlockSpec((1, rows, DIFF_DV), lambda b, s, pt: (b, 0, 0))]
                  + [page_spec(i) for i in range(n_pages)] * 2
                  + [pl.BlockSpec((1, n_new * DIFF_H, DIFF_DV), lambda b, s, pt: (b, 0, 0)),
                     pl.BlockSpec((1, n_new * DIFF_H, DIFF_DV), lambda b, s, pt: (b, 0, 0)),
                     pl.BlockSpec((rows, LANES), lambda b, s, pt: (0, 0)),
                     pl.BlockSpec((4, DIFF_DH), lambda b, s, pt: (0, 0)),
                     pl.BlockSpec((1, DIFF_DV), lambda b, s, pt: (0, 0))]),
        out_specs=pl.BlockSpec((1, rows // 2, DIFF_DV), lambda b, s, pt: (b, 0, 0)),
        scratch_shapes=[pltpu.VMEM((rows, 1), F32), pltpu.VMEM((rows, 1), F32), pltpu.VMEM((rows, DIFF_DV), F32)],
    )
    return pl.pallas_call(
        functools.partial(_paged_diff_attn_kernel, n_pages=n_pages, page=page, n_new=n_new,
                          past_len=n_tab * page, lam_init=lam_init),
        grid_spec=grid_spec,
        out_shape=jax.ShapeDtypeStruct((db, rows // 2, DIFF_DV), F32),
        compiler_params=_params(("arbitrary", "arbitrary"), 32),
        name="paged_diff_attn",
    )(page_table, qm, *([cache_k] * n_pages), *([cache_v] * n_pages), k_new, v_new, slope_rows, lambda_qk, subln)


def _cross_attn_kernel(q_ref, k_ref, v_ref, o_ref, *, scale):
    for h in range(CROSS_H):
        cols = pl.ds(h * CROSS_DH, CROSS_DH)
        q = q_ref[0, :, cols].astype(BF16)
        k = k_ref[0, :, cols].astype(BF16)
        v = v_ref[0, :, cols].astype(BF16)
        s = _dot(q, k, NT) * scale
        p = jnp.exp(s - jnp.max(s, axis=-1, keepdims=True))
        l = jnp.sum(p, axis=-1, keepdims=True)
        o_ref[0, :, cols] = _dot(p.astype(BF16), v) / l


def _cross_attn(q_arr, q_off, k_arr, k_off, v_arr, v_off, *, tq):
    b, t, _ = q_arr.shape
    n_mem = k_arr.shape[1]
    width = CROSS_H * CROSS_DH
    qb, kb, vb = q_off // width, k_off // width, v_off // width
    return pl.pallas_call(
        functools.partial(_cross_attn_kernel, scale=CROSS_DH ** -0.5),
        grid=(b, t // tq),
        in_specs=[pl.BlockSpec((1, tq, width), lambda bb, i: (bb, i, qb)),
                  pl.BlockSpec((1, n_mem, width), lambda bb, i: (bb, 0, kb)),
                  pl.BlockSpec((1, n_mem, width), lambda bb, i: (bb, 0, vb))],
        out_specs=pl.BlockSpec((1, tq, width), lambda bb, i: (bb, i, 0)),
        out_shape=jax.ShapeDtypeStruct((b, t, width), F32),
        compiler_params=_params(("arbitrary", "arbitrary"), 32),
        name="cross_attn",
    )(q_arr, k_arr, v_arr)


def _rwkv_step(r_ref, k_ref, v_ref, lo_ref, fr_ref, fk_ref, fv_ref, fl_ref,
               mur_ref, muk_ref, muv_ref, mul_ref, w0_ref, w2_ref, a0_ref, a2_ref, kk_ref, ka_ref, rk_ref,
               s0_ref, g_ref, bb_ref, y_ref, sout_ref,
               lhs_s, rhs_s, vm_s, gend_s, bonus_s, s_ref, cr_ref, ck_ref, cv_ref, cl_ref,
               *, chunk, n_pairs, nb, n_chunks, valid, slot_w, slot_r, ahead):
    c = pl.program_id(1)
    c2 = 2 * chunk

    rid = lax.broadcasted_iota(jnp.int32, (c2, c2), 0)
    cid = lax.broadcasted_iota(jnp.int32, (c2, c2), 1)
    strict = rid > cid
    incl2 = jnp.concatenate([rid >= cid, rid >= cid], axis=1)
    tr = lax.broadcasted_iota(jnp.int32, (chunk, chunk), 0)
    tc = lax.broadcasted_iota(jnp.int32, (chunk, chunk), 1)
    tri = (tr >= tc).astype(BF16)
    srow = lax.broadcasted_iota(jnp.int32, (c2, LANES), 0)
    slane = lax.broadcasted_iota(jnp.int32, (c2, LANES), 1)
    own = (srow < chunk) == (slane < RWKV_N)
    low_half = lax.broadcasted_iota(jnp.int32, (chunk, LANES), 1) < RWKV_N

    def stack(x):
        return jnp.where(own, jnp.concatenate([x, x], axis=0), 0.0).astype(BF16)

    def head_sum(x):
        s_lo = jnp.sum(jnp.where(low_half, x, 0.0), axis=-1, keepdims=True)
        s_hi = jnp.sum(jnp.where(low_half, 0.0, x), axis=-1, keepdims=True)
        return jnp.where(low_half, s_lo, s_hi)

    row_id = lax.broadcasted_iota(jnp.int32, (chunk, 1), 0)

    def prepare():
        for bi in range(nb):
            def shift_mix(x_ref, carry_ref, mu_ref):
                x = x_ref[bi]
                prev = pltpu.roll(x, 1, axis=0)
                prev = jnp.where(lax.broadcasted_iota(jnp.int32, x.shape, 0) == 0, carry_ref[bi], prev)
                carry_ref[bi] = x[chunk - 1:chunk, :]
                return x + mu_ref[...] * (prev - x)

            r = shift_mix(r_ref, cr_ref, mur_ref)
            k = shift_mix(k_ref, ck_ref, muk_ref)
            v = shift_mix(v_ref, cv_ref, muv_ref)
            lo = shift_mix(lo_ref, cl_ref, mul_ref)
            yield
            xw = -(w0_ref[...] + _dot(jnp.tanh(lo).astype(BF16), w2_ref[...]))
            softplus = jnp.maximum(xw, 0.0) + jnp.log(1.0 + jnp.exp(-jnp.abs(xw)))
            lw = -jnp.exp(-softplus - 0.5)
            a = jax.nn.sigmoid(a0_ref[...] + _dot(lo.astype(BF16), a2_ref[...]))
            kkr = k * kk_ref[...]
            k2 = k * (1.0 + (a - 1.0) * ka_ref[...])
            rb = r * k2 * rk_ref[...]
            if valid is not None:
                ok = jnp.minimum(c, n_chunks - 1) * chunk + row_id < valid
                lw, r, k2, v, kkr, rb = [jnp.where(ok, t, 0.0) for t in (lw, r, k2, v, kkr, rb)]
            yield
            for pi in range(n_pairs):
                cols = slice(pi * LANES, (pi + 1) * LANES)
                lw_p, kkr_p, v_p = lw[:, cols], kkr[:, cols], v[:, cols]
                kk = kkr_p / jnp.maximum(jnp.sqrt(head_sum(kkr_p * kkr_p)), 1e-12)
                h1 = lw_p.astype(BF16)
                r1 = lw_p - h1.astype(F32)
                h2 = r1.astype(BF16)
                h3 = (r1 - h2.astype(F32)).astype(BF16)
                cum = _dot(tri, h1) + (_dot(tri, h2) + _dot(tri, h3))
                g = jnp.exp(cum)
                gi = jnp.exp(-cum)
                gp = jnp.exp(cum - lw_p)
                lhs_s[slot_w, bi, pi] = jnp.concatenate([stack(-kk * gp), stack(r[:, cols] * g)], axis=0)
                rhs_s[slot_w, bi, pi] = jnp.concatenate([stack(kk * a[:, cols] * gi), stack(k2[:, cols] * gi)], axis=0)
                vm_s[slot_w, bi, pi] = stack(v_p)
                gend_s[slot_w, bi, :, cols] = g[chunk - 1:chunk, :]
                bonus_s[slot_w, bi, :, cols] = head_sum(rb[:, cols]) * v_p
                yield

    pieces = prepare()
    n_pieces = nb * (n_pairs + 2)
    n_ticks = (chunk.bit_length() + 4) * nb * n_pairs
    progress = [0, 0]

    def stage(fn):
        out = []
        for u in n_units:
            out.append(fn(u))
            progress[0] += 1
            while progress[1] < progress[0] * n_pieces // n_ticks:
                next(pieces, None)
                progress[1] += 1
        return out

    if not ahead:
        for _ in pieces:
            pass

    units = [(bi, pi) for bi in range(nb) for pi in range(n_pairs)]
    lhs = [lhs_s[slot_r, bi, pi] for bi, pi in units]
    rhs = [rhs_s[slot_r, bi, pi] for bi, pi in units]
    vm = [vm_s[slot_r, bi, pi] for bi, pi in units]
    s_prev = [s_ref[bi, pi] for bi, pi in units]
    n_units = range(len(units))
    p1 = stage(lambda u: _dot(lhs[u], rhs[u], NT))
    p2 = stage(lambda u: _dot(lhs[u], s_prev[u].astype(BF16), NT))
    nm = [jnp.where(strict, p1[u][:c2, :c2], 0.0).astype(BF16) for u in n_units]
    aak = [jnp.where(strict, p1[u][:c2, c2:], 0.0).astype(BF16) for u in n_units]
    m2 = [jnp.where(incl2, p1[u][c2:, :], 0.0).astype(BF16) for u in n_units]
    uu = stage(lambda u: p2[u][:c2] + _dot(aak[u], vm[u]))
    n = 1
    while 2 * n < chunk:
        y = stage(lambda u: _dot(nm[u], jnp.concatenate([nm[u], uu[u].astype(BF16)], axis=1)))
        uu = [uu[u] + y[u][:, c2:] for u in n_units]
        nm = [y[u][:, :c2].astype(BF16) for u in n_units]
        n *= 2
    uu = stage(lambda u: uu[u] + _dot(nm[u], uu[u].astype(BF16)))
    uv = [jnp.concatenate([uu[u].astype(BF16), vm[u]], axis=0) for u in n_units]
    o = stage(lambda u: p2[u][c2:] + _dot(m2[u], uv[u]))
    s_new = stage(lambda u: s_prev[u] + _dot(uv[u], rhs[u], TN))
    for _ in pieces:
        pass
    for u, (bi, pi) in zip(n_units, units):
        s_ref[bi, pi] = s_new[u] * gend_s[slot_r, bi, :, pl.ds(pi * LANES, LANES)]
    for u, (bi, pi) in zip(n_units, units):
        cols = pl.ds(pi * LANES, LANES)
        mu = jnp.sum(o[u], axis=-1, keepdims=True) * (1.0 / RWKV_N)
        d = jnp.where(own, o[u] - mu, 0.0)
        var = jnp.sum(d * d, axis=-1, keepdims=True) * (1.0 / RWKV_N)
        yn = d * lax.rsqrt(var + RWKV_GN_EPS)
        y_ref[bi, :, cols] = ((yn[:chunk] + yn[chunk:]) * g_ref[:, cols] + bb_ref[:, cols]
                              + bonus_s[slot_r, bi, :, cols])


RWKV_REFS = ("r_ref k_ref v_ref lo_ref fr_ref fk_ref fv_ref fl_ref mur_ref muk_ref muv_ref mul_ref w0_ref w2_ref a0_ref "
             "a2_ref kk_ref ka_ref rk_ref s0_ref g_ref bb_ref y_ref sout_ref lhs_s rhs_s vm_s gend_s bonus_s s_ref "
             "cr_ref ck_ref cv_ref cl_ref").split()


def _rwkv_kernel(*refs, n_chunks, **static):
    named = dict(zip(RWKV_REFS, refs, strict=True))
    fr_ref, fk_ref, fv_ref, fl_ref = (named[n] for n in ("fr_ref", "fk_ref", "fv_ref", "fl_ref"))
    cr_ref, ck_ref, cv_ref, cl_ref = (named[n] for n in ("cr_ref", "ck_ref", "cv_ref", "cl_ref"))
    lhs_s, rhs_s, vm_s, gend_s, bonus_s = (named[n] for n in ("lhs_s", "rhs_s", "vm_s", "gend_s", "bonus_s"))
    s0_ref, s_ref, sout_ref = named["s0_ref"], named["s_ref"], named["sout_ref"]
    c = pl.program_id(1)
    ahead = n_chunks > 1

    @pl.when(c == 0)
    def _():
        cr_ref[...] = fr_ref[...]
        ck_ref[...] = fk_ref[...]
        cv_ref[...] = fv_ref[...]
        cl_ref[...] = fl_ref[...]
        s_ref[...] = s0_ref[...]
        if ahead:
            lhs_s[1] = jnp.zeros(lhs_s.shape[1:], BF16)
            rhs_s[1] = jnp.zeros(rhs_s.shape[1:], BF16)
            vm_s[1] = jnp.zeros(vm_s.shape[1:], BF16)
            gend_s[1] = jnp.ones(gend_s.shape[1:], F32)
            bonus_s[1] = jnp.zeros(bonus_s.shape[1:], F32)

    if ahead:
        @pl.when(lax.rem(c, 2) == 0)
        def _():
            _rwkv_step(*refs, n_chunks=n_chunks, slot_w=0, slot_r=1, ahead=True, **static)

        @pl.when(lax.rem(c, 2) == 1)
        def _():
            _rwkv_step(*refs, n_chunks=n_chunks, slot_w=1, slot_r=0, ahead=True, **static)
    else:
        _rwkv_step(*refs, n_chunks=n_chunks, slot_w=0, slot_r=0, ahead=False, **static)

    @pl.when(c == (n_chunks if ahead else 0))
    def _():
        sout_ref[...] = s_ref[...]


def _rwkv(p3, firsts, mus, w0, w2b, a0, a2b, k_k, k_a, r_k, s0_pairs, ln_g, ln_b, *, nb, valid=None):
    bsz, t, _ = p3.shape
    n_pairs = RWKV_H // 2
    n_chunks = t // CHUNK
    assert bsz % nb == 0 and t % CHUNK == 0
    cur = lambda c: jnp.minimum(c, n_chunks - 1)
    wide = lambda blk: pl.BlockSpec((nb, CHUNK, RWKV_W), lambda bb, c: (bb, cur(c), blk))
    vec = lambda n: pl.BlockSpec((1, n), lambda bb, c: (0, 0))
    first = lambda n: pl.BlockSpec((nb, 1, n), lambda bb, c: (bb, 0, 0))
    st = pl.BlockSpec((nb, n_pairs, LANES, LANES), lambda bb, c: (bb, 0, 0, 0))
    in_specs = [wide(OFF_R // RWKV_W), wide(OFF_K // RWKV_W), wide(OFF_V // RWKV_W),
                pl.BlockSpec((nb, CHUNK, LANES), lambda bb, c: (bb, cur(c), OFF_LORA // LANES)),
                first(RWKV_W), first(RWKV_W), first(RWKV_W), first(LANES),
                vec(RWKV_W), vec(RWKV_W), vec(RWKV_W), vec(LANES),
                vec(RWKV_W), pl.BlockSpec((LANES, RWKV_W), lambda bb, c: (0, 0)),
                vec(RWKV_W), pl.BlockSpec((LANES, RWKV_W), lambda bb, c: (0, 0)),
                vec(RWKV_W), vec(RWKV_W), vec(RWKV_W),
                st, vec(RWKV_W), vec(RWKV_W)]
    lag = 1 if n_chunks > 1 else 0
    out_specs = [pl.BlockSpec((nb, CHUNK, RWKV_W), lambda bb, c: (bb, jnp.maximum(c - lag, 0), 0)), st]
    scratch = [pltpu.VMEM((2, nb, n_pairs, 4 * CHUNK, LANES), BF16), pltpu.VMEM((2, nb, n_pairs, 4 * CHUNK, LANES), BF16),
               pltpu.VMEM((2, nb, n_pairs, 2 * CHUNK, LANES), BF16), pltpu.VMEM((2, nb, 1, RWKV_W), F32),
               pltpu.VMEM((2, nb, CHUNK, RWKV_W), F32), pltpu.VMEM((nb, n_pairs, LANES, LANES), F32),
               pltpu.VMEM((nb, 1, RWKV_W), F32), pltpu.VMEM((nb, 1, RWKV_W), F32),
               pltpu.VMEM((nb, 1, RWKV_W), F32), pltpu.VMEM((nb, 1, LANES), F32)]
    return pl.pallas_call(
        functools.partial(_rwkv_kernel, chunk=CHUNK, n_pairs=n_pairs, nb=nb, n_chunks=n_chunks, valid=valid),
        grid=(bsz // nb, n_chunks + lag),
        in_specs=in_specs,
        out_specs=out_specs,
        out_shape=[jax.ShapeDtypeStruct((bsz, t, RWKV_W), F32),
                   jax.ShapeDtypeStruct((bsz, n_pairs, LANES, LANES), F32)],
        scratch_shapes=scratch,
        compiler_params=_params(("arbitrary", "arbitrary"), 48),
        name="rwkv",
    )(p3, p3, p3, p3, *firsts, *mus, w0, w2b, a0, a2b, k_k, k_a, r_k, s0_pairs, ln_g, ln_b)


def _pair_states(s):
    bsz = s.shape[0]
    s = s.reshape(bsz, RWKV_H // 2, 2, RWKV_N, RWKV_N)
    z = jnp.zeros_like(s[:, :, 0])
    top = jnp.concatenate([s[:, :, 0], z], axis=-1)
    bot = jnp.concatenate([z, s[:, :, 1]], axis=-1)
    return jnp.concatenate([top, bot], axis=-2)


def _unpair_states(sp):
    bsz = sp.shape[0]
    even = sp[:, :, :RWKV_N, :RWKV_N]
    odd = sp[:, :, RWKV_N:, RWKV_N:]
    return jnp.stack([even, odd], axis=2).reshape(bsz, RWKV_H, RWKV_N, RWKV_N)


def _merge_out_kernel(x_ref, d_ref, r_ref, c_ref, gate_ref, w_ref, g_ref, o_ref, *, eps):
    gate = gate_ref[...]
    sg = gate * jax.nn.sigmoid(gate)
    nd, nr = d_ref.shape[1], r_ref.shape[1]
    acc = _dot((d_ref[...] * sg[:, :nd]).astype(BF16), w_ref[0:nd, :])
    acc += _dot((r_ref[...] * sg[:, nd:nd + nr]).astype(BF16), w_ref[nd:nd + nr, :])
    acc += _dot((c_ref[...] * sg[:, nd + nr:]).astype(BF16), w_ref[nd + nr:, :])
    x = x_ref[...] + acc
    o_ref[...] = x * lax.rsqrt(jnp.mean(x * x, axis=-1, keepdims=True) + eps) * g_ref[...]


def _merge_out(x, d_o, r_o, c_o, p2, w_out, norm_out, *, tm):
    m, d = x.shape
    row = lambda n: pl.BlockSpec((tm, n), lambda i: (i, 0))
    return pl.pallas_call(
        functools.partial(_merge_out_kernel, eps=1e-6),
        grid=(m // tm,),
        in_specs=[row(d), row(d_o.shape[1]), row(r_o.shape[1]), row(c_o.shape[1]),
                  pl.BlockSpec((tm, d), lambda i: (i, OFF_GATE // d)),
                  pl.BlockSpec((d, d), lambda i: (0, 0)),
                  pl.BlockSpec((1, d), lambda i: (0, 0))],
        out_specs=row(d),
        out_shape=jax.ShapeDtypeStruct((m, d), F32),
        compiler_params=_params(("arbitrary",), 56),
        name="merge_out",
    )(x, d_o, r_o, c_o, p2, w_out, norm_out)


REORDER_COLS = 512


def _cast_kernel(offs_ref, src_ref, dst_ref):
    dst_ref[...] = src_ref[...].astype(BF16)


def _reorder_in_cols(w):
    d = w.shape[0]
    dq, z, cq, gate = 0, 1536, 1536 + 3200, 1536 + 3200 + 512
    runs = [(gate, IN_COLS - gate), (z, 3 * RWKV_W), (dq, z), (cq, gate - cq), (z + 3 * RWKV_W, 2 * LORA)]
    offs = []
    for start, width in runs:
        offs += [start + c for c in range(0, width, REORDER_COLS)]
    assert all(wd % REORDER_COLS == 0 for _, wd in runs[:-1]) and offs[-1] + REORDER_COLS <= IN_COLS
    grid_spec = pltpu.PrefetchScalarGridSpec(
        num_scalar_prefetch=1,
        grid=(len(offs),),
        in_specs=[pl.BlockSpec((pl.Element(d), pl.Element(REORDER_COLS)), lambda i, offs: (0, offs[i] * LANES))],
        out_specs=pl.BlockSpec((d, REORDER_COLS), lambda i, offs: (0, i)),
    )
    return pl.pallas_call(
        _cast_kernel,
        grid_spec=grid_spec,
        out_shape=jax.ShapeDtypeStruct((d, IN_COLS), BF16),
        compiler_params=_params(("arbitrary",), 24),
        name="reorder_cast",
    )(jnp.asarray(np.array(offs, np.int32) // LANES), w)


def kernel(x_prompt, x_sample, cache_k, cache_v, cache_mem_k, cache_mem_v, state_rwkv, state_shift, page_table,
           mem_prompt, norm_in, w_in, norm_mem, w_mem_kv, lambda_qk, diff_subln, rwkv_mu, rwkv_w0, rwkv_w2, rwkv_a0,
           rwkv_a2, rwkv_k_k, rwkv_k_a, rwkv_r_k, rwkv_ln_g, rwkv_ln_b, w_out, norm_out):
    bsz, seq, d = x_prompt.shape
    db, ds, _ = x_sample.shape
    depth = w_in.shape[0]
    assert depth == 1
    l = 0
    lam_init = 0.8 - 0.6 * math.exp(-0.3 * l)
    n_mem = mem_prompt.shape[1]

    w_in_b = _reorder_in_cols(w_in[l])
    w_mem_b = w_mem_kv[l].astype(BF16)
    w_out_b = w_out[l].astype(BF16)
    g_in = norm_in[l][None]
    g_mem = norm_mem[l][None]
    g_out = norm_out[None]
    mu = rwkv_mu[l]
    mus = (mu[None, :RWKV_W], mu[None, RWKV_W:2 * RWKV_W], mu[None, 2 * RWKV_W:3 * RWKV_W], mu[None, 3 * RWKV_W:])
    zl = jnp.zeros((LORA, RWKV_W), F32)
    w2b = jnp.concatenate([rwkv_w2[l], zl], axis=0).astype(BF16)
    a2b = jnp.concatenate([zl, rwkv_a2[l]], axis=0).astype(BF16)
    vec = lambda x: x.reshape(1, RWKV_W)
    w0, a0, k_k, k_a, r_k = vec(rwkv_w0[l]), vec(rwkv_a0[l]), vec(rwkv_k_k[l]), vec(rwkv_k_a[l]), vec(rwkv_r_k[l])
    ln_g, ln_b = vec(rwkv_ln_g[l]), vec(rwkv_ln_b[l])
    slopes = 2.0 ** (-8.0 * np.arange(1, DIFF_H + 1, dtype=np.float64) / DIFF_H)
    subln = diff_subln[l][None]
    lq = lambda_qk[l]

    xp2 = x_prompt.reshape(bsz * seq, d)
    p2, k4_p, v4_p = _norm_matmul(xp2, g_in, w_in_b, tm=512, tn=IN_COLS // 3, vmem_mib=60,
                                  head_cols=(OFF_DK, OFF_DV))
    p3 = p2.reshape(bsz, seq, IN_COLS)
    hs_p = _rmsnorm_rows(x_prompt[:, -1], g_in)
    d_o = _diff_attn(p3, slopes, lq, subln.reshape(DIFF_DV, 1), tq=1024, qw=256, lam_init=lam_init)
    zeros_first = (jnp.zeros((bsz, 1, RWKV_W), F32),) * 3 + (jnp.zeros((bsz, 1, LANES), F32),)
    s0_p = jnp.zeros((bsz, RWKV_H // 2, LANES, LANES), F32)
    r_o, sp_p = _rwkv(p3, zeros_first, mus, w0, w2b, a0, a2b, k_k, k_a, r_k, s0_p, ln_g, ln_b, nb=2)
    mem_kv = _norm_matmul(mem_prompt.reshape(bsz * n_mem, d), g_mem, w_mem_b, tm=256, tn=512,
                          vmem_mib=32).reshape(bsz, n_mem, 2 * CROSS_H * CROSS_DH)
    c_o = _cross_attn(p3, OFF_CQ, mem_kv, 0, mem_kv, CROSS_H * CROSS_DH, tq=1024)
    y_p = _merge_out(xp2, d_o.reshape(bsz * seq, -1), r_o.reshape(bsz * seq, -1), c_o.reshape(bsz * seq, -1), p2,
                     w_out_b, g_out, tm=512).reshape(bsz, seq, d)
    nk_p = k4_p.reshape(1, bsz, seq, DIFF_H, DIFF_DV)
    nv_p = v4_p.reshape(1, bsz, seq, DIFF_H, DIFF_DV)
    nmk_p = mem_kv[:, :, :CROSS_H * CROSS_DH].reshape(1, bsz, n_mem, CROSS_H, CROSS_DH)
    nmv_p = mem_kv[:, :, CROSS_H * CROSS_DH:].reshape(1, bsz, n_mem, CROSS_H, CROSS_DH)
    ns_p = _unpair_states(sp_p)[None]

    xs2 = x_sample.reshape(db * ds, d)
    rows = _pad_rows(jnp.concatenate([xs2, state_shift[l]], axis=0), 16)
    ps_all = _norm_matmul(rows, g_in, w_in_b, tm=rows.shape[0], tn=IN_COLS // 3, vmem_mib=56, n_norm=db * ds)
    ps2 = ps_all[:db * ds]
    ps3 = ps2.reshape(db, ds, IN_COLS)
    hs_s = _rmsnorm_rows(x_sample[:, -1], g_in)
    z_first = ps_all[db * ds:db * ds + db, None, :]
    firsts = (z_first[..., OFF_R:OFF_K], z_first[..., OFF_K:OFF_V], z_first[..., OFF_V:OFF_DQ], z_first[..., OFF_LORA:])

    q5 = ps3[:, :, OFF_DQ:OFF_DK].reshape(db, ds, DIFF_H, 2, DIFF_DH)
    zq = jnp.zeros_like(q5[..., 0, :])
    qm = jnp.stack([jnp.concatenate([q5[..., 0, :], zq], -1), jnp.concatenate([zq, q5[..., 1, :]], -1)], axis=1)
    qm = qm.transpose(0, 1, 3, 2, 4).reshape(db, 2 * DIFF_H * ds, DIFF_DV)
    k_new = ps3[:, :, OFF_DK:OFF_DV].reshape(db, ds * DIFF_H, DIFF_DV)
    v_new = ps3[:, :, OFF_DV:OFF_CQ].reshape(db, ds * DIFF_H, DIFF_DV)
    row_head = (np.arange(2 * DIFF_H * ds) // ds) % DIFF_H
    slope_rows = jnp.asarray(np.broadcast_to(slopes[row_head][:, None], (2 * DIFF_H * ds, LANES)).astype(np.float32))
    n_pool, page = cache_k.shape[1], cache_k.shape[2]
    ck = cache_k[l].reshape(n_pool, page * DIFF_H, DIFF_DV)
    cv = cache_v[l].reshape(n_pool, page * DIFF_H, DIFF_DV)
    ds_o = _paged_diff_attn(qm, ck, cv, k_new, v_new, page_table, slope_rows, lq, subln, n_pages=16,
                            lam_init=lam_init)
    ds_o = ds_o.reshape(db, DIFF_H, ds, DIFF_DV).transpose(0, 2, 1, 3).reshape(db * ds, DIFF_H * DIFF_DV)

    ps3_pad = jnp.pad(ps3, ((0, 0), (0, CHUNK - ds), (0, 0)))
    rs_o, sp_s = _rwkv(ps3_pad, firsts, mus, w0, w2b, a0, a2b, k_k, k_a, r_k, _pair_states(state_rwkv[l]), ln_g, ln_b,
                       nb=2, valid=ds)
    rs_o = rs_o[:, :ds].reshape(db * ds, RWKV_W)

    cq_pad = jnp.pad(ps3[:, :, OFF_CQ:OFF_LORA], ((0, 0), (0, 16 - ds), (0, 0)))
    mk_s = cache_mem_k[l].reshape(db, n_mem, CROSS_H * CROSS_DH)
    mv_s = cache_mem_v[l].reshape(db, n_mem, CROSS_H * CROSS_DH)
    cs_o = _cross_attn(cq_pad, 0, mk_s, 0, mv_s, 0, tq=16)[:, :ds].reshape(db * ds, CROSS_H * CROSS_DH)
    y_s = _merge_out(xs2, ds_o, rs_o, cs_o, ps2, w_out_b, g_out, tm=db * ds).reshape(db, ds, d)
    nk_s = ps3[:, :, OFF_DK:OFF_DV].reshape(1, db, ds, DIFF_H, DIFF_DV)
    nv_s = ps3[:, :, OFF_DV:OFF_CQ].reshape(1, db, ds, DIFF_H, DIFF_DV)
    ns_s = _unpair_states(sp_s)[None]

    return (y_p, y_s, nk_p, nv_p, nmk_p, nmv_p, ns_p, hs_p[None], nk_s, nv_s, ns_s, hs_s[None])
```

```python
import functools
import math

import jax
import jax.numpy as jnp
import numpy as np
from jax import lax
from jax.experimental import pallas as pl
from jax.experimental.pallas import tpu as pltpu

F32 = jnp.float32
BF16 = jnp.bfloat16

LANES = 128
MIB = 1024 * 1024

DIFF_H = 4
DIFF_DV = 128
DIFF_DH = 64
RWKV_N = 64
RWKV_H = 16
RWKV_W = RWKV_H * RWKV_N
CROSS_H = 4
CROSS_DH = 128
LORA = 64
RWKV_GN_EPS = 64e-5
CHUNK = 64
NEG = -1e30
MIX_DTYPE = BF16

OFF_GATE = 0
OFF_R = 2048
OFF_K = 3072
OFF_V = 4096
OFF_DQ = 5120
OFF_DK = 5632
OFF_DV = 6144
OFF_CQ = 6656
OFF_LORA = 7168
IN_COLS = 7296


def _params(sem, vmem_mib):
    return pltpu.CompilerParams(dimension_semantics=sem, vmem_limit_bytes=vmem_mib * MIB)


def _dot(a, b, dims=(((1,), (0,)), ((), ()))):
    return lax.dot_general(a, b, dims, preferred_element_type=F32)


NN = (((1,), (0,)), ((), ()))
NT = (((1,), (1,)), ((), ()))
TN = (((0,), (0,)), ((), ()))


def _norm_matmul_kernel(x_ref, g_ref, w_ref, o_ref, *rest, n_norm, eps, head_cols, tn):
    head_refs, h_ref = rest[:-1], rest[-1]

    @pl.when(pl.program_id(1) == 0)
    def _():
        x = x_ref[:n_norm, :]
        h_ref[:n_norm, :] = (x * lax.rsqrt(jnp.mean(x * x, axis=-1, keepdims=True) + eps) * g_ref[...]).astype(BF16)
        if n_norm < x_ref.shape[0]:
            h_ref[n_norm:, :] = x_ref[n_norm:, :].astype(BF16)

    out = jnp.dot(h_ref[...], w_ref[...], preferred_element_type=F32)
    o_ref[...] = out
    for ref, col in zip(head_refs, head_cols):
        @pl.when(pl.program_id(1) == col // tn)
        def _(ref=ref, col=col):
            c0 = col % tn
            ref[...] = out[:, c0:c0 + ref.shape[1] * ref.shape[2]].reshape(ref.shape)


def _norm_matmul(x, g, w, *, tm, tn, vmem_mib, n_norm=None, head_cols=()):
    m, d = x.shape
    n = w.shape[1]
    width = DIFF_H * DIFF_DV
    assert m % tm == 0 and n % tn == 0 and (n_norm is None or m == tm)
    assert all(c // tn == (c + width - 1) // tn and c % LANES == 0 for c in head_cols)
    head_spec = pl.BlockSpec((tm, DIFF_H, DIFF_DV), lambda i, j: (i, 0, 0))
    outs = pl.pallas_call(
        functools.partial(_norm_matmul_kernel, n_norm=tm if n_norm is None else n_norm, eps=1e-6,
                          head_cols=tuple(head_cols), tn=tn),
        grid=(m // tm, n // tn),
        in_specs=[pl.BlockSpec((tm, d), lambda i, j: (i, 0)),
                  pl.BlockSpec((1, d), lambda i, j: (0, 0)),
                  pl.BlockSpec((d, tn), lambda i, j: (0, j))],
        out_specs=[pl.BlockSpec((tm, tn), lambda i, j: (i, j))] + [head_spec] * len(head_cols),
        out_shape=[jax.ShapeDtypeStruct((m, n), F32)]
        + [jax.ShapeDtypeStruct((m, DIFF_H, DIFF_DV), F32)] * len(head_cols),
        scratch_shapes=[pltpu.VMEM((tm, d), BF16)],
        compiler_params=_params(("arbitrary", "arbitrary"), vmem_mib),
        name="norm_matmul",
    )(x, g, w)
    return outs if head_cols else outs[0]


def _rmsnorm_kernel(x_ref, g_ref, o_ref, *, eps):
    x = x_ref[...]
    o_ref[...] = x * lax.rsqrt(jnp.mean(x * x, axis=-1, keepdims=True) + eps) * g_ref[...]


def _pad_rows(x, mult):
    pad = -x.shape[0] % mult
    return jnp.pad(x, ((0, pad), (0, 0))) if pad else x


def _rmsnorm_rows(x, g):
    n = x.shape[0]
    xp = _pad_rows(x, 8)
    return pl.pallas_call(
        functools.partial(_rmsnorm_kernel, eps=1e-6),
        out_shape=jax.ShapeDtypeStruct(xp.shape, F32),
        name="rmsnorm_rows",
    )(xp, g)[:n]


def _diff_lambda(lq, lam_init):
    t1 = jnp.sum(lq[0:1] * lq[1:2], axis=-1, keepdims=True)
    t2 = jnp.sum(lq[2:3] * lq[3:4], axis=-1, keepdims=True)
    return jnp.exp(t1) - jnp.exp(t2) + lam_init


def _diff_finish(acc, l, lq, subln, rows, lam_init):
    o12 = acc / l
    o = o12[:rows] - _diff_lambda(lq, lam_init) * o12[rows:]
    o = o * lax.rsqrt(jnp.mean(o * o, axis=-1, keepdims=True) + 1e-5) * subln
    return o * (1.0 - lam_init)


LOG2E = math.log2(math.e)
POS_RADIX = 16
N_SLOPE_PIECES = 3
SCORE_LOOKAHEAD = 8


def _alibi_lanes(slopes, tq):
    assert tq <= POS_RADIX * 256
    c = np.asarray(slopes, np.float64) * LOG2E
    pieces, rest = [], c
    for _ in range(N_SLOPE_PIECES):
        piece = rest.astype(np.float32).astype(BF16).astype(np.float64)
        pieces.append(piece)
        rest = rest - piece
    q_lanes = np.zeros((len(c), 1, LANES), np.float32)
    k_lanes = np.zeros((tq, LANES), np.float32)
    pos = np.arange(tq)
    for n, piece in enumerate(pieces):
        q_lanes[:, 0, 2 * n] = POS_RADIX * piece
        q_lanes[:, 0, 2 * n + 1] = piece
        k_lanes[:, 2 * n] = pos // POS_RADIX
        k_lanes[:, 2 * n + 1] = pos % POS_RADIX
    c_sum = np.broadcast_to(sum(pieces)[:, None, None], (len(c), 1, LANES)).astype(np.float32)
    return jnp.asarray(q_lanes), jnp.asarray(k_lanes.astype(BF16)), jnp.asarray(c_sum)


def _stack_maps(q, q_lanes):
    q = q * (DIFF_DH ** -0.5 * LOG2E)
    lane = lax.broadcasted_iota(jnp.int32, q.shape, 1)
    q1 = jnp.where(lane < DIFF_DH, q, 0.0)
    q2 = jnp.where(lane >= DIFF_DH, q, 0.0)
    qq = jnp.concatenate([q1, q2], axis=0)
    return jnp.concatenate([qq, jnp.broadcast_to(q_lanes, qq.shape)], axis=1).astype(BF16)


def _diff_attn_kernel(qi_ref, kj_ref, q_ref, k_ref, v_ref, qlane_ref, klane_ref, slope_ref, lq_ref, subln_ref,
                      o_ref, qq_ref, m_ref, l_ref, acc_ref, *, tq, qw, lam_init):
    t = pl.program_id(2)
    i = qi_ref[t]
    j = kj_ref[t]

    @pl.when(j == 0)
    def _():
        qq_ref[...] = _stack_maps(q_ref[0], qlane_ref[0])
        m_ref[...] = jnp.full(m_ref.shape, NEG, F32)
        l_ref[...] = jnp.zeros(l_ref.shape, F32)
        acc_ref[...] = jnp.zeros(acc_ref.shape, F32)

    kb = jnp.concatenate([k_ref[0].astype(BF16), klane_ref[...]], axis=1)
    vb = v_ref[0].astype(BF16)
    block_off = slope_ref[0][:, :1] * jnp.full((1, 1), (j - i) * tq, jnp.int32).astype(F32)

    def accumulate(diagonal):
        starts = list(range(0, 2 * tq, qw))
        tiles = [pl.ds(c0, qw) for c0 in starts]
        n_keys = [c0 % tq + qw if diagonal else tq for c0 in starts]
        def score_tile(t):
            s = _dot(kb[:n_keys[t]], qq_ref[tiles[t], :], NT)
            if diagonal:
                qry = starts[t] % tq + lax.broadcasted_iota(jnp.int32, s.shape, 1)
                s = jnp.where(lax.broadcasted_iota(jnp.int32, s.shape, 0) <= qry, s, NEG)
            return s

        def softmax_tile(t, s):
            cols = tiles[t]
            m_prev = m_ref[:, cols] - block_off
            m_new = jnp.maximum(m_prev, jnp.max(s, axis=0, keepdims=True))
            alpha = jnp.exp2(m_prev - m_new)
            p = jnp.exp2(s - m_new)
            l_ref[:, cols] = alpha * l_ref[:, cols] + jnp.sum(p, axis=0, keepdims=True)
            acc_ref[:, cols] = alpha * acc_ref[:, cols] + _dot(vb[:n_keys[t]], p.astype(BF16), TN)
            m_ref[:, cols] = m_new + block_off

        scores = {}
        for t in range(len(tiles) + SCORE_LOOKAHEAD):
            if t < len(tiles):
                scores[t] = score_tile(t)
            if t >= SCORE_LOOKAHEAD:
                softmax_tile(t - SCORE_LOOKAHEAD, scores.pop(t - SCORE_LOOKAHEAD))

    @pl.when(j < i)
    def _():
        accumulate(False)

    @pl.when(j == i)
    def _():
        accumulate(True)
        o12 = acc_ref[...] / l_ref[...]
        o = o12[:, :tq] - _diff_lambda(lq_ref[...], lam_init) * o12[:, tq:]
        o = o * lax.rsqrt(jnp.mean(o * o, axis=0, keepdims=True) + 1e-5) * subln_ref[...]
        o_ref[0] = (o * (1.0 - lam_init)).T.astype(o_ref.dtype)


def _diff_attn(p3, slopes, lambda_qk, subln_col, *, tq, qw, lam_init):
    b, t, _ = p3.shape
    nq = t // tq
    q_lanes, k_lanes, c_sum = _alibi_lanes(slopes, tq)
    pairs = [(i, j) for i in range(nq) for j in range(i + 1)]
    qi = jnp.asarray(np.array([p[0] for p in pairs], np.int32))
    kj = jnp.asarray(np.array([p[1] for p in pairs], np.int32))
    qb, kb, vb = OFF_DQ // DIFF_DV, OFF_DK // DIFF_DV, OFF_DV // DIFF_DV
    grid_spec = pltpu.PrefetchScalarGridSpec(
        num_scalar_prefetch=2,
        grid=(b, DIFF_H, len(pairs)),
        in_specs=[pl.BlockSpec((1, tq, DIFF_DV), lambda bb, h, s, qi, kj: (bb, qi[s], qb + h)),
                  pl.BlockSpec((1, tq, DIFF_DV), lambda bb, h, s, qi, kj: (bb, kj[s], kb + h)),
                  pl.BlockSpec((1, tq, DIFF_DV), lambda bb, h, s, qi, kj: (bb, kj[s], vb + h)),
                  pl.BlockSpec((1, 1, LANES), lambda bb, h, s, qi, kj: (h, 0, 0)),
                  pl.BlockSpec((tq, LANES), lambda bb, h, s, qi, kj: (0, 0)),
                  pl.BlockSpec((1, 1, LANES), lambda bb, h, s, qi, kj: (h, 0, 0)),
                  pl.BlockSpec((4, DIFF_DH), lambda bb, h, s, qi, kj: (0, 0)),
                  pl.BlockSpec((DIFF_DV, 1), lambda bb, h, s, qi, kj: (0, 0))],
        out_specs=pl.BlockSpec((1, tq, DIFF_DV), lambda bb, h, s, qi, kj: (bb, qi[s], h)),
        scratch_shapes=[pltpu.VMEM((2 * tq, 2 * DIFF_DV), BF16), pltpu.VMEM((1, 2 * tq), F32),
                        pltpu.VMEM((1, 2 * tq), F32), pltpu.VMEM((DIFF_DV, 2 * tq), F32)],
    )
    return pl.pallas_call(
        functools.partial(_diff_attn_kernel, tq=tq, qw=qw, lam_init=lam_init),
        grid_spec=grid_spec,
        out_shape=jax.ShapeDtypeStruct((b, t, DIFF_H * DIFF_DV), MIX_DTYPE),
        compiler_params=_params(("arbitrary", "arbitrary", "arbitrary"), 40),
        name="diff_attn",
    )(qi, kj, p3, p3, p3, q_lanes, k_lanes, c_sum, lambda_qk, subln_col)


def _paged_diff_attn_kernel(pt_ref, q_ref, *refs, n_pages, page, n_new, past_len, lam_init):
    k_refs = refs[:n_pages]
    v_refs = refs[n_pages:2 * n_pages]
    kn_ref, vn_ref, slope_ref, lq_ref, subln_ref, o_ref, m_ref, l_ref, acc_ref = refs[2 * n_pages:]
    s_id = pl.program_id(1)
    rows = 2 * DIFF_H * n_new

    @pl.when(s_id == 0)
    def _():
        m_ref[...] = jnp.full(m_ref.shape, NEG, F32)
        l_ref[...] = jnp.zeros(l_ref.shape, F32)
        acc_ref[...] = jnp.zeros(acc_ref.shape, F32)

    qb = (q_ref[0] * (DIFF_DH ** -0.5)).astype(BF16)
    slope = slope_ref[...][:, :1]
    log_new, log_h = n_new.bit_length() - 1, DIFF_H.bit_length() - 1

    def coords(ncol):
        row = lax.broadcasted_iota(jnp.int32, (rows, ncol), 0)
        col = lax.broadcasted_iota(jnp.int32, (rows, ncol), 1)
        row_h = lax.shift_right_logical(row, log_new) & (DIFF_H - 1)
        return row & (n_new - 1), (col & (DIFF_H - 1)) == row_h, lax.shift_right_logical(col, log_h)

    def update(ss, vbs, carry):
        m_prev, l_prev, acc_prev = carry
        m_new = m_prev
        for s in ss:
            m_new = jnp.maximum(m_new, jnp.max(s, axis=-1, keepdims=True))
        alpha = jnp.exp(m_prev - m_new)
        l_new = alpha * l_prev
        acc_new = alpha * acc_prev
        for s, vb in zip(ss, vbs):
            p = jnp.exp(s - m_new)
            l_new = l_new + jnp.sum(p, axis=-1, keepdims=True)
            acc_new = acc_new + _dot(p.astype(BF16), vb)
        return m_new, l_new, acc_new

    _, same_head, tok = coords(page * DIFF_H)
    ss, vbs = [], []
    for i in range(n_pages):
        kb = k_refs[i][0].astype(BF16)
        vbs.append(v_refs[i][0].astype(BF16))
        k_pos = ((s_id * n_pages + i) * page - past_len + tok).astype(F32)
        ss.append(jnp.where(same_head, _dot(qb, kb, NT) + slope * k_pos, NEG))
    m_ref[...], l_ref[...], acc_ref[...] = update(ss, vbs, (m_ref[...], l_ref[...], acc_ref[...]))

    @pl.when(s_id == pl.num_programs(1) - 1)
    def _():
        kb = kn_ref[0].astype(BF16)
        vb = vn_ref[0].astype(BF16)
        row_tn, same_head_n, tok_n = coords(n_new * DIFF_H)
        k_pos = tok_n.astype(F32)
        s = jnp.where(same_head_n & (tok_n <= row_tn), _dot(qb, kb, NT) + slope * k_pos, NEG)
        _, l, acc = update([s], [vb], (m_ref[...], l_ref[...], acc_ref[...]))
        o_ref[0] = _diff_finish(acc, l, lq_ref[...], subln_ref[...], rows // 2, lam_init)


def _paged_diff_attn(qm, cache_k, cache_v, k_new, v_new, page_table, slope_rows, lambda_qk, subln, *,
                     n_pages, lam_init):
    db, rows, _ = qm.shape
    n_pool, page = cache_k.shape[0], cache_k.shape[1] // DIFF_H
    n_tab = page_table.shape[1]
    n_new = rows // (2 * DIFF_H)
    assert n_tab % n_pages == 0 and n_new & (n_new - 1) == 0 and DIFF_H & (DIFF_H - 1) == 0

    def page_spec(i):
        return pl.BlockSpec((1, page * DIFF_H, DIFF_DV), lambda b, s, pt: (pt[b, s * n_pages + i], 0, 0))

    grid_spec = pltpu.PrefetchScalarGridSpec(
        num_scalar_prefetch=1,
        grid=(db, n_tab // n_pages),
        in_specs=([pl.BlockSpec((1, rows, DIFF_DV), lambda b, s, pt: (b, 0, 0))]
                  + [page_spec(i) for i in range(n_pages)] * 2
                  + [pl.BlockSpec((1, n_new * DIFF_H, DIFF_DV), lambda b, s, pt: (b, 0, 0)),
                     pl.BlockSpec((1, n_new * DIFF_H, DIFF_DV), lambda b, s, pt: (b, 0, 0)),
                     pl.BlockSpec((rows, LANES), lambda b, s, pt: (0, 0)),
                     pl.BlockSpec((4, DIFF_DH), lambda b, s, pt: (0, 0)),
                     pl.BlockSpec((1, DIFF_DV), lambda b, s, pt: (0, 0))]),
        out_specs=pl.BlockSpec((1, rows // 2, DIFF_DV), lambda b, s, pt: (b, 0, 0)),
        scratch_shapes=[pltpu.VMEM((rows, 1), F32), pltpu.VMEM((rows, 1), F32), pltpu.VMEM((rows, DIFF_DV), F32)],
    )
    return pl.pallas_call(
        functools.partial(_paged_diff_attn_kernel, n_pages=n_pages, page=page, n_new=n_new,
                          past_len=n_tab * page, lam_init=lam_init),
        grid_spec=grid_spec,
        out_shape=jax.ShapeDtypeStruct((db, rows // 2, DIFF_DV), F32),
        compiler_params=_params(("arbitrary", "arbitrary"), 16 + 2 * n_pages * page * DIFF_H * DIFF_DV * 8 // MIB),
        name="paged_diff_attn",
    )(page_table, qm, *([cache_k] * n_pages), *([cache_v] * n_pages), k_new, v_new, slope_rows, lambda_qk, subln)


def _cross_attn_kernel(q_ref, k_ref, v_ref, o_ref, *, scale):
    for h in range(CROSS_H):
        cols = pl.ds(h * CROSS_DH, CROSS_DH)
        q = q_ref[0, :, cols].astype(BF16)
        k = k_ref[0, :, cols].astype(BF16)
        v = v_ref[0, :, cols].astype(BF16)
        s = _dot(q, k, NT) * scale
        p = jnp.exp(s - jnp.max(s, axis=-1, keepdims=True))
        l = jnp.sum(p, axis=-1, keepdims=True)
        o_ref[0, :, cols] = (_dot(p.astype(BF16), v) / l).astype(o_ref.dtype)


def _cross_attn(q_arr, q_off, k_arr, k_off, v_arr, v_off, *, tq):
    b, t, _ = q_arr.shape
    n_mem = k_arr.shape[1]
    width = CROSS_H * CROSS_DH
    qb, kb, vb = q_off // width, k_off // width, v_off // width
    return pl.pallas_call(
        functools.partial(_cross_attn_kernel, scale=CROSS_DH ** -0.5),
        grid=(b, t // tq),
        in_specs=[pl.BlockSpec((1, tq, width), lambda bb, i: (bb, i, qb)),
                  pl.BlockSpec((1, n_mem, width), lambda bb, i: (bb, 0, kb)),
                  pl.BlockSpec((1, n_mem, width), lambda bb, i: (bb, 0, vb))],
        out_specs=pl.BlockSpec((1, tq, width), lambda bb, i: (bb, i, 0)),
        out_shape=jax.ShapeDtypeStruct((b, t, width), MIX_DTYPE),
        compiler_params=_params(("arbitrary", "arbitrary"), 32),
        name="cross_attn",
    )(q_arr, k_arr, v_arr)


def _rwkv_step(r_ref, k_ref, v_ref, lo_ref, fr_ref, fk_ref, fv_ref, fl_ref,
               mur_ref, muk_ref, muv_ref, mul_ref, w0_ref, w2_ref, a0_ref, a2_ref, kk_ref, ka_ref, rk_ref,
               s0_ref, g_ref, bb_ref, y_ref, sout_ref,
               lhs_s, rhs_s, vm_s, gend_s, bonus_s, s_ref, cr_ref, ck_ref, cv_ref, cl_ref,
               *, chunk, n_pairs, nb, n_chunks, valid, slot_w, slot_r, ahead):
    c = pl.program_id(1)
    c2 = 2 * chunk

    rid = lax.broadcasted_iota(jnp.int32, (c2, c2), 0)
    cid = lax.broadcasted_iota(jnp.int32, (c2, c2), 1)
    strict = rid > cid
    incl2 = jnp.concatenate([rid >= cid, rid >= cid], axis=1)
    tr = lax.broadcasted_iota(jnp.int32, (chunk, chunk), 0)
    tc = lax.broadcasted_iota(jnp.int32, (chunk, chunk), 1)
    tri = (tr >= tc).astype(BF16)
    srow = lax.broadcasted_iota(jnp.int32, (c2, LANES), 0)
    slane = lax.broadcasted_iota(jnp.int32, (c2, LANES), 1)
    own = (srow < chunk) == (slane < RWKV_N)
    low_half = lax.broadcasted_iota(jnp.int32, (chunk, LANES), 1) < RWKV_N

    def stack(x):
        return jnp.where(own, jnp.concatenate([x, x], axis=0), 0.0).astype(BF16)

    def head_sum(x):
        s_lo = jnp.sum(jnp.where(low_half, x, 0.0), axis=-1, keepdims=True)
        s_hi = jnp.sum(jnp.where(low_half, 0.0, x), axis=-1, keepdims=True)
        return jnp.where(low_half, s_lo, s_hi)

    row_id = lax.broadcasted_iota(jnp.int32, (chunk, 1), 0)

    def prepare():
        for bi in range(nb):
            def shift_mix(x_ref, carry_ref, mu_ref):
                x = x_ref[bi]
                prev = pltpu.roll(x, 1, axis=0)
                prev = jnp.where(lax.broadcasted_iota(jnp.int32, x.shape, 0) == 0, carry_ref[bi], prev)
                carry_ref[bi] = x[chunk - 1:chunk, :]
                return x + mu_ref[...] * (prev - x)

            r = shift_mix(r_ref, cr_ref, mur_ref)
            k = shift_mix(k_ref, ck_ref, muk_ref)
            v = shift_mix(v_ref, cv_ref, muv_ref)
            lo = shift_mix(lo_ref, cl_ref, mul_ref)
            yield
            xw = -(w0_ref[...] + _dot(jnp.tanh(lo).astype(BF16), w2_ref[...]))
            softplus = jnp.maximum(xw, 0.0) + jnp.log(1.0 + jnp.exp(-jnp.abs(xw)))
            lw = -jnp.exp(-softplus - 0.5)
            a = jax.nn.sigmoid(a0_ref[...] + _dot(lo.astype(BF16), a2_ref[...]))
            kkr = k * kk_ref[...]
            k2 = k * (1.0 + (a - 1.0) * ka_ref[...])
            rb = r * k2 * rk_ref[...]
            if valid is not None:
                ok = jnp.minimum(c, n_chunks - 1) * chunk + row_id < valid
                lw, r, k2, v, kkr, rb = [jnp.where(ok, t, 0.0) for t in (lw, r, k2, v, kkr, rb)]
            yield
            for pi in range(n_pairs):
                cols = slice(pi * LANES, (pi + 1) * LANES)
                lw_p, kkr_p, v_p = lw[:, cols], kkr[:, cols], v[:, cols]
                kk = kkr_p / jnp.maximum(jnp.sqrt(head_sum(kkr_p * kkr_p)), 1e-12)
                h1 = lw_p.astype(BF16)
                r1 = lw_p - h1.astype(F32)
                h2 = r1.astype(BF16)
                h3 = (r1 - h2.astype(F32)).astype(BF16)
                cum = _dot(tri, h1) + (_dot(tri, h2) + _dot(tri, h3))
                g = jnp.exp(cum)
                gi = jnp.exp(-cum)
                gp = jnp.exp(cum - lw_p)
                lhs_s[slot_w, bi, pi] = jnp.concatenate([stack(-kk * gp), stack(r[:, cols] * g)], axis=0)
                rhs_s[slot_w, bi, pi] = jnp.concatenate([stack(kk * a[:, cols] * gi), stack(k2[:, cols] * gi)], axis=0)
                vm_s[slot_w, bi, pi] = stack(v_p)
                gend_s[slot_w, bi, :, cols] = g[chunk - 1:chunk, :]
                bonus_s[slot_w, bi, :, cols] = head_sum(rb[:, cols]) * v_p
                yield

    pieces = prepare()
    n_pieces = nb * (n_pairs + 2)
    n_ticks = (chunk.bit_length() + 4) * nb * n_pairs
    progress = [0, 0]

    def stage(fn):
        out = []
        for u in n_units:
            out.append(fn(u))
            progress[0] += 1
            while progress[1] < progress[0] * n_pieces // n_ticks:
                next(pieces, None)
                progress[1] += 1
        return out

    if not ahead:
        for _ in pieces:
            pass

    units = [(bi, pi) for bi in range(nb) for pi in range(n_pairs)]
    lhs = [lhs_s[slot_r, bi, pi] for bi, pi in units]
    rhs = [rhs_s[slot_r, bi, pi] for bi, pi in units]
    vm = [vm_s[slot_r, bi, pi] for bi, pi in units]
    s_prev = [s_ref[bi, pi] for bi, pi in units]
    n_units = range(len(units))
    p1 = stage(lambda u: _dot(lhs[u], rhs[u], NT))
    p2 = stage(lambda u: _dot(lhs[u], s_prev[u].astype(BF16), NT))
    nm = [jnp.where(strict, p1[u][:c2, :c2], 0.0).astype(BF16) for u in n_units]
    aak = [jnp.where(strict, p1[u][:c2, c2:], 0.0).astype(BF16) for u in n_units]
    m2 = [jnp.where(incl2, p1[u][c2:, :], 0.0).astype(BF16) for u in n_units]
    uu = stage(lambda u: p2[u][:c2] + _dot(aak[u], vm[u]))
    n = 1
    while 2 * n < chunk:
        y = stage(lambda u: _dot(nm[u], jnp.concatenate([nm[u], uu[u].astype(BF16)], axis=1)))
        uu = [uu[u] + y[u][:, c2:] for u in n_units]
        nm = [y[u][:, :c2].astype(BF16) for u in n_units]
        n *= 2
    uu = stage(lambda u: uu[u] + _dot(nm[u], uu[u].astype(BF16)))
    uv = [jnp.concatenate([uu[u].astype(BF16), vm[u]], axis=0) for u in n_units]
    o = stage(lambda u: p2[u][c2:] + _dot(m2[u], uv[u]))
    s_new = stage(lambda u: s_prev[u] + _dot(uv[u], rhs[u], TN))
    for _ in pieces:
        pass
    for u, (bi, pi) in zip(n_units, units):
        s_ref[bi, pi] = s_new[u] * gend_s[slot_r, bi, :, pl.ds(pi * LANES, LANES)]
    for u, (bi, pi) in zip(n_units, units):
        cols = pl.ds(pi * LANES, LANES)
        mu = jnp.sum(o[u], axis=-1, keepdims=True) * (1.0 / RWKV_N)
        d = jnp.where(own, o[u] - mu, 0.0)
        var = jnp.sum(d * d, axis=-1, keepdims=True) * (1.0 / RWKV_N)
        yn = d * lax.rsqrt(var + RWKV_GN_EPS)
        y_ref[bi, :, cols] = ((yn[:chunk] + yn[chunk:]) * g_ref[:, cols] + bb_ref[:, cols]
                              + bonus_s[slot_r, bi, :, cols]).astype(y_ref.dtype)


RWKV_REFS = ("r_ref k_ref v_ref lo_ref fr_ref fk_ref fv_ref fl_ref mur_ref muk_ref muv_ref mul_ref w0_ref w2_ref a0_ref "
             "a2_ref kk_ref ka_ref rk_ref s0_ref g_ref bb_ref y_ref sout_ref lhs_s rhs_s vm_s gend_s bonus_s s_ref "
             "cr_ref ck_ref cv_ref cl_ref").split()


def _rwkv_kernel(*refs, n_chunks, **static):
    named = dict(zip(RWKV_REFS, refs, strict=True))
    fr_ref, fk_ref, fv_ref, fl_ref = (named[n] for n in ("fr_ref", "fk_ref", "fv_ref", "fl_ref"))
    cr_ref, ck_ref, cv_ref, cl_ref = (named[n] for n in ("cr_ref", "ck_ref", "cv_ref", "cl_ref"))
    lhs_s, rhs_s, vm_s, gend_s, bonus_s = (named[n] for n in ("lhs_s", "rhs_s", "vm_s", "gend_s", "bonus_s"))
    s0_ref, s_ref, sout_ref = named["s0_ref"], named["s_ref"], named["sout_ref"]
    c = pl.program_id(1)
    ahead = n_chunks > 1

    @pl.when(c == 0)
    def _():
        cr_ref[...] = fr_ref[...]
        ck_ref[...] = fk_ref[...]
        cv_ref[...] = fv_ref[...]
        cl_ref[...] = fl_ref[...]
        s_ref[...] = s0_ref[...]
        if ahead:
            lhs_s[1] = jnp.zeros(lhs_s.shape[1:], BF16)
            rhs_s[1] = jnp.zeros(rhs_s.shape[1:], BF16)
            vm_s[1] = jnp.zeros(vm_s.shape[1:], BF16)
            gend_s[1] = jnp.ones(gend_s.shape[1:], F32)
            bonus_s[1] = jnp.zeros(bonus_s.shape[1:], F32)

    if ahead:
        @pl.when(lax.rem(c, 2) == 0)
        def _():
            _rwkv_step(*refs, n_chunks=n_chunks, slot_w=0, slot_r=1, ahead=True, **static)

        @pl.when(lax.rem(c, 2) == 1)
        def _():
            _rwkv_step(*refs, n_chunks=n_chunks, slot_w=1, slot_r=0, ahead=True, **static)
    else:
        _rwkv_step(*refs, n_chunks=n_chunks, slot_w=0, slot_r=0, ahead=False, **static)

    @pl.when(c == (n_chunks if ahead else 0))
    def _():
        sout_ref[...] = s_ref[...]


def _rwkv(p3, firsts, mus, w0, w2b, a0, a2b, k_k, k_a, r_k, s0_pairs, ln_g, ln_b, *, nb, valid=None):
    bsz, t, _ = p3.shape
    n_pairs = RWKV_H // 2
    n_chunks = t // CHUNK
    assert bsz % nb == 0 and t % CHUNK == 0
    cur = lambda c: jnp.minimum(c, n_chunks - 1)
    wide = lambda blk: pl.BlockSpec((nb, CHUNK, RWKV_W), lambda bb, c: (bb, cur(c), blk))
    vec = lambda n: pl.BlockSpec((1, n), lambda bb, c: (0, 0))
    first = lambda n: pl.BlockSpec((nb, 1, n), lambda bb, c: (bb, 0, 0))
    st = pl.BlockSpec((nb, n_pairs, LANES, LANES), lambda bb, c: (bb, 0, 0, 0))
    in_specs = [wide(OFF_R // RWKV_W), wide(OFF_K // RWKV_W), wide(OFF_V // RWKV_W),
                pl.BlockSpec((nb, CHUNK, LANES), lambda bb, c: (bb, cur(c), OFF_LORA // LANES)),
                first(RWKV_W), first(RWKV_W), first(RWKV_W), first(LANES),
                vec(RWKV_W), vec(RWKV_W), vec(RWKV_W), vec(LANES),
                vec(RWKV_W), pl.BlockSpec((LANES, RWKV_W), lambda bb, c: (0, 0)),
                vec(RWKV_W), pl.BlockSpec((LANES, RWKV_W), lambda bb, c: (0, 0)),
                vec(RWKV_W), vec(RWKV_W), vec(RWKV_W),
                st, vec(RWKV_W), vec(RWKV_W)]
    lag = 1 if n_chunks > 1 else 0
    out_specs = [pl.BlockSpec((nb, CHUNK, RWKV_W), lambda bb, c: (bb, jnp.maximum(c - lag, 0), 0)), st]
    scratch = [pltpu.VMEM((2, nb, n_pairs, 4 * CHUNK, LANES), BF16), pltpu.VMEM((2, nb, n_pairs, 4 * CHUNK, LANES), BF16),
               pltpu.VMEM((2, nb, n_pairs, 2 * CHUNK, LANES), BF16), pltpu.VMEM((2, nb, 1, RWKV_W), F32),
               pltpu.VMEM((2, nb, CHUNK, RWKV_W), F32), pltpu.VMEM((nb, n_pairs, LANES, LANES), F32),
               pltpu.VMEM((nb, 1, RWKV_W), F32), pltpu.VMEM((nb, 1, RWKV_W), F32),
               pltpu.VMEM((nb, 1, RWKV_W), F32), pltpu.VMEM((nb, 1, LANES), F32)]
    return pl.pallas_call(
        functools.partial(_rwkv_kernel, chunk=CHUNK, n_pairs=n_pairs, nb=nb, n_chunks=n_chunks, valid=valid),
        grid=(bsz // nb, n_chunks + lag),
        in_specs=in_specs,
        out_specs=out_specs,
        out_shape=[jax.ShapeDtypeStruct((bsz, t, RWKV_W), MIX_DTYPE),
                   jax.ShapeDtypeStruct((bsz, n_pairs, LANES, LANES), F32)],
        scratch_shapes=scratch,
        compiler_params=_params(("arbitrary", "arbitrary"), 48),
        name="rwkv",
    )(p3, p3, p3, p3, *firsts, *mus, w0, w2b, a0, a2b, k_k, k_a, r_k, s0_pairs, ln_g, ln_b)


def _pair_states(s):
    bsz = s.shape[0]
    s = s.reshape(bsz, RWKV_H // 2, 2, RWKV_N, RWKV_N)
    z = jnp.zeros_like(s[:, :, 0])
    top = jnp.concatenate([s[:, :, 0], z], axis=-1)
    bot = jnp.concatenate([z, s[:, :, 1]], axis=-1)
    return jnp.concatenate([top, bot], axis=-2)


def _unpair_states(sp):
    bsz = sp.shape[0]
    even = sp[:, :, :RWKV_N, :RWKV_N]
    odd = sp[:, :, RWKV_N:, RWKV_N:]
    return jnp.stack([even, odd], axis=2).reshape(bsz, RWKV_H, RWKV_N, RWKV_N)


def _merge_out_kernel(x_ref, d_ref, r_ref, c_ref, gate_ref, w_ref, g_ref, o_ref, *, eps):
    gate = gate_ref[...]
    sg = gate * jax.nn.sigmoid(gate)
    nd, nr = d_ref.shape[1], r_ref.shape[1]
    acc = _dot((d_ref[...] * sg[:, :nd]).astype(BF16), w_ref[0:nd, :])
    acc += _dot((r_ref[...] * sg[:, nd:nd + nr]).astype(BF16), w_ref[nd:nd + nr, :])
    acc += _dot((c_ref[...] * sg[:, nd + nr:]).astype(BF16), w_ref[nd + nr:, :])
    x = x_ref[...] + acc
    o_ref[...] = x * lax.rsqrt(jnp.mean(x * x, axis=-1, keepdims=True) + eps) * g_ref[...]


def _merge_out(x, d_o, r_o, c_o, p2, w_out, norm_out, *, tm):
    m, d = x.shape
    row = lambda n: pl.BlockSpec((tm, n), lambda i: (i, 0))
    return pl.pallas_call(
        functools.partial(_merge_out_kernel, eps=1e-6),
        grid=(m // tm,),
        in_specs=[row(d), row(d_o.shape[1]), row(r_o.shape[1]), row(c_o.shape[1]),
                  pl.BlockSpec((tm, d), lambda i: (i, OFF_GATE // d)),
                  pl.BlockSpec((d, d), lambda i: (0, 0)),
                  pl.BlockSpec((1, d), lambda i: (0, 0))],
        out_specs=row(d),
        out_shape=jax.ShapeDtypeStruct((m, d), F32),
        compiler_params=_params(("arbitrary",), 56),
        name="merge_out",
    )(x, d_o, r_o, c_o, p2, w_out, norm_out)


REORDER_COLS = 512


def _cast_kernel(offs_ref, src_ref, dst_ref):
    dst_ref[...] = src_ref[...].astype(BF16)


def _reorder_in_cols(w):
    d = w.shape[0]
    dq, z, cq, gate = 0, 1536, 1536 + 3200, 1536 + 3200 + 512
    runs = [(gate, IN_COLS - gate), (z, 3 * RWKV_W), (dq, z), (cq, gate - cq), (z + 3 * RWKV_W, 2 * LORA)]
    offs = []
    for start, width in runs:
        offs += [start + c for c in range(0, width, REORDER_COLS)]
    assert all(wd % REORDER_COLS == 0 for _, wd in runs[:-1]) and offs[-1] + REORDER_COLS <= IN_COLS
    grid_spec = pltpu.PrefetchScalarGridSpec(
        num_scalar_prefetch=1,
        grid=(len(offs),),
        in_specs=[pl.BlockSpec((pl.Element(d), pl.Element(REORDER_COLS)), lambda i, offs: (0, offs[i] * LANES))],
        out_specs=pl.BlockSpec((d, REORDER_COLS), lambda i, offs: (0, i)),
    )
    return pl.pallas_call(
        _cast_kernel,
        grid_spec=grid_spec,
        out_shape=jax.ShapeDtypeStruct((d, IN_COLS), BF16),
        compiler_params=_params(("arbitrary",), 24),
        name="reorder_cast",
    )(jnp.asarray(np.array(offs, np.int32) // LANES), w)


def kernel(x_prompt, x_sample, cache_k, cache_v, cache_mem_k, cache_mem_v, state_rwkv, state_shift, page_table,
           mem_prompt, norm_in, w_in, norm_mem, w_mem_kv, lambda_qk, diff_subln, rwkv_mu, rwkv_w0, rwkv_w2, rwkv_a0,
           rwkv_a2, rwkv_k_k, rwkv_k_a, rwkv_r_k, rwkv_ln_g, rwkv_ln_b, w_out, norm_out):
    bsz, seq, d = x_prompt.shape
    db, ds, _ = x_sample.shape
    depth = w_in.shape[0]
    assert depth == 1
    l = 0
    lam_init = 0.8 - 0.6 * math.exp(-0.3 * l)
    n_mem = mem_prompt.shape[1]

    w_in_b = _reorder_in_cols(w_in[l])
    w_mem_b = w_mem_kv[l].astype(BF16)
    w_out_b = w_out[l].astype(BF16)
    g_in = norm_in[l][None]
    g_mem = norm_mem[l][None]
    g_out = norm_out[None]
    mu = rwkv_mu[l]
    mus = (mu[None, :RWKV_W], mu[None, RWKV_W:2 * RWKV_W], mu[None, 2 * RWKV_W:3 * RWKV_W], mu[None, 3 * RWKV_W:])
    zl = jnp.zeros((LORA, RWKV_W), F32)
    w2b = jnp.concatenate([rwkv_w2[l], zl], axis=0).astype(BF16)
    a2b = jnp.concatenate([zl, rwkv_a2[l]], axis=0).astype(BF16)
    vec = lambda x: x.reshape(1, RWKV_W)
    w0, a0, k_k, k_a, r_k = vec(rwkv_w0[l]), vec(rwkv_a0[l]), vec(rwkv_k_k[l]), vec(rwkv_k_a[l]), vec(rwkv_r_k[l])
    ln_g, ln_b = vec(rwkv_ln_g[l]), vec(rwkv_ln_b[l])
    slopes = 2.0 ** (-8.0 * np.arange(1, DIFF_H + 1, dtype=np.float64) / DIFF_H)
    subln = diff_subln[l][None]
    lq = lambda_qk[l]

    xp2 = x_prompt.reshape(bsz * seq, d)
    p2, k4_p, v4_p = _norm_matmul(xp2, g_in, w_in_b, tm=512, tn=IN_COLS // 3, vmem_mib=60,
                                  head_cols=(OFF_DK, OFF_DV))
    p3 = p2.reshape(bsz, seq, IN_COLS)
    hs_p = _rmsnorm_rows(x_prompt[:, -1], g_in)
    d_o = _diff_attn(p3, slopes, lq, subln.reshape(DIFF_DV, 1), tq=1024, qw=256, lam_init=lam_init)
    zeros_first = (jnp.zeros((bsz, 1, RWKV_W), F32),) * 3 + (jnp.zeros((bsz, 1, LANES), F32),)
    s0_p = jnp.zeros((bsz, RWKV_H // 2, LANES, LANES), F32)
    r_o, sp_p = _rwkv(p3, zeros_first, mus, w0, w2b, a0, a2b, k_k, k_a, r_k, s0_p, ln_g, ln_b, nb=2)
    mem_kv = _norm_matmul(mem_prompt.reshape(bsz * n_mem, d), g_mem, w_mem_b, tm=256, tn=512,
                          vmem_mib=32).reshape(bsz, n_mem, 2 * CROSS_H * CROSS_DH)
    c_o = _cross_attn(p3, OFF_CQ, mem_kv, 0, mem_kv, CROSS_H * CROSS_DH, tq=1024)
    y_p = _merge_out(xp2, d_o.reshape(bsz * seq, -1), r_o.reshape(bsz * seq, -1), c_o.reshape(bsz * seq, -1), p2,
                     w_out_b, g_out, tm=512).reshape(bsz, seq, d)
    nk_p = k4_p.reshape(1, bsz, seq, DIFF_H, DIFF_DV)
    nv_p = v4_p.reshape(1, bsz, seq, DIFF_H, DIFF_DV)
    nmk_p = mem_kv[:, :, :CROSS_H * CROSS_DH].reshape(1, bsz, n_mem, CROSS_H, CROSS_DH)
    nmv_p = mem_kv[:, :, CROSS_H * CROSS_DH:].reshape(1, bsz, n_mem, CROSS_H, CROSS_DH)
    ns_p = _unpair_states(sp_p)[None]

    xs2 = x_sample.reshape(db * ds, d)
    rows = _pad_rows(jnp.concatenate([xs2, state_shift[l]], axis=0), 16)
    ps_all = _norm_matmul(rows, g_in, w_in_b, tm=rows.shape[0], tn=IN_COLS // 3, vmem_mib=56, n_norm=db * ds)
    ps2 = ps_all[:db * ds]
    ps3 = ps2.reshape(db, ds, IN_COLS)
    hs_s = _rmsnorm_rows(x_sample[:, -1], g_in)
    z_first = ps_all[db * ds:db * ds + db, None, :]
    firsts = (z_first[..., OFF_R:OFF_K], z_first[..., OFF_K:OFF_V], z_first[..., OFF_V:OFF_DQ], z_first[..., OFF_LORA:])

    q5 = ps3[:, :, OFF_DQ:OFF_DK].reshape(db, ds, DIFF_H, 2, DIFF_DH)
    zq = jnp.zeros_like(q5[..., 0, :])
    qm = jnp.stack([jnp.concatenate([q5[..., 0, :], zq], -1), jnp.concatenate([zq, q5[..., 1, :]], -1)], axis=1)
    qm = qm.transpose(0, 1, 3, 2, 4).reshape(db, 2 * DIFF_H * ds, DIFF_DV)
    k_new = ps3[:, :, OFF_DK:OFF_DV].reshape(db, ds * DIFF_H, DIFF_DV)
    v_new = ps3[:, :, OFF_DV:OFF_CQ].reshape(db, ds * DIFF_H, DIFF_DV)
    row_head = (np.arange(2 * DIFF_H * ds) // ds) % DIFF_H
    slope_rows = jnp.asarray(np.broadcast_to(slopes[row_head][:, None], (2 * DIFF_H * ds, LANES)).astype(np.float32))
    n_pool, page = cache_k.shape[1], cache_k.shape[2]
    ck = cache_k[l].reshape(n_pool, page * DIFF_H, DIFF_DV)
    cv = cache_v[l].reshape(n_pool, page * DIFF_H, DIFF_DV)
    ds_o = _paged_diff_attn(qm, ck, cv, k_new, v_new, page_table, slope_rows, lq, subln, n_pages=32,
                            lam_init=lam_init)
    ds_o = ds_o.reshape(db, DIFF_H, ds, DIFF_DV).transpose(0, 2, 1, 3).reshape(db * ds, DIFF_H * DIFF_DV)

    ps3_pad = jnp.pad(ps3, ((0, 0), (0, CHUNK - ds), (0, 0)))
    rs_o, sp_s = _rwkv(ps3_pad, firsts, mus, w0, w2b, a0, a2b, k_k, k_a, r_k, _pair_states(state_rwkv[l]), ln_g, ln_b,
                       nb=2, valid=ds)
    rs_o = rs_o[:, :ds].reshape(db * ds, RWKV_W)

    cq_pad = jnp.pad(ps3[:, :, OFF_CQ:OFF_LORA], ((0, 0), (0, 16 - ds), (0, 0)))
    mk_s = cache_mem_k[l].reshape(db, n_mem, CROSS_H * CROSS_DH)
    mv_s = cache_mem_v[l].reshape(db, n_mem, CROSS_H * CROSS_DH)
    cs_o = _cross_attn(cq_pad, 0, mk_s, 0, mv_s, 0, tq=16)[:, :ds].reshape(db * ds, CROSS_H * CROSS_DH)
    y_s = _merge_out(xs2, ds_o, rs_o, cs_o, ps2, w_out_b, g_out, tm=db * ds).reshape(db, ds, d)
    nk_s = ps3[:, :, OFF_DK:OFF_DV].reshape(1, db, ds, DIFF_H, DIFF_DV)
    nv_s = ps3[:, :, OFF_DV:OFF_CQ].reshape(1, db, ds, DIFF_H, DIFF_DV)
    ns_s = _unpair_states(sp_s)[None]

    return (y_p, y_s, nk_p, nv_p, nmk_p, nmv_p, ns_p, hs_p[None], nk_s, nv_s, ns_s, hs_s[None])
```
